```python
import math
import jax, jax.numpy as jnp
from jax import lax
import numpy as np

D_MODEL = 1024
BATCH = 16
SEQ = 256
DEPTH = 1
DEC_BATCH = 8
DEC_SEQ = 2048
PAST_LEN = 256

GRID_W = 64
N_HEADS = 4
D_QK = 64
D_V = 128
ATTN_W = N_HEADS * D_V
CONV_W = D_MODEL - ATTN_W
CONV_K = 3
IN_W = 3 * ATTN_W + 3 * CONV_W
N_GROUPS = 4
EXP_PER_GROUP = 4
N_EXPERTS = N_GROUPS * EXP_PER_GROUP
TOP_K = 2
D_EXPERT = 512
ROPE_BASE = 10000.0
Q_BLOCK = 128
EPS = 1e-6

kernel_name = "hymba_diffattn_shortconv_hiermoe_dit_step"


def rmsnorm(x, g):
    xf = x.astype(jnp.float32)
    y = xf * lax.rsqrt(jnp.mean(xf * xf, axis=-1, keepdims=True) + EPS)
    return (y * g.astype(jnp.float32)).astype(x.dtype)


def adaln(cond, w_mod, b_mod):
    m = jnp.einsum('nd,de->ne', jax.nn.silu(cond), w_mod) + b_mod
    return jnp.split(m[:, None, :], 6, axis=-1)


def axial_rope_tables(L):
    n_rows = L // GRID_W
    row = jnp.repeat(jnp.arange(n_rows), GRID_W).astype(jnp.float32)
    col = jnp.tile(jnp.arange(GRID_W), n_rows).astype(jnp.float32)
    nf = D_QK // 4
    inv = ROPE_BASE ** (-jnp.arange(nf, dtype=jnp.float32) / nf)
    ar = row[:, None] * inv
    ac = col[:, None] * inv
    ang = jnp.concatenate([ar, ar, ac, ac], axis=-1)
    return jnp.cos(ang), jnp.sin(ang)


def apply_axial_rope(x, cos, sin):
    x4 = x.reshape(x.shape[:-1] + (2, 2, D_QK // 4))
    rot = jnp.stack([-x4[..., 1, :], x4[..., 0, :]], axis=-2).reshape(x.shape)
    c = cos[:, None, :]
    s = sin[:, None, :]
    return (x.astype(jnp.float32) * c + rot.astype(jnp.float32) * s).astype(x.dtype)


def diff_attention(q, k, v, lam):
    B, H, Lq, _ = q.shape
    nb = Lq // Q_BLOCK
    scale = 1.0 / math.sqrt(D_QK)
    qb = q.reshape(B, H, nb, Q_BLOCK, 2 * D_QK).transpose(2, 0, 1, 3, 4)
    k1, k2 = k[..., :D_QK], k[..., D_QK:]

    def block(qi):
        s1 = jnp.einsum('bhqd,bhkd->bhqk', qi[..., :D_QK], k1).astype(jnp.float32) * scale
        s2 = jnp.einsum('bhqd,bhkd->bhqk', qi[..., D_QK:], k2).astype(jnp.float32) * scale
        p = jax.nn.softmax(s1, axis=-1) - lam * jax.nn.softmax(s2, axis=-1)
        return jnp.einsum('bhqk,bhkd->bhqd', p.astype(v.dtype), v)

    o = lax.map(block, qb)
    return o.transpose(1, 2, 0, 3, 4).reshape(B, H, Lq, D_V)


def short_conv(u, w):
    return lax.conv_general_dilated(
        u, w[:, None, :].astype(u.dtype), window_strides=(1,), padding=((1, 1),),
        dimension_numbers=('NWC', 'WIO', 'NWC'), feature_group_count=u.shape[-1])


def token_mixer(h, p, lam, lam_init, rope, ctx_kv):
    B, L, _ = h.shape
    z = jnp.einsum('bld,de->ble', h, p['w_in'])
    q, k, v, gb, gc, u = jnp.split(
        z, [ATTN_W, 2 * ATTN_W, 3 * ATTN_W, 3 * ATTN_W + CONV_W, 3 * ATTN_W + 2 * CONV_W], axis=-1)
    q = q.reshape(B, L, N_HEADS, 2, D_QK).transpose(0, 2, 1, 3, 4)
    k = k.reshape(B, L, N_HEADS, 2, D_QK).transpose(0, 2, 1, 3, 4)
    v = v.reshape(B, L, N_HEADS, D_V).transpose(0, 2, 1, 3)
    if rope is not None:
        q = apply_axial_rope(q, rope[0], rope[1])
        k = apply_axial_rope(k, rope[0], rope[1])
    q = q.reshape(B, N_HEADS, L, 2 * D_QK)
    k = k.reshape(B, N_HEADS, L, 2 * D_QK)
    if ctx_kv is not None:
        k_all = jnp.concatenate([ctx_kv[0].astype(k.dtype), k], axis=2)
        v_all = jnp.concatenate([ctx_kv[1].astype(v.dtype), v], axis=2)
    else:
        k_all, v_all = k, v
    o = diff_attention(q, k_all, v_all, lam)
    o = rmsnorm(o, p['subln_g']) * (1.0 - lam_init)
    o = o.transpose(0, 2, 1, 3).reshape(B, L, ATTN_W)
    y_conv = gb * short_conv(gc * u, p['conv_w'])
    out = jnp.einsum('ble,ed->bld', jnp.concatenate([o, y_conv], axis=-1), p['w_o'])
    return out, k, v


def hier_moe(h, p):
    B, L, D = h.shape
    t = h.reshape(B * L, D)
    pg = jax.nn.softmax(jnp.einsum('td,dg->tg', t, p['w_rg']).astype(jnp.float32), axis=-1)
    g_sel = jnp.argmax(pg, axis=-1)
    p_sel = jnp.max(pg, axis=-1, keepdims=True)
    le = jnp.einsum('td,de->te', t, p['w_re']).astype(jnp.float32).reshape(-1, N_GROUPS, EXP_PER_GROUP)
    le_sel = jnp.einsum('tg,tge->te', jax.nn.one_hot(g_sel, N_GROUPS, dtype=jnp.float32), le)
    wt, idx = lax.top_k(jax.nn.softmax(le_sel, axis=-1), TOP_K)
    wt = wt / jnp.sum(wt, axis=-1, keepdims=True) * p_sel
    eidx = g_sel[:, None] * EXP_PER_GROUP + idx
    gate = jnp.sum(jax.nn.one_hot(eidx, N_EXPERTS, dtype=jnp.float32) * wt[..., None], axis=1)
    y = jnp.zeros((B * L, D), jnp.float32)
    for e in range(N_EXPERTS):
        a = jax.nn.silu(t @ p['w_eg'][e]) * (t @ p['w_eu'][e])
        y = y + gate[:, e:e + 1] * (a @ p['w_ed'][e]).astype(jnp.float32)
    return y.astype(h.dtype).reshape(B, L, D)


def layer(x, mod, p, lam, lam_init, rope, ctx_kv):
    shift1, scale1, gate1, shift2, scale2, gate2 = mod
    h = rmsnorm(x, p['g_pre1']) * (1 + scale1) + shift1
    out, k, v = token_mixer(h, p, lam, lam_init, rope, ctx_kv)
    x = x + gate1 * rmsnorm(out, p['g_post1'])
    h = rmsnorm(x, p['g_pre2']) * (1 + scale2) + shift2
    x = x + gate2 * rmsnorm(hier_moe(h, p), p['g_post2'])
    return x, k, v


def setup_inputs(seed: int = 0) -> dict:
    key = jax.random.key(seed)
    ks = jax.random.split(key, 26)

    def nrm(k, shape, s):
        return jax.random.normal(k, shape, jnp.float32) * s

    D = D_MODEL
    return {
        "x_prompt": nrm(ks[0], (BATCH, SEQ, D), 1.0),
        "x_sample": nrm(ks[1], (DEC_BATCH, DEC_SEQ, D), 1.0),
        "cache_k": nrm(ks[2], (DEC_BATCH, DEPTH, N_HEADS, PAST_LEN, 2 * D_QK), 1.0),
        "cache_v": nrm(ks[3], (DEC_BATCH, DEPTH, N_HEADS, PAST_LEN, D_V), 1.0),
        "c": nrm(ks[4], (DEC_BATCH, D), 1.0),
        "c_ctx": nrm(ks[5], (D,), 1.0),
        "w_mod": nrm(ks[6], (DEPTH, D, 6 * D), 0.5 * D ** -0.5),
        "b_mod": nrm(ks[7], (DEPTH, 6 * D), 0.02),
        "g_pre1": 1.0 + nrm(ks[8], (DEPTH, D), 0.02),
        "g_post1": 1.0 + nrm(ks[9], (DEPTH, D), 0.02),
        "g_pre2": 1.0 + nrm(ks[10], (DEPTH, D), 0.02),
        "g_post2": 1.0 + nrm(ks[11], (DEPTH, D), 0.02),
        "w_in": nrm(ks[12], (DEPTH, D, IN_W), D ** -0.5),
        "conv_w": nrm(ks[13], (DEPTH, CONV_K, CONV_W), CONV_K ** -0.5),
        "lambda_q1": nrm(ks[14], (DEPTH, D_QK), 0.1),
        "lambda_k1": nrm(ks[15], (DEPTH, D_QK), 0.1),
        "lambda_q2": nrm(ks[16], (DEPTH, D_QK), 0.1),
        "lambda_k2": nrm(ks[17], (DEPTH, D_QK), 0.1),
        "subln_g": 1.0 + nrm(ks[18], (DEPTH, D_V), 0.02),
        "w_o": nrm(ks[19], (DEPTH, D, D), D ** -0.5),
        "w_router_group": nrm(ks[20], (DEPTH, D, N_GROUPS), D ** -0.5),
        "w_router_expert": nrm(ks[21], (DEPTH, D, N_EXPERTS), D ** -0.5),
        "w_exp_gate": nrm(ks[22], (DEPTH, N_EXPERTS, D, D_EXPERT), D ** -0.5),
        "w_exp_up": nrm(ks[23], (DEPTH, N_EXPERTS, D, D_EXPERT), D ** -0.5),
        "w_exp_down": nrm(ks[24], (DEPTH, N_EXPERTS, D_EXPERT, D), D_EXPERT ** -0.5),
    }


def reference(x_prompt, x_sample, cache_k, cache_v, c, c_ctx, w_mod, b_mod,
              g_pre1, g_post1, g_pre2, g_post2, w_in, conv_w,
              lambda_q1, lambda_k1, lambda_q2, lambda_k2, subln_g, w_o,
              w_router_group, w_router_expert, w_exp_gate, w_exp_up, w_exp_down):
    rope = axial_rope_tables(x_sample.shape[1])
    xp = x_prompt
    xs = x_sample
    new_k = []
    new_v = []
    for l in range(DEPTH):
        lam_init = 0.8 - 0.6 * math.exp(-0.3 * l)
        lam = (jnp.exp(jnp.sum(lambda_q1[l].astype(jnp.float32) * lambda_k1[l].astype(jnp.float32)))
               - jnp.exp(jnp.sum(lambda_q2[l].astype(jnp.float32) * lambda_k2[l].astype(jnp.float32)))
               + lam_init)
        p = {
            'w_in': w_in[l], 'w_o': w_o[l], 'conv_w': conv_w[l], 'subln_g': subln_g[l],
            'g_pre1': g_pre1[l], 'g_post1': g_post1[l], 'g_pre2': g_pre2[l], 'g_post2': g_post2[l],
            'w_rg': w_router_group[l], 'w_re': w_router_expert[l],
            'w_eg': w_exp_gate[l], 'w_eu': w_exp_up[l], 'w_ed': w_exp_down[l],
        }
        mod_ctx = adaln(c_ctx[None, :], w_mod[l], b_mod[l])
        mod_lat = adaln(c, w_mod[l], b_mod[l])
        xp, kp, vp = layer(xp, mod_ctx, p, lam, lam_init, None, None)
        new_k.append(kp)
        new_v.append(vp)
        xs, _, _ = layer(xs, mod_lat, p, lam, lam_init, rope, (cache_k[:, l], cache_v[:, l]))
    new_cache_k = jnp.stack(new_k, axis=1)
    new_cache_v = jnp.stack(new_v, axis=1)
    return (xp, xs, new_cache_k, new_cache_v)
```

```python
import functools
import math

import numpy as np
import jax
import jax.numpy as jnp
from jax import lax
from jax.experimental import pallas as pl
from jax.experimental.pallas import tpu as pltpu

D_MODEL = 1024
GRID_W = 64
N_HEADS = 4
D_QK = 64
D_V = 128
ATTN_W = N_HEADS * D_V
CONV_W = D_MODEL - ATTN_W
IN_W = 3 * ATTN_W + 3 * CONV_W
N_GROUPS = 4
EXP_PER_GROUP = 4
N_EXPERTS = N_GROUPS * EXP_PER_GROUP
D_EXPERT = 512
ROPE_BASE = 10000.0
EPS = 1e-6
LAM_INIT = 0.8 - 0.6 * math.exp(-0.3 * 0)

LANES = 128
BF16_SUBLANES = 16
MOD_ROWS = 16
TOKEN_TILE = 256
Q_TILE = 256
MOE_TILE = 1024
QK_SCALE = (1.0 / math.sqrt(D_QK)) * math.log2(math.e)

NT_DIMS = (((1,), (1,)), ((), ()))


def _rms(x):
    return x * lax.rsqrt(jnp.mean(x * x, axis=-1, keepdims=True) + EPS)


def _silu(x):
    return x * (1.0 / (1.0 + jnp.exp(-x)))


def _mod_kernel(cond_ref, w_ref, b_ref, lq1_ref, lk1_ref, lq2_ref, lk2_ref, mod_ref, lam_ref):
    s = _silu(cond_ref[...])
    m = lax.dot_general(s, w_ref[...], (((1,), (0,)), ((), ())),
                        precision=lax.Precision.HIGHEST, preferred_element_type=jnp.float32)
    mod_ref[...] = m + b_ref[...]
    a = jnp.sum(lq1_ref[...] * lk1_ref[...], axis=-1, keepdims=True)
    b = jnp.sum(lq2_ref[...] * lk2_ref[...], axis=-1, keepdims=True)
    lam_ref[...] = jnp.broadcast_to(jnp.exp(a) - jnp.exp(b) + LAM_INIT, lam_ref.shape)


def _mod_call(cond, w_mod, b_mod, lq1, lk1, lq2, lk2):
    n_col = 6 * D_MODEL
    col_tile = 1536
    small = pl.BlockSpec((1, D_QK), lambda j: (0, 0))
    return pl.pallas_call(
        _mod_kernel,
        grid=(n_col // col_tile,),
        in_specs=[
            pl.BlockSpec((MOD_ROWS, D_MODEL), lambda j: (0, 0)),
            pl.BlockSpec((D_MODEL, col_tile), lambda j: (0, j)),
            pl.BlockSpec((1, col_tile), lambda j: (0, j)),
            small, small, small, small,
        ],
        out_specs=[
            pl.BlockSpec((MOD_ROWS, col_tile), lambda j: (0, j)),
            pl.BlockSpec((1, LANES), lambda j: (0, 0)),
        ],
        out_shape=[
            jax.ShapeDtypeStruct((MOD_ROWS, n_col), jnp.float32),
            jax.ShapeDtypeStruct((1, LANES), jnp.float32),
        ],
        name="mod",
    )(cond, w_mod, b_mod, lq1, lk1, lq2, lk2)


def _in_proj_kernel(rope, x_ref, shift_ref, scale_ref, g_ref, w_ref, *rest):
    if rope:
        cos_ref, sina_ref, sinb_ref, q_ref, k_ref, v_ref, gb_ref, cu_ref = rest
    else:
        q_ref, k_ref, v_ref, gb_ref, cu_ref = rest
    x = x_ref[...]
    h = _rms(x) * g_ref[...] * (1.0 + scale_ref[...]) + shift_ref[...]
    h = h.astype(jnp.bfloat16)

    def proj(lo, hi):
        return jnp.dot(h, w_ref[:, lo:hi], preferred_element_type=jnp.float32)

    def rot(t):
        return (t * cos_ref[...] + pltpu.roll(t, LANES - 16, axis=1) * sina_ref[...]
                + pltpu.roll(t, 16, axis=1) * sinb_ref[...])

    for hd in range(N_HEADS):
        q = proj(hd * D_V, (hd + 1) * D_V)
        k = proj(ATTN_W + hd * D_V, ATTN_W + (hd + 1) * D_V)
        v = proj(2 * ATTN_W + hd * D_V, 2 * ATTN_W + (hd + 1) * D_V)
        if rope:
            q = rot(q)
            k = rot(k)
        q_ref[hd] = (q * QK_SCALE).astype(q_ref.dtype)
        k_ref[hd] = k.astype(k_ref.dtype)
        v_ref[hd] = v.astype(v_ref.dtype)
    c0 = 3 * ATTN_W
    gb_ref[...] = proj(c0, c0 + CONV_W).astype(gb_ref.dtype)
    gc = proj(c0 + CONV_W, c0 + 2 * CONV_W)
    u = proj(c0 + 2 * CONV_W, c0 + 3 * CONV_W)
    cu_ref[...] = (gc * u).astype(cu_ref.dtype)


def _rope_tables(seq):
    n_rows = seq // GRID_W
    row = np.repeat(np.arange(n_rows), GRID_W).astype(np.float64)
    col = np.tile(np.arange(GRID_W), n_rows).astype(np.float64)
    nf = D_QK // 4
    inv = ROPE_BASE ** (-np.arange(nf, dtype=np.float64) / nf)
    ar = row[:, None] * inv
    ac = col[:, None] * inv
    ang = np.concatenate([ar, ar, ac, ac], axis=-1)
    ang = np.concatenate([ang, ang], axis=-1)
    first_half = (np.arange(LANES) % 32) < 16
    cos = np.cos(ang)
    sin = np.sin(ang)
    sina = np.where(first_half, -sin, 0.0)
    sinb = np.where(first_half, 0.0, sin)
    return tuple(jnp.asarray(t, dtype=jnp.float32) for t in (cos, sina, sinb))


def _in_proj_call(x, mod, mod_per_batch, g_pre1, w_in, rope, kv_dtype):
    bsz, seq, _ = x.shape
    tm = TOKEN_TILE

    def mod_spec(chunk):
        if mod_per_batch:
            return pl.BlockSpec((None, 1, D_MODEL), lambda b, i: (b, 0, chunk))
        return pl.BlockSpec((None, 1, D_MODEL), lambda b, i: (0, 0, chunk))

    in_specs = [
        pl.BlockSpec((None, tm, D_MODEL), lambda b, i: (b, i, 0)),
        mod_spec(0), mod_spec(1),
        pl.BlockSpec((1, D_MODEL), lambda b, i: (0, 0)),
        pl.BlockSpec((D_MODEL, IN_W), lambda b, i: (0, 0)),
    ]
    args = [x, mod, mod, g_pre1, w_in]
    if rope:
        in_specs += [pl.BlockSpec((tm, LANES), lambda b, i: (i, 0))] * 3
        args += list(_rope_tables(seq))
    head_spec = pl.BlockSpec((None, N_HEADS, tm, D_V), lambda b, i: (b, 0, i, 0))
    tok_spec = pl.BlockSpec((None, tm, CONV_W), lambda b, i: (b, i, 0))
    return pl.pallas_call(
        functools.partial(_in_proj_kernel, rope),
        grid=(bsz, seq // tm),
        in_specs=in_specs,
        out_specs=[head_spec, head_spec, head_spec, tok_spec, tok_spec],
        out_shape=[
            jax.ShapeDtypeStruct((bsz, N_HEADS, seq, D_V), jnp.bfloat16),
            jax.ShapeDtypeStruct((bsz, N_HEADS, seq, D_V), kv_dtype),
            jax.ShapeDtypeStruct((bsz, N_HEADS, seq, D_V), kv_dtype),
            jax.ShapeDtypeStruct((bsz, seq, CONV_W), jnp.bfloat16),
            jax.ShapeDtypeStruct((bsz, seq, CONV_W), jnp.bfloat16),
        ],
        name="in_proj_rope" if rope else "in_proj",
    )(*args)


def _attn_kernel(n_kv, lam_ref, g_ref, q_ref, *rest):
    kv_refs = rest[:2 * n_kv]
    o_ref = rest[2 * n_kv]
    q = q_ref[...]
    lane = lax.broadcasted_iota(jnp.int32, q.shape, 1)
    zero = jnp.zeros_like(q)
    q1 = jnp.where(lane < D_QK, q, zero)
    q2 = jnp.where(lane >= D_QK, q, zero)
    ks = [kv_refs[2 * j][...].astype(jnp.bfloat16) for j in range(n_kv)]
    vs = [kv_refs[2 * j + 1][...].astype(jnp.bfloat16) for j in range(n_kv)]

    def probs(qh):
        s = [lax.dot_general(qh, k, NT_DIMS, preferred_element_type=jnp.float32) for k in ks]
        m = functools.reduce(jnp.maximum, [jnp.max(t, axis=-1, keepdims=True) for t in s])
        p = [jnp.exp2(t - m) for t in s]
        l = functools.reduce(jnp.add, [jnp.sum(t, axis=-1, keepdims=True) for t in p])
        return p, 1.0 / l

    p1, r1 = probs(q1)
    p2, r2 = probs(q2)
    r2 = r2 * lam_ref[0:1, 0:1]
    o = None
    for a, b, v in zip(p1, p2, vs):
        p = (a * r1 - b * r2).astype(jnp.bfloat16)
        t = jnp.dot(p, v, preferred_element_type=jnp.float32)
        o = t if o is None else o + t
    o_ref[...] = (_rms(o) * (g_ref[...] * (1.0 - LAM_INIT))).astype(o_ref.dtype)


def _attn_call(lam, subln_g, q, kvs):
    bsz, _, seq, _ = q.shape
    tq = Q_TILE
    in_specs = [
        pl.BlockSpec((1, LANES), lambda b, h, i: (0, 0)),
        pl.BlockSpec((1, D_V), lambda b, h, i: (0, 0)),
        pl.BlockSpec((None, None, tq, D_V), lambda b, h, i: (b, h, i, 0)),
    ]
    args = [lam, subln_g, q]
    for k, v, spec in kvs:
        in_specs += [spec, spec]
        args += [k, v]
    return pl.pallas_call(
        functools.partial(_attn_kernel, len(kvs)),
        grid=(bsz, N_HEADS, seq // tq),
        in_specs=in_specs,
        out_specs=pl.BlockSpec((None, tq, D_V), lambda b, h, i: (b, i, h)),
        out_shape=jax.ShapeDtypeStruct((bsz, seq, ATTN_W), jnp.bfloat16),
        name="attn%d" % len(kvs),
    )(*args)


def _route(logits_t):
    lg = [logits_t[g:g + 1, :] for g in range(N_GROUPS)]
    mg = functools.reduce(jnp.maximum, lg)
    p_sel = 1.0 / functools.reduce(jnp.add, [jnp.exp(t - mg) for t in lg])
    g_sel = jnp.full(mg.shape, N_GROUPS - 1, jnp.int32)
    for g in range(N_GROUPS - 2, -1, -1):
        g_sel = jnp.where(lg[g] == mg, g, g_sel)
    le = []
    for j in range(EXP_PER_GROUP):
        t = jnp.zeros_like(mg)
        for g in range(N_GROUPS):
            r = 8 + g * EXP_PER_GROUP + j
            t = jnp.where(g_sel == g, logits_t[r:r + 1, :], t)
        le.append(t)
    m1 = functools.reduce(jnp.maximum, le)
    i1 = jnp.full(mg.shape, EXP_PER_GROUP - 1, jnp.int32)
    for j in range(EXP_PER_GROUP - 2, -1, -1):
        i1 = jnp.where(le[j] == m1, j, i1)
    neg = jnp.float32(-jnp.inf)
    rest = [jnp.where(i1 == j, neg, le[j]) for j in range(EXP_PER_GROUP)]
    m2 = functools.reduce(jnp.maximum, rest)
    i2 = jnp.full(mg.shape, EXP_PER_GROUP - 1, jnp.int32)
    for j in range(EXP_PER_GROUP - 2, -1, -1):
        i2 = jnp.where(rest[j] == m2, j, i2)
    e2 = jnp.exp(m2 - m1)
    w1 = p_sel / (1.0 + e2)
    w2 = p_sel * e2 / (1.0 + e2)
    rows = []
    for e in range(N_EXPERTS):
        g, j = divmod(e, EXP_PER_GROUP)
        in_g = g_sel == g
        rows.append(jnp.where(in_g & (i1 == j), w1, 0.0) + jnp.where(in_g & (i2 == j), w2, 0.0))
    return jnp.concatenate(rows, axis=0)


def _out_proj_kernel(n_tiles, o_ref, gb_ref, cu_ref, cup_ref, cun_ref, cw_ref, wo_ref, x_ref,
                     gate1_ref, shift2_ref, scale2_ref, gpost_ref, gpre_ref, wr_ref,
                     x1_ref, h2_ref, gate_ref):
    i = pl.program_id(1)
    tm = cu_ref.shape[0]
    cu = cu_ref[...].astype(jnp.float32)
    row = lax.broadcasted_iota(jnp.int32, cu.shape, 0)
    prev_row = jnp.where(i > 0, cup_ref[BF16_SUBLANES - 1:BF16_SUBLANES, :].astype(jnp.float32), 0.0)
    next_row = jnp.where(i < n_tiles - 1, cun_ref[0:1, :].astype(jnp.float32), 0.0)
    prev = jnp.where(row == 0, prev_row, pltpu.roll(cu, 1, axis=0))
    nxt = jnp.where(row == tm - 1, next_row, pltpu.roll(cu, tm - 1, axis=0))
    conv = cw_ref[0:1, :] * prev + cw_ref[1:2, :] * cu + cw_ref[2:3, :] * nxt
    yc = (gb_ref[...].astype(jnp.float32) * conv).astype(jnp.bfloat16)
    out = (jnp.dot(o_ref[...], wo_ref[0:ATTN_W, :], preferred_element_type=jnp.float32)
           + jnp.dot(yc, wo_ref[ATTN_W:D_MODEL, :], preferred_element_type=jnp.float32))
    x1 = x_ref[...] + gate1_ref[...] * (_rms(out) * gpost_ref[...])
    x1_ref[...] = x1
    h2 = (_rms(x1) * gpre_ref[...] * (1.0 + scale2_ref[...]) + shift2_ref[...]).astype(jnp.bfloat16)
    h2_ref[...] = h2
    logits = jnp.dot(h2, wr_ref[...], preferred_element_type=jnp.float32)
    gate_t = _route(logits.T)
    pad = jnp.zeros((LANES - N_EXPERTS, tm), jnp.float32)
    gate_ref[...] = jnp.concatenate([gate_t, pad], axis=0).T


def _out_proj_call(o, gb, cu, conv_w, w_o, x, mod, mod_per_batch, g_post1, g_pre2, w_router):
    bsz, seq, _ = x.shape
    tm = TOKEN_TILE
    n_tiles = seq // tm
    halo = tm // BF16_SUBLANES
    n_halo = seq // BF16_SUBLANES

    def mod_spec(chunk):
        if mod_per_batch:
            return pl.BlockSpec((None, 1, D_MODEL), lambda b, i: (b, 0, chunk))
        return pl.BlockSpec((None, 1, D_MODEL), lambda b, i: (0, 0, chunk))

    def vec_spec():
        return pl.BlockSpec((1, D_MODEL), lambda b, i: (0, 0))

    tok = lambda w: pl.BlockSpec((None, tm, w), lambda b, i: (b, i, 0))
    return pl.pallas_call(
        functools.partial(_out_proj_kernel, n_tiles),
        grid=(bsz, n_tiles),
        in_specs=[
            tok(ATTN_W), tok(CONV_W), tok(CONV_W),
            pl.BlockSpec((None, BF16_SUBLANES, CONV_W), lambda b, i: (b, jnp.maximum(i * halo - 1, 0), 0)),
            pl.BlockSpec((None, BF16_SUBLANES, CONV_W), lambda b, i: (b, jnp.minimum((i + 1) * halo, n_halo - 1), 0)),
            pl.BlockSpec((3, CONV_W), lambda b, i: (0, 0)),
            pl.BlockSpec((D_MODEL, D_MODEL), lambda b, i: (0, 0)),
            tok(D_MODEL),
            mod_spec(2), mod_spec(3), mod_spec(4),
            vec_spec(), vec_spec(),
            pl.BlockSpec((D_MODEL, LANES), lambda b, i: (0, 0)),
        ],
        out_specs=[tok(D_MODEL), tok(D_MODEL), tok(LANES)],
        out_shape=[
            jax.ShapeDtypeStruct((bsz, seq, D_MODEL), jnp.float32),
            jax.ShapeDtypeStruct((bsz, seq, D_MODEL), jnp.bfloat16),
            jax.ShapeDtypeStruct((bsz, seq, LANES), jnp.float32),
        ],
        name="out_proj",
    )(o, gb, cu, cu, cu, conv_w, w_o, x, mod, mod, mod, g_post1, g_pre2, w_router)


def _moe_kernel(h_ref, gate_ref, wg_ref, wu_ref, wd_ref, x1_ref, gate2_ref, gpost_ref, y_ref, acc_ref):
    e = pl.program_id(1)
    h = h_ref[...]
    a = jnp.dot(h, wg_ref[...].astype(jnp.bfloat16), preferred_element_type=jnp.float32)
    u = jnp.dot(h, wu_ref[...].astype(jnp.bfloat16), preferred_element_type=jnp.float32)
    act = (_silu(a) * u).astype(jnp.bfloat16)
    yk = jnp.dot(act, wd_ref[...].astype(jnp.bfloat16), preferred_element_type=jnp.float32)
    gate = gate_ref[...]
    lane = lax.broadcasted_iota(jnp.int32, gate.shape, 1)
    ge = jnp.sum(jnp.where(lane == e, gate, 0.0), axis=-1, keepdims=True)

    @pl.when(e == 0)
    def _():
        acc_ref[...] = ge * yk

    @pl.when(e > 0)
    def _():
        acc_ref[...] += ge * yk

    @pl.when(e == N_EXPERTS - 1)
    def _():
        y_ref[...] = x1_ref[...] + gate2_ref[...] * (_rms(acc_ref[...]) * gpost_ref[...])


def _moe_call(h2, gate, w_eg, w_eu, w_ed, x1, mod, mod_per_batch, g_post2):
    bsz, seq, _ = x1.shape
    tm = MOE_TILE
    tokens = bsz * seq
    h2 = h2.reshape(tokens, D_MODEL)
    gate = gate.reshape(tokens, LANES)
    x1 = x1.reshape(tokens, D_MODEL)
    if mod_per_batch:
        assert seq % tm == 0
        per = seq // tm
        gate2_spec = pl.BlockSpec((None, 1, D_MODEL), lambda i, e: (i // per, 0, 5))
    else:
        gate2_spec = pl.BlockSpec((None, 1, D_MODEL), lambda i, e: (0, 0, 5))
    tok = lambda w: pl.BlockSpec((tm, w), lambda i, e: (i, 0))
    y = pl.pallas_call(
        _moe_kernel,
        grid=(tokens // tm, N_EXPERTS),
        in_specs=[
            tok(D_MODEL), tok(LANES),
            pl.BlockSpec((None, D_MODEL, D_EXPERT), lambda i, e: (e, 0, 0)),
            pl.BlockSpec((None, D_MODEL, D_EXPERT), lambda i, e: (e, 0, 0)),
            pl.BlockSpec((None, D_EXPERT, D_MODEL), lambda i, e: (e, 0, 0)),
            tok(D_MODEL), gate2_spec,
            pl.BlockSpec((1, D_MODEL), lambda i, e: (0, 0)),
        ],
        out_specs=tok(D_MODEL),
        out_shape=jax.ShapeDtypeStruct((tokens, D_MODEL), jnp.float32),
        scratch_shapes=[pltpu.VMEM((tm, D_MODEL), jnp.float32)],
        name="moe",
    )(h2, gate, w_eg, w_eu, w_ed, x1, mod, g_post2)
    return y.reshape(bsz, seq, D_MODEL)


def _full_kv_spec(seq):
    return pl.BlockSpec((None, None, seq, D_V), lambda b, h, i: (b, h, 0, 0))


def kernel(x_prompt, x_sample, cache_k, cache_v, c, c_ctx, w_mod, b_mod, g_pre1, g_post1, g_pre2, g_post2,
           w_in, conv_w, lambda_q1, lambda_k1, lambda_q2, lambda_k2, subln_g, w_o, w_router_group,
           w_router_expert, w_exp_gate, w_exp_up, w_exp_down):
    n_lat = c.shape[0]
    cond = jnp.concatenate(
        [c, c_ctx[None, :], jnp.zeros((MOD_ROWS - n_lat - 1, D_MODEL), jnp.float32)], axis=0)
    mod, lam = _mod_call(cond, w_mod[0], b_mod, lambda_q1, lambda_k1, lambda_q2, lambda_k2)
    mod = mod.reshape(MOD_ROWS, 1, 6 * D_MODEL)
    mod_lat, mod_ctx = mod[:n_lat], mod[n_lat:n_lat + 1]

    w_in_b = w_in[0].astype(jnp.bfloat16)
    w_o_b = w_o[0].astype(jnp.bfloat16)
    w_router = jnp.concatenate(
        [w_router_group[0], jnp.zeros((D_MODEL, 8 - N_GROUPS), jnp.float32), w_router_expert[0],
         jnp.zeros((D_MODEL, LANES - 8 - N_EXPERTS), jnp.float32)], axis=1).astype(jnp.bfloat16)
    w_eg, w_eu, w_ed = w_exp_gate[0], w_exp_up[0], w_exp_down[0]

    def layer(x, mod_x, per_batch, rope, ctx_kv):
        kv_dtype = jnp.bfloat16 if rope else jnp.float32
        q, k, v, gb, cu = _in_proj_call(x, mod_x, per_batch, g_pre1, w_in_b, rope, kv_dtype)
        kvs = []
        if ctx_kv is not None:
            ck, cv = ctx_kv
            spec = pl.BlockSpec((None, None, None, ck.shape[3], D_V), lambda b, h, i: (b, 0, h, 0, 0))
            kvs.append((ck, cv, spec))
        kvs.append((k, v, _full_kv_spec(x.shape[1])))
        o = _attn_call(lam, subln_g, q, kvs)
        x1, h2, gate = _out_proj_call(o, gb, cu, conv_w[0], w_o_b, x, mod_x, per_batch, g_post1, g_pre2, w_router)
        y = _moe_call(h2, gate, w_eg, w_eu, w_ed, x1, mod_x, per_batch, g_post2)
        return y, k, v

    yp, kp, vp = layer(x_prompt, mod_ctx, False, False, None)
    ys, _, _ = layer(x_sample, mod_lat, True, True, (cache_k, cache_v))
    return yp, ys, kp[:, None], vp[:, None]
```

```python
import functools
import math

import numpy as np
import jax
import jax.numpy as jnp
from jax import lax
from jax.experimental import pallas as pl
from jax.experimental.pallas import tpu as pltpu

D_MODEL = 1024
GRID_W = 64
N_HEADS = 4
D_QK = 64
D_V = 128
ATTN_W = N_HEADS * D_V
CONV_W = D_MODEL - ATTN_W
IN_W = 3 * ATTN_W + 3 * CONV_W
N_GROUPS = 4
EXP_PER_GROUP = 4
N_EXPERTS = N_GROUPS * EXP_PER_GROUP
D_EXPERT = 512
ROPE_BASE = 10000.0
EPS = 1e-6
LAM_INIT = 0.8 - 0.6 * math.exp(-0.3 * 0)

LANES = 128
BF16_SUBLANES = 16
MOD_ROWS = 16
TOKEN_TILE = 256
Q_TILE = 256
DISPATCH_TILE = 512
CHUNK = BF16_SUBLANES
TILE_SLOTS = 1280
EXPERT_BLOCK = 512
XS_W = D_MODEL + LANES
QK_SCALE = (1.0 / math.sqrt(D_QK)) * math.log2(math.e)

NT_DIMS = (((1,), (1,)), ((), ()))


def _rms(x):
    return x * lax.rsqrt(jnp.mean(x * x, axis=-1, keepdims=True) + EPS)


def _silu(x):
    return x * (1.0 / (1.0 + jnp.exp(-x)))


def _mod_kernel(cond_ref, w_ref, b_ref, lq1_ref, lk1_ref, lq2_ref, lk2_ref, mod_ref, lam_ref):
    s = _silu(cond_ref[...])
    m = lax.dot_general(s, w_ref[...], (((1,), (0,)), ((), ())),
                        precision=lax.Precision.HIGHEST, preferred_element_type=jnp.float32)
    mod_ref[...] = m + b_ref[...]
    a = jnp.sum(lq1_ref[...] * lk1_ref[...], axis=-1, keepdims=True)
    b = jnp.sum(lq2_ref[...] * lk2_ref[...], axis=-1, keepdims=True)
    lam_ref[...] = jnp.broadcast_to(jnp.exp(a) - jnp.exp(b) + LAM_INIT, lam_ref.shape)


def _mod_call(cond, w_mod, b_mod, lq1, lk1, lq2, lk2):
    n_col = 6 * D_MODEL
    col_tile = 1536
    small = pl.BlockSpec((1, D_QK), lambda j: (0, 0))
    return pl.pallas_call(
        _mod_kernel,
        grid=(n_col // col_tile,),
        in_specs=[
            pl.BlockSpec((MOD_ROWS, D_MODEL), lambda j: (0, 0)),
            pl.BlockSpec((D_MODEL, col_tile), lambda j: (0, j)),
            pl.BlockSpec((1, col_tile), lambda j: (0, j)),
            small, small, small, small,
        ],
        out_specs=[
            pl.BlockSpec((MOD_ROWS, col_tile), lambda j: (0, j)),
            pl.BlockSpec((1, LANES), lambda j: (0, 0)),
        ],
        out_shape=[
            jax.ShapeDtypeStruct((MOD_ROWS, n_col), jnp.float32),
            jax.ShapeDtypeStruct((1, LANES), jnp.float32),
        ],
        name="mod",
    )(cond, w_mod, b_mod, lq1, lk1, lq2, lk2)


def _in_proj_kernel(rope, x_ref, shift_ref, scale_ref, g_ref, w_ref, *rest):
    if rope:
        cos_ref, sina_ref, sinb_ref, q_ref, k_ref, v_ref, gb_ref, cu_ref = rest
    else:
        q_ref, k_ref, v_ref, gb_ref, cu_ref = rest
    x = x_ref[...]
    h = _rms(x) * g_ref[...] * (1.0 + scale_ref[...]) + shift_ref[...]
    h = h.astype(jnp.bfloat16)

    def proj(lo, hi):
        return jnp.dot(h, w_ref[:, lo:hi], preferred_element_type=jnp.float32)

    def rot(t):
        return (t * cos_ref[...] + pltpu.roll(t, LANES - 16, axis=1) * sina_ref[...]
                + pltpu.roll(t, 16, axis=1) * sinb_ref[...])

    for hd in range(N_HEADS):
        q = proj(hd * D_V, (hd + 1) * D_V)
        k = proj(ATTN_W + hd * D_V, ATTN_W + (hd + 1) * D_V)
        v = proj(2 * ATTN_W + hd * D_V, 2 * ATTN_W + (hd + 1) * D_V)
        if rope:
            q = rot(q)
            k = rot(k)
        q_ref[hd] = (q * QK_SCALE).astype(q_ref.dtype)
        k_ref[hd] = k.astype(k_ref.dtype)
        v_ref[hd] = v.astype(v_ref.dtype)
    c0 = 3 * ATTN_W
    gb_ref[...] = proj(c0, c0 + CONV_W).astype(gb_ref.dtype)
    gc = proj(c0 + CONV_W, c0 + 2 * CONV_W)
    u = proj(c0 + 2 * CONV_W, c0 + 3 * CONV_W)
    cu_ref[...] = (gc * u).astype(cu_ref.dtype)


def _rope_tables(seq):
    n_rows = seq // GRID_W
    row = np.repeat(np.arange(n_rows), GRID_W).astype(np.float64)
    col = np.tile(np.arange(GRID_W), n_rows).astype(np.float64)
    nf = D_QK // 4
    inv = ROPE_BASE ** (-np.arange(nf, dtype=np.float64) / nf)
    ar = row[:, None] * inv
    ac = col[:, None] * inv
    ang = np.concatenate([ar, ar, ac, ac], axis=-1)
    ang = np.concatenate([ang, ang], axis=-1)
    first_half = (np.arange(LANES) % 32) < 16
    cos = np.cos(ang)
    sin = np.sin(ang)
    sina = np.where(first_half, -sin, 0.0)
    sinb = np.where(first_half, 0.0, sin)
    return tuple(jnp.asarray(t, dtype=jnp.float32) for t in (cos, sina, sinb))


def _in_proj_call(x, mod, mod_per_batch, g_pre1, w_in, rope, kv_dtype):
    bsz, seq, _ = x.shape
    tm = TOKEN_TILE

    def mod_spec(chunk):
        if mod_per_batch:
            return pl.BlockSpec((None, 1, D_MODEL), lambda b, i: (b, 0, chunk))
        return pl.BlockSpec((None, 1, D_MODEL), lambda b, i: (0, 0, chunk))

    in_specs = [
        pl.BlockSpec((None, tm, D_MODEL), lambda b, i: (b, i, 0)),
        mod_spec(0), mod_spec(1),
        pl.BlockSpec((1, D_MODEL), lambda b, i: (0, 0)),
        pl.BlockSpec((D_MODEL, IN_W), lambda b, i: (0, 0)),
    ]
    args = [x, mod, mod, g_pre1, w_in]
    if rope:
        in_specs += [pl.BlockSpec((tm, LANES), lambda b, i: (i, 0))] * 3
        args += list(_rope_tables(seq))
    head_spec = pl.BlockSpec((None, N_HEADS, tm, D_V), lambda b, i: (b, 0, i, 0))
    tok_spec = pl.BlockSpec((None, tm, CONV_W), lambda b, i: (b, i, 0))
    return pl.pallas_call(
        functools.partial(_in_proj_kernel, rope),
        grid=(bsz, seq // tm),
        in_specs=in_specs,
        out_specs=[head_spec, head_spec, head_spec, tok_spec, tok_spec],
        out_shape=[
            jax.ShapeDtypeStruct((bsz, N_HEADS, seq, D_V), jnp.bfloat16),
            jax.ShapeDtypeStruct((bsz, N_HEADS, seq, D_V), kv_dtype),
            jax.ShapeDtypeStruct((bsz, N_HEADS, seq, D_V), kv_dtype),
            jax.ShapeDtypeStruct((bsz, seq, CONV_W), jnp.bfloat16),
            jax.ShapeDtypeStruct((bsz, seq, CONV_W), jnp.bfloat16),
        ],
        name="in_proj_rope" if rope else "in_proj",
    )(*args)


def _attn_kernel(n_kv, lam_ref, g_ref, q_ref, *rest):
    kv_refs = rest[:2 * n_kv]
    o_ref = rest[2 * n_kv]
    q = q_ref[...]
    lane = lax.broadcasted_iota(jnp.int32, q.shape, 1)
    zero = jnp.zeros_like(q)
    q1 = jnp.where(lane < D_QK, q, zero)
    q2 = jnp.where(lane >= D_QK, q, zero)
    ks = [kv_refs[2 * j][...].astype(jnp.bfloat16) for j in range(n_kv)]
    vs = [kv_refs[2 * j + 1][...].astype(jnp.bfloat16) for j in range(n_kv)]

    def probs(qh):
        s = [lax.dot_general(qh, k, NT_DIMS, preferred_element_type=jnp.float32) for k in ks]
        m = functools.reduce(jnp.maximum, [jnp.max(t, axis=-1, keepdims=True) for t in s])
        p = [jnp.exp2(t - m) for t in s]
        l = functools.reduce(jnp.add, [jnp.sum(t, axis=-1, keepdims=True) for t in p])
        return p, 1.0 / l

    p1, r1 = probs(q1)
    p2, r2 = probs(q2)
    r2 = r2 * lam_ref[0:1, 0:1]
    o = None
    for a, b, v in zip(p1, p2, vs):
        p = (a * r1 - b * r2).astype(jnp.bfloat16)
        t = jnp.dot(p, v, preferred_element_type=jnp.float32)
        o = t if o is None else o + t
    o_ref[...] = (_rms(o) * (g_ref[...] * (1.0 - LAM_INIT))).astype(o_ref.dtype)


def _attn_call(lam, subln_g, q, kvs):
    bsz, _, seq, _ = q.shape
    tq = Q_TILE
    in_specs = [
        pl.BlockSpec((1, LANES), lambda b, h, i: (0, 0)),
        pl.BlockSpec((1, D_V), lambda b, h, i: (0, 0)),
        pl.BlockSpec((None, None, tq, D_V), lambda b, h, i: (b, h, i, 0)),
    ]
    args = [lam, subln_g, q]
    for k, v, spec in kvs:
        in_specs += [spec, spec]
        args += [k, v]
    return pl.pallas_call(
        functools.partial(_attn_kernel, len(kvs)),
        grid=(bsz, N_HEADS, seq // tq),
        in_specs=in_specs,
        out_specs=pl.BlockSpec((None, tq, D_V), lambda b, h, i: (b, i, h)),
        out_shape=jax.ShapeDtypeStruct((bsz, seq, ATTN_W), jnp.bfloat16),
        name="attn%d" % len(kvs),
    )(*args)


def _route(logits_t):
    lg = [logits_t[g:g + 1, :] for g in range(N_GROUPS)]
    mg = functools.reduce(jnp.maximum, lg)
    p_sel = 1.0 / functools.reduce(jnp.add, [jnp.exp(t - mg) for t in lg])
    g_sel = jnp.full(mg.shape, N_GROUPS - 1, jnp.int32)
    for g in range(N_GROUPS - 2, -1, -1):
        g_sel = jnp.where(lg[g] == mg, g, g_sel)
    le = []
    for j in range(EXP_PER_GROUP):
        t = jnp.zeros_like(mg)
        for g in range(N_GROUPS):
            r = 8 + g * EXP_PER_GROUP + j
            t = jnp.where(g_sel == g, logits_t[r:r + 1, :], t)
        le.append(t)
    m1 = functools.reduce(jnp.maximum, le)
    i1 = jnp.full(mg.shape, EXP_PER_GROUP - 1, jnp.int32)
    for j in range(EXP_PER_GROUP - 2, -1, -1):
        i1 = jnp.where(le[j] == m1, j, i1)
    neg = jnp.float32(-jnp.inf)
    rest = [jnp.where(i1 == j, neg, le[j]) for j in range(EXP_PER_GROUP)]
    m2 = functools.reduce(jnp.maximum, rest)
    i2 = jnp.full(mg.shape, EXP_PER_GROUP - 1, jnp.int32)
    for j in range(EXP_PER_GROUP - 2, -1, -1):
        i2 = jnp.where(rest[j] == m2, j, i2)
    e2 = jnp.exp(m2 - m1)
    w1 = p_sel / (1.0 + e2)
    w2 = p_sel * e2 / (1.0 + e2)
    base = g_sel * EXP_PER_GROUP
    return (base + i1).astype(jnp.float32), (base + i2).astype(jnp.float32), w1, w2


def _out_proj_kernel(n_tiles, o_ref, gb_ref, cu_ref, cup_ref, cun_ref, cw_ref, wo_ref, x_ref,
                     gate1_ref, shift2_ref, scale2_ref, gpost_ref, gpre_ref, wr_ref, *rest):
    x1_ref, h2_ref, route_ref = rest[-3:]
    i = pl.program_id(1)
    tm = cu_ref.shape[0]
    cu = cu_ref[...].astype(jnp.float32)
    row = lax.broadcasted_iota(jnp.int32, cu.shape, 0)
    prev_row = jnp.where(i > 0, cup_ref[BF16_SUBLANES - 1:BF16_SUBLANES, :].astype(jnp.float32), 0.0)
    next_row = jnp.where(i < n_tiles - 1, cun_ref[0:1, :].astype(jnp.float32), 0.0)
    prev = jnp.where(row == 0, prev_row, pltpu.roll(cu, 1, axis=0))
    nxt = jnp.where(row == tm - 1, next_row, pltpu.roll(cu, tm - 1, axis=0))
    conv = cw_ref[0:1, :] * prev + cw_ref[1:2, :] * cu + cw_ref[2:3, :] * nxt
    yc = (gb_ref[...].astype(jnp.float32) * conv).astype(jnp.bfloat16)
    out = (jnp.dot(o_ref[...], wo_ref[0:ATTN_W, :], preferred_element_type=jnp.float32)
           + jnp.dot(yc, wo_ref[ATTN_W:D_MODEL, :], preferred_element_type=jnp.float32))
    x1 = x_ref[...] + gate1_ref[...] * (_rms(out) * gpost_ref[...])
    x1_ref[...] = x1
    h2 = (_rms(x1) * gpre_ref[...] * (1.0 + scale2_ref[...]) + shift2_ref[...]).astype(jnp.bfloat16)
    h2_ref[...] = h2
    logits = jnp.dot(h2, wr_ref[...], preferred_element_type=jnp.float32)
    e1, e2, w1, w2 = _route(logits.T)
    route_ref[...] = jnp.concatenate([e1, e2, w1, w2, jnp.zeros((4, tm), jnp.float32)], axis=0)


def _out_proj_call(o, gb, cu, conv_w, w_o, x, mod, mod_per_batch, g_post1, g_pre2, w_router,
                   total_tokens, token_offset, carried):
    bsz, seq, _ = x.shape
    tm = TOKEN_TILE
    n_tiles = seq // tm
    halo = tm // BF16_SUBLANES
    n_halo = seq // BF16_SUBLANES
    tile0 = token_offset // tm
    per_route = DISPATCH_TILE // tm

    def mod_spec(chunk):
        if mod_per_batch:
            return pl.BlockSpec((None, 1, D_MODEL), lambda b, i: (b, 0, chunk))
        return pl.BlockSpec((None, 1, D_MODEL), lambda b, i: (0, 0, chunk))

    def vec_spec():
        return pl.BlockSpec((1, D_MODEL), lambda b, i: (0, 0))

    tok = lambda w: pl.BlockSpec((None, tm, w), lambda b, i: (b, i, 0))
    flat = lambda w: pl.BlockSpec((tm, w), lambda b, i: (tile0 + b * n_tiles + i, 0))
    route_spec = pl.BlockSpec(
        (None, 8, tm),
        lambda b, i: ((tile0 + b * n_tiles + i) // per_route, 0, (tile0 + b * n_tiles + i) % per_route))
    in_specs = [
        tok(ATTN_W), tok(CONV_W), tok(CONV_W),
        pl.BlockSpec((None, BF16_SUBLANES, CONV_W), lambda b, i: (b, jnp.maximum(i * halo - 1, 0), 0)),
        pl.BlockSpec((None, BF16_SUBLANES, CONV_W), lambda b, i: (b, jnp.minimum((i + 1) * halo, n_halo - 1), 0)),
        pl.BlockSpec((3, CONV_W), lambda b, i: (0, 0)),
        pl.BlockSpec((D_MODEL, D_MODEL), lambda b, i: (0, 0)),
        tok(D_MODEL),
        mod_spec(2), mod_spec(3), mod_spec(4),
        vec_spec(), vec_spec(),
        pl.BlockSpec((D_MODEL, LANES), lambda b, i: (0, 0)),
    ]
    args = [o, gb, cu, cu, cu, conv_w, w_o, x, mod, mod, mod, g_post1, g_pre2, w_router]
    aliases = {}
    if carried is not None:
        for j, arr in enumerate(carried):
            aliases[len(args)] = j
            in_specs.append(pl.BlockSpec(memory_space=pl.ANY))
            args.append(arr)
    return pl.pallas_call(
        functools.partial(_out_proj_kernel, n_tiles),
        grid=(bsz, n_tiles),
        in_specs=in_specs,
        out_specs=[flat(D_MODEL), flat(D_MODEL), route_spec],
        out_shape=[
            jax.ShapeDtypeStruct((total_tokens, D_MODEL), jnp.float32),
            jax.ShapeDtypeStruct((total_tokens, D_MODEL), jnp.bfloat16),
            jax.ShapeDtypeStruct((total_tokens // DISPATCH_TILE, 8, DISPATCH_TILE), jnp.float32),
        ],
        input_output_aliases=aliases,
        name="out_proj",
    )(*args)


def _plan_kernel(route_ref, slots_ref, slots_t_ref, counts_ref):
    r = route_ref[...]
    t = r.shape[1]
    e1 = r[0:1, :].astype(jnp.int32)
    e2 = r[1:2, :].astype(jnp.int32)
    eid = lax.broadcasted_iota(jnp.int32, (N_EXPERTS, t), 0)
    hot1 = jnp.where(eid == e1, 1.0, 0.0)
    hot2 = jnp.where(eid == e2, 1.0, 0.0)
    hot = jnp.concatenate([hot1, hot2], axis=0).astype(jnp.bfloat16)
    before = (lax.broadcasted_iota(jnp.int32, (t, t), 0) < lax.broadcasted_iota(jnp.int32, (t, t), 1))
    before = jnp.where(before, 1.0, 0.0).astype(jnp.bfloat16)
    rank = jnp.dot(hot, before, preferred_element_type=jnp.float32)
    n1 = jnp.sum(hot1, axis=1, keepdims=True)
    n2 = jnp.sum(hot2, axis=1, keepdims=True)
    ones = jnp.ones((8, t), jnp.bfloat16)
    cnt_row = lax.dot_general(ones, (hot1 + hot2).astype(jnp.bfloat16), NT_DIMS,
                              preferred_element_type=jnp.float32)[0:1, :]
    pad_row = jnp.floor((cnt_row + (CHUNK - 1)) * (1.0 / CHUNK)) * CHUNK
    lower = (lax.broadcasted_iota(jnp.int32, (N_EXPERTS, N_EXPERTS), 1)
             < lax.broadcasted_iota(jnp.int32, (N_EXPERTS, N_EXPERTS), 0))
    base = jnp.sum(jnp.where(lower, pad_row, 0.0), axis=1, keepdims=True)
    slot1 = jnp.sum(hot1 * (base + rank[:N_EXPERTS]), axis=0, keepdims=True)
    slot2 = jnp.sum(hot2 * (base + n1 + rank[N_EXPERTS:]), axis=0, keepdims=True)
    slots_ref[...] = jnp.concatenate([slot1, slot2, r[2:4, :], jnp.zeros((4, t), jnp.float32)], axis=0)
    wide = jnp.concatenate([slot1, slot2, jnp.zeros((LANES - 2, t), jnp.float32)], axis=0)
    slots_t_ref[...] = wide.T
    counts_ref[...] = jnp.broadcast_to(n1 + n2, counts_ref.shape)


def _plan_call(route):
    n_tiles, _, t = route.shape
    return pl.pallas_call(
        _plan_kernel,
        grid=(n_tiles,),
        in_specs=[pl.BlockSpec((None, 8, t), lambda i: (i, 0, 0))],
        out_specs=[
            pl.BlockSpec((None, 8, t), lambda i: (i, 0, 0)),
            pl.BlockSpec((None, t, LANES), lambda i: (i, 0, 0)),
            pl.BlockSpec((None, N_EXPERTS, LANES), lambda i: (i, 0, 0)),
        ],
        out_shape=[
            jax.ShapeDtypeStruct((n_tiles, 8, t), jnp.float32),
            jax.ShapeDtypeStruct((n_tiles, t, LANES), jnp.float32),
            jax.ShapeDtypeStruct((n_tiles, N_EXPERTS, LANES), jnp.float32),
        ],
        name="moe_plan",
    )(route)


def _layout_tables(counts, n_blocks):
    cnt = counts[:, :, 0].astype(jnp.int32)
    nch = (cnt + (CHUNK - 1)) // CHUNK
    csum = jnp.cumsum(nch, axis=0)
    rows = CHUNK * csum[-1]
    blocks = (rows + (EXPERT_BLOCK - 1)) // EXPERT_BLOCK
    blk_end = jnp.cumsum(blocks)
    blk_start = blk_end - blocks
    off = blk_start[None, :] * EXPERT_BLOCK + CHUNK * (csum - nch)
    base = CHUNK * (jnp.cumsum(nch, axis=1) - nch)
    n_used = blk_end[-1]
    bc = jnp.minimum(jnp.arange(n_blocks, dtype=jnp.int32), n_used - 1)
    block_expert = jnp.sum((bc[:, None] >= blk_end[None, :]).astype(jnp.int32), axis=1)
    tail_start = blk_start * EXPERT_BLOCK + rows
    tail_chunks = (blocks * EXPERT_BLOCK - rows) // CHUNK
    flat = lambda a: a.reshape(-1).astype(jnp.int32)
    return (flat(off), flat(nch), flat(base), flat(tail_start), flat(tail_chunks),
            flat(block_expert), flat(n_used))


def _run_copies(tile, off_ref, nch_ref, base_ref, make_copy, action):
    for e in range(N_EXPERTS):
        idx = tile * N_EXPERTS + e
        off = off_ref[idx]
        base = base_ref[idx]

        def body(m, carry, off=off, base=base):
            action(make_copy(pl.multiple_of(base + m * CHUNK, CHUNK), pl.multiple_of(off + m * CHUNK, CHUNK)))
            return carry

        lax.fori_loop(0, nch_ref[idx], body, 0)


def _dispatch_kernel(off_ref, nch_ref, base_ref, tail_start_ref, tail_chunks_ref,
                     h_ref, slots_ref, xs_ref, buf, zbuf, sem, zsem):
    i = pl.program_id(0)
    n = pl.num_programs(0)
    slot = i % 2
    s = slots_ref[...]
    t = s.shape[1]
    s1 = s[0:1, :].astype(jnp.int32)
    s2 = s[1:2, :].astype(jnp.int32)
    rid = lax.broadcasted_iota(jnp.int32, (TILE_SLOTS, t), 0)
    m1 = rid == s1
    m2 = rid == s2
    perm = jnp.where(m1, 1.0, jnp.where(m2, 1.0, 0.0)).astype(jnp.bfloat16)
    rows = jnp.dot(perm, h_ref[...], preferred_element_type=jnp.float32)
    w = jnp.sum(jnp.where(m1, s[2:3, :], 0.0) + jnp.where(m2, s[3:4, :], 0.0), axis=1, keepdims=True)
    hi = w.astype(jnp.bfloat16).astype(jnp.float32)
    lane = lax.broadcasted_iota(jnp.int32, (TILE_SLOTS, LANES), 1)
    aux = jnp.where(lane == 0, hi, jnp.where(lane == 1, w - hi, 0.0))
    buf[slot, :, 0:D_MODEL] = rows.astype(jnp.bfloat16)
    buf[slot, :, D_MODEL:XS_W] = aux.astype(jnp.bfloat16)

    def copies(sl):
        def make(src_row, dst_row):
            return pltpu.make_async_copy(buf.at[sl, pl.ds(src_row, CHUNK), :],
                                         xs_ref.at[pl.ds(dst_row, CHUNK), :], sem.at[sl])
        return make

    _run_copies(i, off_ref, nch_ref, base_ref, copies(slot), lambda c: c.start())

    @pl.when(i > 0)
    def _():
        _run_copies(i - 1, off_ref, nch_ref, base_ref, copies(1 - slot), lambda c: c.wait())

    @pl.when(i == n - 1)
    def _():
        zbuf[...] = jnp.zeros_like(zbuf)

        def tails(action):
            for e in range(N_EXPERTS):
                start = tail_start_ref[e]

                def body(m, carry, start=start):
                    action(pltpu.make_async_copy(
                        zbuf, xs_ref.at[pl.ds(pl.multiple_of(start + m * CHUNK, CHUNK), CHUNK), :], zsem))
                    return carry

                lax.fori_loop(0, tail_chunks_ref[e], body, 0)

        tails(lambda c: c.start())
        tails(lambda c: c.wait())
        _run_copies(i, off_ref, nch_ref, base_ref, copies(slot), lambda c: c.wait())


def _dispatch_call(tables, h2, slots, n_blocks):
    off, nch, base, tail_start, tail_chunks = tables
    tokens = h2.shape[0]
    t = DISPATCH_TILE
    return pl.pallas_call(
        _dispatch_kernel,
        grid_spec=pltpu.PrefetchScalarGridSpec(
            num_scalar_prefetch=5,
            grid=(tokens // t,),
            in_specs=[
                pl.BlockSpec((t, D_MODEL), lambda i, *_: (i, 0)),
                pl.BlockSpec((None, 8, t), lambda i, *_: (i, 0, 0)),
            ],
            out_specs=pl.BlockSpec(memory_space=pl.ANY),
            scratch_shapes=[
                pltpu.VMEM((2, TILE_SLOTS, XS_W), jnp.bfloat16),
                pltpu.VMEM((CHUNK, XS_W), jnp.bfloat16),
                pltpu.SemaphoreType.DMA((2,)),
                pltpu.SemaphoreType.DMA(()),
            ],
        ),
        out_shape=jax.ShapeDtypeStruct((n_blocks * EXPERT_BLOCK, XS_W), jnp.bfloat16),
        name="moe_dispatch",
    )(off, nch, base, tail_start, tail_chunks, h2, slots)


def _expert_kernel(bexp_ref, nused_ref, xs_ref, wg_ref, wu_ref, wd_ref, ys_ref, wgu_b, wd_b):
    b = pl.program_id(0)
    e = bexp_ref[b]
    prev = bexp_ref[jnp.maximum(b - 1, 0)]

    @pl.when((b == 0) | (e != prev))
    def _():
        wgu_b[:, 0:D_EXPERT] = wg_ref[...].astype(jnp.bfloat16)
        wgu_b[:, D_EXPERT:2 * D_EXPERT] = wu_ref[...].astype(jnp.bfloat16)
        wd_b[...] = wd_ref[...].astype(jnp.bfloat16)

    @pl.when(b < nused_ref[0])
    def _():
        x = xs_ref[:, 0:D_MODEL]
        gu = jnp.dot(x, wgu_b[...], preferred_element_type=jnp.float32)
        act = (_silu(gu[:, 0:D_EXPERT]) * gu[:, D_EXPERT:2 * D_EXPERT]).astype(jnp.bfloat16)
        y = jnp.dot(act, wd_b[...], preferred_element_type=jnp.float32)
        w = (xs_ref[:, D_MODEL:D_MODEL + 1].astype(jnp.float32)
             + xs_ref[:, D_MODEL + 1:D_MODEL + 2].astype(jnp.float32))
        ys_ref[...] = (w * y).astype(jnp.bfloat16)


def _expert_call(block_expert, n_used, xs, w_eg, w_eu, w_ed):
    n_blocks = xs.shape[0] // EXPERT_BLOCK
    row_blk = lambda w: pl.BlockSpec((EXPERT_BLOCK, w), lambda b, be, nu: (jnp.minimum(b, nu[0] - 1), 0))
    return pl.pallas_call(
        _expert_kernel,
        grid_spec=pltpu.PrefetchScalarGridSpec(
            num_scalar_prefetch=2,
            grid=(n_blocks,),
            in_specs=[
                row_blk(XS_W),
                pl.BlockSpec((None, D_MODEL, D_EXPERT), lambda b, be, nu: (be[b], 0, 0)),
                pl.BlockSpec((None, D_MODEL, D_EXPERT), lambda b, be, nu: (be[b], 0, 0)),
                pl.BlockSpec((None, D_EXPERT, D_MODEL), lambda b, be, nu: (be[b], 0, 0)),
            ],
            out_specs=row_blk(D_MODEL),
            scratch_shapes=[
                pltpu.VMEM((D_MODEL, 2 * D_EXPERT), jnp.bfloat16),
                pltpu.VMEM((D_EXPERT, D_MODEL), jnp.bfloat16),
            ],
        ),
        out_shape=jax.ShapeDtypeStruct((n_blocks * EXPERT_BLOCK, D_MODEL), jnp.bfloat16),
        name="moe_experts",
    )(block_expert, n_used, xs, w_eg, w_eu, w_ed)


def _combine_kernel(tile0, off_ref, nch_ref, base_ref, ys_ref, slots_t_ref, x1_ref, gate2_ref, gpost_ref,
                    out_ref, buf, sem):
    j = pl.program_id(0)
    n = pl.num_programs(0)
    slot = j % 2
    tile = tile0 + j

    def copies(sl):
        def make(dst_row, src_row):
            return pltpu.make_async_copy(ys_ref.at[pl.ds(src_row, CHUNK), :],
                                         buf.at[sl, pl.ds(dst_row, CHUNK), :], sem.at[sl])
        return make

    @pl.when(j == 0)
    def _():
        buf[...] = jnp.zeros_like(buf)
        _run_copies(tile, off_ref, nch_ref, base_ref, copies(slot), lambda c: c.start())

    @pl.when(j + 1 < n)
    def _():
        _run_copies(tile + 1, off_ref, nch_ref, base_ref, copies(1 - slot), lambda c: c.start())

    _run_copies(tile, off_ref, nch_ref, base_ref, copies(slot), lambda c: c.wait())
    st = slots_t_ref[...]
    s1 = st[:, 0:1].astype(jnp.int32)
    s2 = st[:, 1:2].astype(jnp.int32)
    lane = lax.broadcasted_iota(jnp.int32, (st.shape[0], TILE_SLOTS), 1)
    unperm = jnp.where(lane == s1, 1.0, jnp.where(lane == s2, 1.0, 0.0)).astype(jnp.bfloat16)
    y = jnp.dot(unperm, buf[slot], preferred_element_type=jnp.float32)
    out_ref[...] = x1_ref[...] + gate2_ref[...] * (_rms(y) * gpost_ref[...])


def _combine_call(tables, ys, slots_t, x1, mod, mod_per_batch, g_post2, token_offset, bsz, seq):
    off, nch, base = tables
    t = DISPATCH_TILE
    tile0 = token_offset // t
    tokens = bsz * seq
    if mod_per_batch:
        per = seq // t
        gate2_spec = pl.BlockSpec((None, 1, D_MODEL), lambda j, *_: (j // per, 0, 5))
    else:
        gate2_spec = pl.BlockSpec((None, 1, D_MODEL), lambda j, *_: (0, 0, 5))
    y = pl.pallas_call(
        functools.partial(_combine_kernel, tile0),
        grid_spec=pltpu.PrefetchScalarGridSpec(
            num_scalar_prefetch=3,
            grid=(tokens // t,),
            in_specs=[
                pl.BlockSpec(memory_space=pl.ANY),
                pl.BlockSpec((None, t, LANES), lambda j, *_: (tile0 + j, 0, 0)),
                pl.BlockSpec((t, D_MODEL), lambda j, *_: (tile0 + j, 0)),
                gate2_spec,
                pl.BlockSpec((1, D_MODEL), lambda j, *_: (0, 0)),
            ],
            out_specs=pl.BlockSpec((t, D_MODEL), lambda j, *_: (j, 0)),
            scratch_shapes=[
                pltpu.VMEM((2, TILE_SLOTS, D_MODEL), jnp.bfloat16),
                pltpu.SemaphoreType.DMA((2,)),
            ],
        ),
        out_shape=jax.ShapeDtypeStruct((tokens, D_MODEL), jnp.float32),
        name="moe_combine",
    )(off, nch, base, ys, slots_t, x1, mod, g_post2)
    return y.reshape(bsz, seq, D_MODEL)


def _full_kv_spec(seq):
    return pl.BlockSpec((None, None, seq, D_V), lambda b, h, i: (b, h, 0, 0))


def kernel(x_prompt, x_sample, cache_k, cache_v, c, c_ctx, w_mod, b_mod, g_pre1, g_post1, g_pre2, g_post2,
           w_in, conv_w, lambda_q1, lambda_k1, lambda_q2, lambda_k2, subln_g, w_o, w_router_group,
           w_router_expert, w_exp_gate, w_exp_up, w_exp_down):
    n_lat = c.shape[0]
    cond = jnp.concatenate(
        [c, c_ctx[None, :], jnp.zeros((MOD_ROWS - n_lat - 1, D_MODEL), jnp.float32)], axis=0)
    mod, lam = _mod_call(cond, w_mod[0], b_mod, lambda_q1, lambda_k1, lambda_q2, lambda_k2)
    mod = mod.reshape(MOD_ROWS, 1, 6 * D_MODEL)
    mod_lat, mod_ctx = mod[:n_lat], mod[n_lat:n_lat + 1]

    w_in_b = w_in[0].astype(jnp.bfloat16)
    w_o_b = w_o[0].astype(jnp.bfloat16)
    w_router = jnp.concatenate(
        [w_router_group[0], jnp.zeros((D_MODEL, 8 - N_GROUPS), jnp.float32), w_router_expert[0],
         jnp.zeros((D_MODEL, LANES - 8 - N_EXPERTS), jnp.float32)], axis=1).astype(jnp.bfloat16)

    n_prompt = x_prompt.shape[0] * x_prompt.shape[1]
    n_sample = x_sample.shape[0] * x_sample.shape[1]
    total = n_prompt + n_sample

    def mixer(x, mod_x, per_batch, rope, ctx_kv, token_offset, carried):
        kv_dtype = jnp.bfloat16 if rope else jnp.float32
        q, k, v, gb, cu = _in_proj_call(x, mod_x, per_batch, g_pre1, w_in_b, rope, kv_dtype)
        kvs = []
        if ctx_kv is not None:
            ck, cv = ctx_kv
            spec = pl.BlockSpec((None, None, None, ck.shape[3], D_V), lambda b, h, i: (b, 0, h, 0, 0))
            kvs.append((ck, cv, spec))
        kvs.append((k, v, _full_kv_spec(x.shape[1])))
        o = _attn_call(lam, subln_g, q, kvs)
        shared = _out_proj_call(o, gb, cu, conv_w[0], w_o_b, x, mod_x, per_batch, g_post1, g_pre2, w_router,
                                total, token_offset, carried)
        return shared, k, v

    shared, kp, vp = mixer(x_prompt, mod_ctx, False, False, None, 0, None)
    (x1, h2, route), _, _ = mixer(x_sample, mod_lat, True, True, (cache_k, cache_v), n_prompt, shared)

    n_tiles = total // DISPATCH_TILE
    max_rows = 2 * total + n_tiles * N_EXPERTS * (CHUNK - 1) + N_EXPERTS * (EXPERT_BLOCK - CHUNK)
    n_blocks = -(-max_rows // EXPERT_BLOCK)
    slots, slots_t, counts = _plan_call(route)
    off, nch, base, tail_start, tail_chunks, block_expert, n_used = _layout_tables(counts, n_blocks)
    xs = _dispatch_call((off, nch, base, tail_start, tail_chunks), h2, slots, n_blocks)
    ys = _expert_call(block_expert, n_used, xs, w_exp_gate[0], w_exp_up[0], w_exp_down[0])
    tables = (off, nch, base)
    yp = _combine_call(tables, ys, slots_t, x1, mod_ctx, False, g_post2, 0, x_prompt.shape[0], x_prompt.shape[1])
    ysamp = _combine_call(tables, ys, slots_t, x1, mod_lat, True, g_post2, n_prompt,
                          x_sample.shape[0], x_sample.shape[1])
    return yp, ysamp, kp[:, None], vp[:, None]
```

```python
import functools
import math

import numpy as np
import jax
import jax.numpy as jnp
from jax import lax
from jax.experimental import pallas as pl
from jax.experimental.pallas import tpu as pltpu

D_MODEL = 1024
GRID_W = 64
N_HEADS = 4
D_QK = 64
D_V = 128
ATTN_W = N_HEADS * D_V
CONV_W = D_MODEL - ATTN_W
IN_W = 3 * ATTN_W + 3 * CONV_W
N_GROUPS = 4
EXP_PER_GROUP = 4
N_EXPERTS = N_GROUPS * EXP_PER_GROUP
D_EXPERT = 512
ROPE_BASE = 10000.0
EPS = 1e-6
LAM_INIT = 0.8 - 0.6 * math.exp(-0.3 * 0)

LANES = 128
BF16_SUBLANES = 16
MOD_ROWS = 16
TOKEN_TILE = 256
Q_TILE = 1024
KEY_CHUNK = 512
DISPATCH_TILE = 512
CHUNK = BF16_SUBLANES
TILE_SLOTS = 1280
EXPERT_BLOCK = 512
XS_W = D_MODEL + LANES
QK_SCALE = (1.0 / math.sqrt(D_QK)) * math.log2(math.e)

NT_DIMS = (((1,), (1,)), ((), ()))


def _rms(x):
    return x * lax.rsqrt(jnp.mean(x * x, axis=-1, keepdims=True) + EPS)


def _silu(x):
    return x * (1.0 / (1.0 + jnp.exp(-x)))


def _mod_kernel(cond_ref, w_ref, b_ref, lq1_ref, lk1_ref, lq2_ref, lk2_ref, mod_ref, lam_ref):
    s = _silu(cond_ref[...])
    m = lax.dot_general(s, w_ref[...], (((1,), (0,)), ((), ())),
                        precision=lax.Precision.HIGHEST, preferred_element_type=jnp.float32)
    mod_ref[...] = m + b_ref[...]
    a = jnp.sum(lq1_ref[...] * lk1_ref[...], axis=-1, keepdims=True)
    b = jnp.sum(lq2_ref[...] * lk2_ref[...], axis=-1, keepdims=True)
    lam_ref[...] = jnp.broadcast_to(jnp.exp(a) - jnp.exp(b) + LAM_INIT, lam_ref.shape)


def _mod_call(cond, w_mod, b_mod, lq1, lk1, lq2, lk2):
    n_col = 6 * D_MODEL
    col_tile = 1536
    small = pl.BlockSpec((1, D_QK), lambda j: (0, 0))
    return pl.pallas_call(
        _mod_kernel,
        grid=(n_col // col_tile,),
        in_specs=[
            pl.BlockSpec((MOD_ROWS, D_MODEL), lambda j: (0, 0)),
            pl.BlockSpec((D_MODEL, col_tile), lambda j: (0, j)),
            pl.BlockSpec((1, col_tile), lambda j: (0, j)),
            small, small, small, small,
        ],
        out_specs=[
            pl.BlockSpec((MOD_ROWS, col_tile), lambda j: (0, j)),
            pl.BlockSpec((1, LANES), lambda j: (0, 0)),
        ],
        out_shape=[
            jax.ShapeDtypeStruct((MOD_ROWS, n_col), jnp.float32),
            jax.ShapeDtypeStruct((1, LANES), jnp.float32),
        ],
        name="mod",
    )(cond, w_mod, b_mod, lq1, lk1, lq2, lk2)


def _in_proj_kernel(rope, x_ref, shift_ref, scale_ref, g_ref, w_ref, *rest):
    if rope:
        cos_ref, sina_ref, sinb_ref, q_ref, k_ref, v_ref, gb_ref, cu_ref = rest
    else:
        q_ref, k_ref, v_ref, gb_ref, cu_ref = rest
    x = x_ref[...]
    h = _rms(x) * g_ref[...] * (1.0 + scale_ref[...]) + shift_ref[...]
    h = h.astype(jnp.bfloat16)

    def proj(lo, hi):
        return jnp.dot(h, w_ref[:, lo:hi], preferred_element_type=jnp.float32)

    def rot(t):
        return (t * cos_ref[...] + pltpu.roll(t, LANES - 16, axis=1) * sina_ref[...]
                + pltpu.roll(t, 16, axis=1) * sinb_ref[...])

    for hd in range(N_HEADS):
        q = proj(hd * D_V, (hd + 1) * D_V)
        k = proj(ATTN_W + hd * D_V, ATTN_W + (hd + 1) * D_V)
        v = proj(2 * ATTN_W + hd * D_V, 2 * ATTN_W + (hd + 1) * D_V)
        if rope:
            q = rot(q)
            k = rot(k)
        q_ref[hd] = (q * QK_SCALE).astype(q_ref.dtype)
        k_ref[hd] = k.astype(k_ref.dtype)
        v_ref[hd] = v.astype(v_ref.dtype)
    c0 = 3 * ATTN_W
    gb_ref[...] = proj(c0, c0 + CONV_W).astype(gb_ref.dtype)
    gc = proj(c0 + CONV_W, c0 + 2 * CONV_W)
    u = proj(c0 + 2 * CONV_W, c0 + 3 * CONV_W)
    cu_ref[...] = (gc * u).astype(cu_ref.dtype)


def _rope_tables(seq):
    n_rows = seq // GRID_W
    row = np.repeat(np.arange(n_rows), GRID_W).astype(np.float64)
    col = np.tile(np.arange(GRID_W), n_rows).astype(np.float64)
    nf = D_QK // 4
    inv = ROPE_BASE ** (-np.arange(nf, dtype=np.float64) / nf)
    ar = row[:, None] * inv
    ac = col[:, None] * inv
    ang = np.concatenate([ar, ar, ac, ac], axis=-1)
    ang = np.concatenate([ang, ang], axis=-1)
    first_half = (np.arange(LANES) % 32) < 16
    cos = np.cos(ang)
    sin = np.sin(ang)
    sina = np.where(first_half, -sin, 0.0)
    sinb = np.where(first_half, 0.0, sin)
    return tuple(jnp.asarray(t, dtype=jnp.float32) for t in (cos, sina, sinb))


def _in_proj_call(x, mod, mod_per_batch, g_pre1, w_in, rope, kv_dtype):
    bsz, seq, _ = x.shape
    tm = TOKEN_TILE

    def mod_spec(chunk):
        if mod_per_batch:
            return pl.BlockSpec((None, 1, D_MODEL), lambda b, i: (b, 0, chunk))
        return pl.BlockSpec((None, 1, D_MODEL), lambda b, i: (0, 0, chunk))

    in_specs = [
        pl.BlockSpec((None, tm, D_MODEL), lambda b, i: (b, i, 0)),
        mod_spec(0), mod_spec(1),
        pl.BlockSpec((1, D_MODEL), lambda b, i: (0, 0)),
        pl.BlockSpec((D_MODEL, IN_W), lambda b, i: (0, 0)),
    ]
    args = [x, mod, mod, g_pre1, w_in]
    if rope:
        in_specs += [pl.BlockSpec((tm, LANES), lambda b, i: (i, 0))] * 3
        args += list(_rope_tables(seq))
    head_spec = pl.BlockSpec((None, N_HEADS, tm, D_V), lambda b, i: (b, 0, i, 0))
    tok_spec = pl.BlockSpec((None, tm, CONV_W), lambda b, i: (b, i, 0))
    return pl.pallas_call(
        functools.partial(_in_proj_kernel, rope),
        grid=(bsz, seq // tm),
        in_specs=in_specs,
        out_specs=[head_spec, head_spec, head_spec, tok_spec, tok_spec],
        out_shape=[
            jax.ShapeDtypeStruct((bsz, N_HEADS, seq, D_V), jnp.bfloat16),
            jax.ShapeDtypeStruct((bsz, N_HEADS, seq, D_V), kv_dtype),
            jax.ShapeDtypeStruct((bsz, N_HEADS, seq, D_V), kv_dtype),
            jax.ShapeDtypeStruct((bsz, seq, CONV_W), jnp.bfloat16),
            jax.ShapeDtypeStruct((bsz, seq, CONV_W), jnp.bfloat16),
        ],
        name="in_proj_rope" if rope else "in_proj",
    )(*args)


def _attn_kernel(n_kv, lam_ref, g_ref, q_ref, *rest):
    kv_refs = rest[:2 * n_kv]
    o_ref = rest[2 * n_kv]
    q = q_ref[...]
    lane = lax.broadcasted_iota(jnp.int32, q.shape, 1)
    zero = jnp.zeros_like(q)
    q1 = jnp.where(lane < D_QK, q, zero)
    q2 = jnp.where(lane >= D_QK, q, zero)
    tq = q.shape[0]
    acc = [jnp.zeros((tq, 2 * D_V), jnp.float32) for _ in range(2)]
    m = [jnp.full((tq, 1), -1e30, jnp.float32) for _ in range(2)]
    for j in range(n_kv):
        k_ref, v_ref = kv_refs[2 * j], kv_refs[2 * j + 1]
        n_keys = k_ref.shape[0]
        ck = min(KEY_CHUNK, n_keys)
        ones = jnp.ones((ck, D_V), jnp.bfloat16)
        for c in range(n_keys // ck):
            k = k_ref[c * ck:(c + 1) * ck, :].astype(jnp.bfloat16)
            v1 = jnp.concatenate([v_ref[c * ck:(c + 1) * ck, :].astype(jnp.bfloat16), ones], axis=1)
            for x, qh in enumerate((q1, q2)):
                s = lax.dot_general(qh, k, NT_DIMS, preferred_element_type=jnp.float32)
                m_new = jnp.maximum(m[x], jnp.max(s, axis=-1, keepdims=True))
                p = jnp.exp2(s - m_new).astype(jnp.bfloat16)
                acc[x] = jnp.exp2(m[x] - m_new) * acc[x] + jnp.dot(p, v1, preferred_element_type=jnp.float32)
                m[x] = m_new
    o = (acc[0][:, 0:D_V] / acc[0][:, D_V:2 * D_V]
         - lam_ref[0:1, 0:1] * (acc[1][:, 0:D_V] / acc[1][:, D_V:2 * D_V]))
    o_ref[...] = (_rms(o) * (g_ref[...] * (1.0 - LAM_INIT))).astype(o_ref.dtype)


def _attn_call(lam, subln_g, q, kvs):
    bsz, _, seq, _ = q.shape
    tq = min(Q_TILE, seq)
    in_specs = [
        pl.BlockSpec((1, LANES), lambda b, h, i: (0, 0)),
        pl.BlockSpec((1, D_V), lambda b, h, i: (0, 0)),
        pl.BlockSpec((None, None, tq, D_V), lambda b, h, i: (b, h, i, 0)),
    ]
    args = [lam, subln_g, q]
    for k, v, spec in kvs:
        in_specs += [spec, spec]
        args += [k, v]
    return pl.pallas_call(
        functools.partial(_attn_kernel, len(kvs)),
        grid=(bsz, N_HEADS, seq // tq),
        in_specs=in_specs,
        out_specs=pl.BlockSpec((None, tq, D_V), lambda b, h, i: (b, i, h)),
        out_shape=jax.ShapeDtypeStruct((bsz, seq, ATTN_W), jnp.bfloat16),
        name="attn%d" % len(kvs),
    )(*args)


def _route(logits_t):
    lg = [logits_t[g:g + 1, :] for g in range(N_GROUPS)]
    mg = functools.reduce(jnp.maximum, lg)
    p_sel = 1.0 / functools.reduce(jnp.add, [jnp.exp(t - mg) for t in lg])
    g_sel = jnp.full(mg.shape, N_GROUPS - 1, jnp.int32)
    for g in range(N_GROUPS - 2, -1, -1):
        g_sel = jnp.where(lg[g] == mg, g, g_sel)
    le = []
    for j in range(EXP_PER_GROUP):
        t = jnp.zeros_like(mg)
        for g in range(N_GROUPS):
            r = 8 + g * EXP_PER_GROUP + j
            t = jnp.where(g_sel == g, logits_t[r:r + 1, :], t)
        le.append(t)
    m1 = functools.reduce(jnp.maximum, le)
    i1 = jnp.full(mg.shape, EXP_PER_GROUP - 1, jnp.int32)
    for j in range(EXP_PER_GROUP - 2, -1, -1):
        i1 = jnp.where(le[j] == m1, j, i1)
    neg = jnp.float32(-jnp.inf)
    rest = [jnp.where(i1 == j, neg, le[j]) for j in range(EXP_PER_GROUP)]
    m2 = functools.reduce(jnp.maximum, rest)
    i2 = jnp.full(mg.shape, EXP_PER_GROUP - 1, jnp.int32)
    for j in range(EXP_PER_GROUP - 2, -1, -1):
        i2 = jnp.where(rest[j] == m2, j, i2)
    e2 = jnp.exp(m2 - m1)
    w1 = p_sel / (1.0 + e2)
    w2 = p_sel * e2 / (1.0 + e2)
    base = g_sel * EXP_PER_GROUP
    return (base + i1).astype(jnp.float32), (base + i2).astype(jnp.float32), w1, w2


def _out_proj_kernel(n_tiles, o_ref, gb_ref, cu_ref, cup_ref, cun_ref, cw_ref, wo_ref, x_ref,
                     gate1_ref, shift2_ref, scale2_ref, gpost_ref, gpre_ref, wr_ref, *rest):
    x1_ref, h2_ref, route_ref = rest[-3:]
    i = pl.program_id(1)
    tm = cu_ref.shape[0]
    cu = cu_ref[...].astype(jnp.float32)
    row = lax.broadcasted_iota(jnp.int32, cu.shape, 0)
    prev_row = jnp.where(i > 0, cup_ref[BF16_SUBLANES - 1:BF16_SUBLANES, :].astype(jnp.float32), 0.0)
    next_row = jnp.where(i < n_tiles - 1, cun_ref[0:1, :].astype(jnp.float32), 0.0)
    prev = jnp.where(row == 0, prev_row, pltpu.roll(cu, 1, axis=0))
    nxt = jnp.where(row == tm - 1, next_row, pltpu.roll(cu, tm - 1, axis=0))
    conv = cw_ref[0:1, :] * prev + cw_ref[1:2, :] * cu + cw_ref[2:3, :] * nxt
    yc = (gb_ref[...].astype(jnp.float32) * conv).astype(jnp.bfloat16)
    out = (jnp.dot(o_ref[...], wo_ref[0:ATTN_W, :], preferred_element_type=jnp.float32)
           + jnp.dot(yc, wo_ref[ATTN_W:D_MODEL, :], preferred_element_type=jnp.float32))
    x1 = x_ref[...] + gate1_ref[...] * (_rms(out) * gpost_ref[...])
    x1_ref[...] = x1
    h2 = (_rms(x1) * gpre_ref[...] * (1.0 + scale2_ref[...]) + shift2_ref[...]).astype(jnp.bfloat16)
    h2_ref[...] = h2
    logits = jnp.dot(h2, wr_ref[...], preferred_element_type=jnp.float32)
    e1, e2, w1, w2 = _route(logits.T)
    route_ref[...] = jnp.concatenate([e1, e2, w1, w2, jnp.zeros((4, tm), jnp.float32)], axis=0)


def _out_proj_call(o, gb, cu, conv_w, w_o, x, mod, mod_per_batch, g_post1, g_pre2, w_router,
                   total_tokens, token_offset, carried):
    bsz, seq, _ = x.shape
    tm = TOKEN_TILE
    n_tiles = seq // tm
    halo = tm // BF16_SUBLANES
    n_halo = seq // BF16_SUBLANES
    tile0 = token_offset // tm
    per_route = DISPATCH_TILE // tm

    def mod_spec(chunk):
        if mod_per_batch:
            return pl.BlockSpec((None, 1, D_MODEL), lambda b, i: (b, 0, chunk))
        return pl.BlockSpec((None, 1, D_MODEL), lambda b, i: (0, 0, chunk))

    def vec_spec():
        return pl.BlockSpec((1, D_MODEL), lambda b, i: (0, 0))

    tok = lambda w: pl.BlockSpec((None, tm, w), lambda b, i: (b, i, 0))
    flat = lambda w: pl.BlockSpec((tm, w), lambda b, i: (tile0 + b * n_tiles + i, 0))
    route_spec = pl.BlockSpec(
        (None, 8, tm),
        lambda b, i: ((tile0 + b * n_tiles + i) // per_route, 0, (tile0 + b * n_tiles + i) % per_route))
    in_specs = [
        tok(ATTN_W), tok(CONV_W), tok(CONV_W),
        pl.BlockSpec((None, BF16_SUBLANES, CONV_W), lambda b, i: (b, jnp.maximum(i * halo - 1, 0), 0)),
        pl.BlockSpec((None, BF16_SUBLANES, CONV_W), lambda b, i: (b, jnp.minimum((i + 1) * halo, n_halo - 1), 0)),
        pl.BlockSpec((3, CONV_W), lambda b, i: (0, 0)),
        pl.BlockSpec((D_MODEL, D_MODEL), lambda b, i: (0, 0)),
        tok(D_MODEL),
        mod_spec(2), mod_spec(3), mod_spec(4),
        vec_spec(), vec_spec(),
        pl.BlockSpec((D_MODEL, LANES), lambda b, i: (0, 0)),
    ]
    args = [o, gb, cu, cu, cu, conv_w, w_o, x, mod, mod, mod, g_post1, g_pre2, w_router]
    aliases = {}
    if carried is not None:
        for j, arr in enumerate(carried):
            aliases[len(args)] = j
            in_specs.append(pl.BlockSpec(memory_space=pl.ANY))
            args.append(arr)
    return pl.pallas_call(
        functools.partial(_out_proj_kernel, n_tiles),
        grid=(bsz, n_tiles),
        in_specs=in_specs,
        out_specs=[flat(D_MODEL), flat(D_MODEL), route_spec],
        out_shape=[
            jax.ShapeDtypeStruct((total_tokens, D_MODEL), jnp.float32),
            jax.ShapeDtypeStruct((total_tokens, D_MODEL), jnp.bfloat16),
            jax.ShapeDtypeStruct((total_tokens // DISPATCH_TILE, 8, DISPATCH_TILE), jnp.float32),
        ],
        input_output_aliases=aliases,
        name="out_proj",
    )(*args)


def _plan_kernel(route_ref, slots_ref, slots_t_ref, counts_ref):
    r = route_ref[...]
    t = r.shape[1]
    e1 = r[0:1, :].astype(jnp.int32)
    e2 = r[1:2, :].astype(jnp.int32)
    eid = lax.broadcasted_iota(jnp.int32, (N_EXPERTS, t), 0)
    hot1 = jnp.where(eid == e1, 1.0, 0.0)
    hot2 = jnp.where(eid == e2, 1.0, 0.0)
    hot = jnp.concatenate([hot1, hot2], axis=0).astype(jnp.bfloat16)
    before = (lax.broadcasted_iota(jnp.int32, (t, t), 0) < lax.broadcasted_iota(jnp.int32, (t, t), 1))
    before = jnp.where(before, 1.0, 0.0).astype(jnp.bfloat16)
    rank = jnp.dot(hot, before, preferred_element_type=jnp.float32)
    n1 = jnp.sum(hot1, axis=1, keepdims=True)
    n2 = jnp.sum(hot2, axis=1, keepdims=True)
    ones = jnp.ones((8, t), jnp.bfloat16)
    cnt_row = lax.dot_general(ones, (hot1 + hot2).astype(jnp.bfloat16), NT_DIMS,
                              preferred_element_type=jnp.float32)[0:1, :]
    pad_row = jnp.floor((cnt_row + (CHUNK - 1)) * (1.0 / CHUNK)) * CHUNK
    lower = (lax.broadcasted_iota(jnp.int32, (N_EXPERTS, N_EXPERTS), 1)
             < lax.broadcasted_iota(jnp.int32, (N_EXPERTS, N_EXPERTS), 0))
    base = jnp.sum(jnp.where(lower, pad_row, 0.0), axis=1, keepdims=True)
    slot1 = jnp.sum(hot1 * (base + rank[:N_EXPERTS]), axis=0, keepdims=True)
    slot2 = jnp.sum(hot2 * (base + n1 + rank[N_EXPERTS:]), axis=0, keepdims=True)
    slots_ref[...] = jnp.concatenate([slot1, slot2, r[2:4, :], jnp.zeros((4, t), jnp.float32)], axis=0)
    wide = jnp.concatenate([slot1, slot2, jnp.zeros((LANES - 2, t), jnp.float32)], axis=0)
    slots_t_ref[...] = wide.T
    counts_ref[...] = jnp.broadcast_to(n1 + n2, counts_ref.shape)


def _plan_call(route):
    n_tiles, _, t = route.shape
    return pl.pallas_call(
        _plan_kernel,
        grid=(n_tiles,),
        in_specs=[pl.BlockSpec((None, 8, t), lambda i: (i, 0, 0))],
        out_specs=[
            pl.BlockSpec((None, 8, t), lambda i: (i, 0, 0)),
            pl.BlockSpec((None, t, LANES), lambda i: (i, 0, 0)),
            pl.BlockSpec((None, N_EXPERTS, LANES), lambda i: (i, 0, 0)),
        ],
        out_shape=[
            jax.ShapeDtypeStruct((n_tiles, 8, t), jnp.float32),
            jax.ShapeDtypeStruct((n_tiles, t, LANES), jnp.float32),
            jax.ShapeDtypeStruct((n_tiles, N_EXPERTS, LANES), jnp.float32),
        ],
        name="moe_plan",
    )(route)


def _layout_tables(counts, n_blocks):
    cnt = counts[:, :, 0].astype(jnp.int32)
    nch = (cnt + (CHUNK - 1)) // CHUNK
    csum = jnp.cumsum(nch, axis=0)
    rows = CHUNK * csum[-1]
    blocks = (rows + (EXPERT_BLOCK - 1)) // EXPERT_BLOCK
    blk_end = jnp.cumsum(blocks)
    blk_start = blk_end - blocks
    off = blk_start[None, :] * EXPERT_BLOCK + CHUNK * (csum - nch)
    base = CHUNK * (jnp.cumsum(nch, axis=1) - nch)
    n_used = blk_end[-1]
    bc = jnp.minimum(jnp.arange(n_blocks, dtype=jnp.int32), n_used - 1)
    block_expert = jnp.sum((bc[:, None] >= blk_end[None, :]).astype(jnp.int32), axis=1)
    tail_start = blk_start * EXPERT_BLOCK + rows
    tail_chunks = (blocks * EXPERT_BLOCK - rows) // CHUNK
    flat = lambda a: a.reshape(-1).astype(jnp.int32)
    return (flat(off), flat(nch), flat(base), flat(tail_start), flat(tail_chunks),
            flat(block_expert), flat(n_used))


def _run_copies(tile, off_ref, nch_ref, base_ref, make_copy, action):
    for e in range(N_EXPERTS):
        idx = tile * N_EXPERTS + e
        off = off_ref[idx]
        base = base_ref[idx]

        def body(m, carry, off=off, base=base):
            action(make_copy(pl.multiple_of(base + m * CHUNK, CHUNK), pl.multiple_of(off + m * CHUNK, CHUNK)))
            return carry

        lax.fori_loop(0, nch_ref[idx], body, 0)


def _dispatch_kernel(off_ref, nch_ref, base_ref, tail_start_ref, tail_chunks_ref,
                     h_ref, slots_ref, xs_ref, buf, zbuf, sem, zsem):
    i = pl.program_id(0)
    n = pl.num_programs(0)
    slot = i % 2
    s = slots_ref[...]
    t = s.shape[1]
    s1 = s[0:1, :].astype(jnp.int32)
    s2 = s[1:2, :].astype(jnp.int32)
    rid = lax.broadcasted_iota(jnp.int32, (TILE_SLOTS, t), 0)
    m1 = rid == s1
    m2 = rid == s2
    perm = jnp.where(m1, 1.0, jnp.where(m2, 1.0, 0.0)).astype(jnp.bfloat16)
    rows = jnp.dot(perm, h_ref[...], preferred_element_type=jnp.float32)
    w = jnp.sum(jnp.where(m1, s[2:3, :], 0.0) + jnp.where(m2, s[3:4, :], 0.0), axis=1, keepdims=True)
    hi = w.astype(jnp.bfloat16).astype(jnp.float32)
    lane = lax.broadcasted_iota(jnp.int32, (TILE_SLOTS, LANES), 1)
    aux = jnp.where(lane == 0, hi, jnp.where(lane == 1, w - hi, 0.0))
    buf[slot, :, 0:D_MODEL] = rows.astype(jnp.bfloat16)
    buf[slot, :, D_MODEL:XS_W] = aux.astype(jnp.bfloat16)

    def copies(sl):
        def make(src_row, dst_row):
            return pltpu.make_async_copy(buf.at[sl, pl.ds(src_row, CHUNK), :],
                                         xs_ref.at[pl.ds(dst_row, CHUNK), :], sem.at[sl])
        return make

    _run_copies(i, off_ref, nch_ref, base_ref, copies(slot), lambda c: c.start())

    @pl.when(i > 0)
    def _():
        _run_copies(i - 1, off_ref, nch_ref, base_ref, copies(1 - slot), lambda c: c.wait())

    @pl.when(i == n - 1)
    def _():
        zbuf[...] = jnp.zeros_like(zbuf)

        def tails(action):
            for e in range(N_EXPERTS):
                start = tail_start_ref[e]

                def body(m, carry, start=start):
                    action(pltpu.make_async_copy(
                        zbuf, xs_ref.at[pl.ds(pl.multiple_of(start + m * CHUNK, CHUNK), CHUNK), :], zsem))
                    return carry

                lax.fori_loop(0, tail_chunks_ref[e], body, 0)

        tails(lambda c: c.start())
        tails(lambda c: c.wait())
        _run_copies(i, off_ref, nch_ref, base_ref, copies(slot), lambda c: c.wait())


def _dispatch_call(tables, h2, slots, n_blocks):
    off, nch, base, tail_start, tail_chunks = tables
    tokens = h2.shape[0]
    t = DISPATCH_TILE
    return pl.pallas_call(
        _dispatch_kernel,
        grid_spec=pltpu.PrefetchScalarGridSpec(
            num_scalar_prefetch=5,
            grid=(tokens // t,),
            in_specs=[
                pl.BlockSpec((t, D_MODEL), lambda i, *_: (i, 0)),
                pl.BlockSpec((None, 8, t), lambda i, *_: (i, 0, 0)),
            ],
            out_specs=pl.BlockSpec(memory_space=pl.ANY),
            scratch_shapes=[
                pltpu.VMEM((2, TILE_SLOTS, XS_W), jnp.bfloat16),
                pltpu.VMEM((CHUNK, XS_W), jnp.bfloat16),
                pltpu.SemaphoreType.DMA((2,)),
                pltpu.SemaphoreType.DMA(()),
            ],
        ),
        out_shape=jax.ShapeDtypeStruct((n_blocks * EXPERT_BLOCK, XS_W), jnp.bfloat16),
        name="moe_dispatch",
    )(off, nch, base, tail_start, tail_chunks, h2, slots)


def _expert_kernel(bexp_ref, nused_ref, xs_ref, wg_ref, wu_ref, wd_ref, ys_ref, wgu_b, wd_b):
    b = pl.program_id(0)
    e = bexp_ref[b]
    prev = bexp_ref[jnp.maximum(b - 1, 0)]

    @pl.when((b == 0) | (e != prev))
    def _():
        wgu_b[:, 0:D_EXPERT] = wg_ref[...].astype(jnp.bfloat16)
        wgu_b[:, D_EXPERT:2 * D_EXPERT] = wu_ref[...].astype(jnp.bfloat16)
        wd_b[...] = wd_ref[...].astype(jnp.bfloat16)

    @pl.when(b < nused_ref[0])
    def _():
        x = xs_ref[:, 0:D_MODEL]
        gu = jnp.dot(x, wgu_b[...], preferred_element_type=jnp.float32)
        act = (_silu(gu[:, 0:D_EXPERT]) * gu[:, D_EXPERT:2 * D_EXPERT]).astype(jnp.bfloat16)
        y = jnp.dot(act, wd_b[...], preferred_element_type=jnp.float32)
        w = (xs_ref[:, D_MODEL:D_MODEL + 1].astype(jnp.float32)
             + xs_ref[:, D_MODEL + 1:D_MODEL + 2].astype(jnp.float32))
        ys_ref[...] = (w * y).astype(jnp.bfloat16)


def _expert_call(block_expert, n_used, xs, w_eg, w_eu, w_ed):
    n_blocks = xs.shape[0] // EXPERT_BLOCK
    row_blk = lambda w: pl.BlockSpec((EXPERT_BLOCK, w), lambda b, be, nu: (jnp.minimum(b, nu[0] - 1), 0))
    return pl.pallas_call(
        _expert_kernel,
        grid_spec=pltpu.PrefetchScalarGridSpec(
            num_scalar_prefetch=2,
            grid=(n_blocks,),
            in_specs=[
                row_blk(XS_W),
                pl.BlockSpec((None, D_MODEL, D_EXPERT), lambda b, be, nu: (be[b], 0, 0)),
                pl.BlockSpec((None, D_MODEL, D_EXPERT), lambda b, be, nu: (be[b], 0, 0)),
                pl.BlockSpec((None, D_EXPERT, D_MODEL), lambda b, be, nu: (be[b], 0, 0)),
            ],
            out_specs=row_blk(D_MODEL),
            scratch_shapes=[
                pltpu.VMEM((D_MODEL, 2 * D_EXPERT), jnp.bfloat16),
                pltpu.VMEM((D_EXPERT, D_MODEL), jnp.bfloat16),
            ],
        ),
        out_shape=jax.ShapeDtypeStruct((n_blocks * EXPERT_BLOCK, D_MODEL), jnp.bfloat16),
        name="moe_experts",
    )(block_expert, n_used, xs, w_eg, w_eu, w_ed)


def _combine_kernel(tile0, off_ref, nch_ref, base_ref, ys_ref, slots_t_ref, x1_ref, gate2_ref, gpost_ref,
                    out_ref, buf, sem):
    j = pl.program_id(0)
    n = pl.num_programs(0)
    slot = j % 2
    tile = tile0 + j

    def copies(sl):
        def make(dst_row, src_row):
            return pltpu.make_async_copy(ys_ref.at[pl.ds(src_row, CHUNK), :],
                                         buf.at[sl, pl.ds(dst_row, CHUNK), :], sem.at[sl])
        return make

    @pl.when(j == 0)
    def _():
        buf[...] = jnp.zeros_like(buf)
        _run_copies(tile, off_ref, nch_ref, base_ref, copies(slot), lambda c: c.start())

    @pl.when(j + 1 < n)
    def _():
        _run_copies(tile + 1, off_ref, nch_ref, base_ref, copies(1 - slot), lambda c: c.start())

    _run_copies(tile, off_ref, nch_ref, base_ref, copies(slot), lambda c: c.wait())
    st = slots_t_ref[...]
    s1 = st[:, 0:1].astype(jnp.int32)
    s2 = st[:, 1:2].astype(jnp.int32)
    lane = lax.broadcasted_iota(jnp.int32, (st.shape[0], TILE_SLOTS), 1)
    unperm = jnp.where(lane == s1, 1.0, jnp.where(lane == s2, 1.0, 0.0)).astype(jnp.bfloat16)
    y = jnp.dot(unperm, buf[slot], preferred_element_type=jnp.float32)
    out_ref[...] = x1_ref[...] + gate2_ref[...] * (_rms(y) * gpost_ref[...])


def _combine_call(tables, ys, slots_t, x1, mod, mod_per_batch, g_post2, token_offset, bsz, seq):
    off, nch, base = tables
    t = DISPATCH_TILE
    tile0 = token_offset // t
    tokens = bsz * seq
    if mod_per_batch:
        per = seq // t
        gate2_spec = pl.BlockSpec((None, 1, D_MODEL), lambda j, *_: (j // per, 0, 5))
    else:
        gate2_spec = pl.BlockSpec((None, 1, D_MODEL), lambda j, *_: (0, 0, 5))
    y = pl.pallas_call(
        functools.partial(_combine_kernel, tile0),
        grid_spec=pltpu.PrefetchScalarGridSpec(
            num_scalar_prefetch=3,
            grid=(tokens // t,),
            in_specs=[
                pl.BlockSpec(memory_space=pl.ANY),
                pl.BlockSpec((None, t, LANES), lambda j, *_: (tile0 + j, 0, 0)),
                pl.BlockSpec((t, D_MODEL), lambda j, *_: (tile0 + j, 0)),
                gate2_spec,
                pl.BlockSpec((1, D_MODEL), lambda j, *_: (0, 0)),
            ],
            out_specs=pl.BlockSpec((t, D_MODEL), lambda j, *_: (j, 0)),
            scratch_shapes=[
                pltpu.VMEM((2, TILE_SLOTS, D_MODEL), jnp.bfloat16),
                pltpu.SemaphoreType.DMA((2,)),
            ],
        ),
        out_shape=jax.ShapeDtypeStruct((tokens, D_MODEL), jnp.float32),
        name="moe_combine",
    )(off, nch, base, ys, slots_t, x1, mod, g_post2)
    return y.reshape(bsz, seq, D_MODEL)


def _full_kv_spec(seq):
    return pl.BlockSpec((None, None, seq, D_V), lambda b, h, i: (b, h, 0, 0))


def kernel(x_prompt, x_sample, cache_k, cache_v, c, c_ctx, w_mod, b_mod, g_pre1, g_post1, g_pre2, g_post2,
           w_in, conv_w, lambda_q1, lambda_k1, lambda_q2, lambda_k2, subln_g, w_o, w_router_group,
           w_router_expert, w_exp_gate, w_exp_up, w_exp_down):
    n_lat = c.shape[0]
    cond = jnp.concatenate(
        [c, c_ctx[None, :], jnp.zeros((MOD_ROWS - n_lat - 1, D_MODEL), jnp.float32)], axis=0)
    mod, lam = _mod_call(cond, w_mod[0], b_mod, lambda_q1, lambda_k1, lambda_q2, lambda_k2)
    mod = mod.reshape(MOD_ROWS, 1, 6 * D_MODEL)
    mod_lat, mod_ctx = mod[:n_lat], mod[n_lat:n_lat + 1]

    w_in_b = w_in[0].astype(jnp.bfloat16)
    w_o_b = w_o[0].astype(jnp.bfloat16)
    w_router = jnp.concatenate(
        [w_router_group[0], jnp.zeros((D_MODEL, 8 - N_GROUPS), jnp.float32), w_router_expert[0],
         jnp.zeros((D_MODEL, LANES - 8 - N_EXPERTS), jnp.float32)], axis=1).astype(jnp.bfloat16)

    n_prompt = x_prompt.shape[0] * x_prompt.shape[1]
    n_sample = x_sample.shape[0] * x_sample.shape[1]
    total = n_prompt + n_sample

    def mixer(x, mod_x, per_batch, rope, ctx_kv, token_offset, carried):
        kv_dtype = jnp.bfloat16 if rope else jnp.float32
        q, k, v, gb, cu = _in_proj_call(x, mod_x, per_batch, g_pre1, w_in_b, rope, kv_dtype)
        kvs = []
        if ctx_kv is not None:
            ck, cv = ctx_kv
            spec = pl.BlockSpec((None, None, None, ck.shape[3], D_V), lambda b, h, i: (b, 0, h, 0, 0))
            kvs.append((ck, cv, spec))
        kvs.append((k, v, _full_kv_spec(x.shape[1])))
        o = _attn_call(lam, subln_g, q, kvs)
        shared = _out_proj_call(o, gb, cu, conv_w[0], w_o_b, x, mod_x, per_batch, g_post1, g_pre2, w_router,
                                total, token_offset, carried)
        return shared, k, v

    shared, kp, vp = mixer(x_prompt, mod_ctx, False, False, None, 0, None)
    (x1, h2, route), _, _ = mixer(x_sample, mod_lat, True, True, (cache_k, cache_v), n_prompt, shared)

    n_tiles = total // DISPATCH_TILE
    max_rows = 2 * total + n_tiles * N_EXPERTS * (CHUNK - 1) + N_EXPERTS * (EXPERT_BLOCK - CHUNK)
    n_blocks = -(-max_rows // EXPERT_BLOCK)
    slots, slots_t, counts = _plan_call(route)
    off, nch, base, tail_start, tail_chunks, block_expert, n_used = _layout_tables(counts, n_blocks)
    xs = _dispatch_call((off, nch, base, tail_start, tail_chunks), h2, slots, n_blocks)
    ys = _expert_call(block_expert, n_used, xs, w_exp_gate[0], w_exp_up[0], w_exp_down[0])
    tables = (off, nch, base)
    yp = _combine_call(tables, ys, slots_t, x1, mod_ctx, False, g_post2, 0, x_prompt.shape[0], x_prompt.shape[1])
    ysamp = _combine_call(tables, ys, slots_t, x1, mod_lat, True, g_post2, n_prompt,
                          x_sample.shape[0], x_sample.shape[1])
    return yp, ysamp, kp[:, None], vp[:, None]
```

```python
import functools
import math

import numpy as np
import jax
import jax.numpy as jnp
from jax import lax
from jax.experimental import pallas as pl
from jax.experimental.pallas import tpu as pltpu

D_MODEL = 1024
GRID_W = 64
N_HEADS = 4
D_QK = 64
D_V = 128
ATTN_W = N_HEADS * D_V
CONV_W = D_MODEL - ATTN_W
IN_W = 3 * ATTN_W + 3 * CONV_W
N_GROUPS = 4
EXP_PER_GROUP = 4
N_EXPERTS = N_GROUPS * EXP_PER_GROUP
D_EXPERT = 512
ROPE_BASE = 10000.0
EPS = 1e-6
LAM_INIT = 0.8 - 0.6 * math.exp(-0.3 * 0)

LANES = 128
F32_SUBLANES = 8
BF16_SUBLANES = 16
MOD_ROWS = 16
TOKEN_TILE = 512
Q_TILE = 1024
KEY_CHUNK = 512
DISPATCH_TILE = 512
CHUNK = BF16_SUBLANES
TILE_SLOTS = 1280
N_CHUNKS = TILE_SLOTS // CHUNK
EXPERT_BLOCK = 512
XS_W = D_MODEL + LANES
QK_SCALE = (1.0 / math.sqrt(D_QK)) * math.log2(math.e)

NT_DIMS = (((1,), (1,)), ((), ()))


def _rms(x):
    return x * lax.rsqrt(jnp.mean(x * x, axis=-1, keepdims=True) + EPS)


def _silu(x):
    return x * (1.0 / (1.0 + jnp.exp(-x)))


def _mod_kernel(cond_ref, w_ref, b_ref, lq1_ref, lk1_ref, lq2_ref, lk2_ref, mod_ref, lam_ref):
    s = _silu(cond_ref[...])
    m = lax.dot_general(s, w_ref[...], (((1,), (0,)), ((), ())),
                        precision=lax.Precision.HIGHEST, preferred_element_type=jnp.float32)
    mod_ref[...] = m + b_ref[...]
    a = jnp.sum(lq1_ref[...] * lk1_ref[...], axis=-1, keepdims=True)
    b = jnp.sum(lq2_ref[...] * lk2_ref[...], axis=-1, keepdims=True)
    lam_ref[...] = jnp.broadcast_to(jnp.exp(a) - jnp.exp(b) + LAM_INIT, lam_ref.shape)


def _mod_call(cond, w_mod, b_mod, lq1, lk1, lq2, lk2):
    n_col = 6 * D_MODEL
    col_tile = 1536
    small = pl.BlockSpec((1, D_QK), lambda j: (0, 0))
    return pl.pallas_call(
        _mod_kernel,
        grid=(n_col // col_tile,),
        in_specs=[
            pl.BlockSpec((MOD_ROWS, D_MODEL), lambda j: (0, 0)),
            pl.BlockSpec((D_MODEL, col_tile), lambda j: (0, j)),
            pl.BlockSpec((1, col_tile), lambda j: (0, j)),
            small, small, small, small,
        ],
        out_specs=[
            pl.BlockSpec((MOD_ROWS, col_tile), lambda j: (0, j)),
            pl.BlockSpec((1, LANES), lambda j: (0, 0)),
        ],
        out_shape=[
            jax.ShapeDtypeStruct((MOD_ROWS, n_col), jnp.float32),
            jax.ShapeDtypeStruct((1, LANES), jnp.float32),
        ],
        name="mod",
    )(cond, w_mod, b_mod, lq1, lk1, lq2, lk2)


def _in_proj_kernel(rope, n_tiles, x_ref, xp_ref, xn_ref, shift_ref, scale_ref, g_ref, w_ref, cw_ref, *rest):
    if rope:
        cos_ref, sina_ref, sinb_ref, q_ref, k_ref, v_ref, yc_ref = rest
    else:
        q_ref, k_ref, v_ref, yc_ref = rest
    i = pl.program_id(1)
    tm = x_ref.shape[0]
    gain = g_ref[...] * (1.0 + scale_ref[...])
    shift = shift_ref[...]

    def modulate(x):
        return (_rms(x) * gain + shift).astype(jnp.bfloat16)

    h = modulate(x_ref[...])
    h_halo = modulate(jnp.concatenate([xp_ref[...], xn_ref[...]], axis=0))

    def proj(lhs, lo, hi):
        return jnp.dot(lhs, w_ref[:, lo:hi], preferred_element_type=jnp.float32)

    def rot(t):
        return (t * cos_ref[...] + pltpu.roll(t, LANES - 16, axis=1) * sina_ref[...]
                + pltpu.roll(t, 16, axis=1) * sinb_ref[...])

    zq = proj(h, 0, ATTN_W)
    zk = proj(h, ATTN_W, 2 * ATTN_W)
    zv = proj(h, 2 * ATTN_W, 3 * ATTN_W)
    for hd in range(N_HEADS):
        q = zq[:, hd * D_V:(hd + 1) * D_V]
        k = zk[:, hd * D_V:(hd + 1) * D_V]
        v = zv[:, hd * D_V:(hd + 1) * D_V]
        if rope:
            q = rot(q)
            k = rot(k)
        q_ref[hd] = (q * QK_SCALE).astype(q_ref.dtype)
        k_ref[hd] = k.astype(k_ref.dtype)
        v_ref[hd] = v.astype(v_ref.dtype)
    c0 = 3 * ATTN_W
    gb = proj(h, c0, c0 + CONV_W)
    zc = proj(jnp.concatenate([h, h_halo], axis=0), c0 + CONV_W, c0 + 3 * CONV_W)
    cu_all = zc[:, 0:CONV_W] * zc[:, CONV_W:2 * CONV_W]
    cu = cu_all[0:tm]
    prev_row = jnp.where(i > 0, cu_all[tm + 7:tm + 8], 0.0)
    next_row = jnp.where(i < n_tiles - 1, cu_all[tm + 8:tm + 9], 0.0)
    row = lax.broadcasted_iota(jnp.int32, cu.shape, 0)
    prev = jnp.where(row == 0, prev_row, pltpu.roll(cu, 1, axis=0))
    nxt = jnp.where(row == tm - 1, next_row, pltpu.roll(cu, tm - 1, axis=0))
    conv = cw_ref[0:1, :] * prev + cw_ref[1:2, :] * cu + cw_ref[2:3, :] * nxt
    yc_ref[...] = (gb * conv).astype(yc_ref.dtype)


def _rope_tables(seq):
    n_rows = seq // GRID_W
    row = np.repeat(np.arange(n_rows), GRID_W).astype(np.float64)
    col = np.tile(np.arange(GRID_W), n_rows).astype(np.float64)
    nf = D_QK // 4
    inv = ROPE_BASE ** (-np.arange(nf, dtype=np.float64) / nf)
    ar = row[:, None] * inv
    ac = col[:, None] * inv
    ang = np.concatenate([ar, ar, ac, ac], axis=-1)
    ang = np.concatenate([ang, ang], axis=-1)
    first_half = (np.arange(LANES) % 32) < 16
    cos = np.cos(ang)
    sin = np.sin(ang)
    sina = np.where(first_half, -sin, 0.0)
    sinb = np.where(first_half, 0.0, sin)
    return tuple(jnp.asarray(t, dtype=jnp.float32) for t in (cos, sina, sinb))


def _in_proj_call(x, mod, mod_per_batch, g_pre1, w_in, conv_w, rope, kv_dtype):
    bsz, seq, _ = x.shape
    tm = min(TOKEN_TILE, seq)
    n_tiles = seq // tm
    halo = tm // F32_SUBLANES
    n_halo = seq // F32_SUBLANES

    def mod_spec(chunk):
        if mod_per_batch:
            return pl.BlockSpec((None, 1, D_MODEL), lambda b, i: (b, 0, chunk))
        return pl.BlockSpec((None, 1, D_MODEL), lambda b, i: (0, 0, chunk))

    in_specs = [
        pl.BlockSpec((None, tm, D_MODEL), lambda b, i: (b, i, 0)),
        pl.BlockSpec((None, F32_SUBLANES, D_MODEL), lambda b, i: (b, jnp.maximum(i * halo - 1, 0), 0)),
        pl.BlockSpec((None, F32_SUBLANES, D_MODEL), lambda b, i: (b, jnp.minimum((i + 1) * halo, n_halo - 1), 0)),
        mod_spec(0), mod_spec(1),
        pl.BlockSpec((1, D_MODEL), lambda b, i: (0, 0)),
        pl.BlockSpec((D_MODEL, IN_W), lambda b, i: (0, 0)),
        pl.BlockSpec((3, CONV_W), lambda b, i: (0, 0)),
    ]
    args = [x, x, x, mod, mod, g_pre1, w_in, conv_w]
    if rope:
        in_specs += [pl.BlockSpec((tm, LANES), lambda b, i: (i, 0))] * 3
        args += list(_rope_tables(seq))
    head_spec = pl.BlockSpec((None, N_HEADS, tm, D_V), lambda b, i: (b, 0, i, 0))
    return pl.pallas_call(
        functools.partial(_in_proj_kernel, rope, n_tiles),
        grid=(bsz, n_tiles),
        in_specs=in_specs,
        out_specs=[head_spec, head_spec, head_spec, pl.BlockSpec((None, tm, CONV_W), lambda b, i: (b, i, 0))],
        out_shape=[
            jax.ShapeDtypeStruct((bsz, N_HEADS, seq, D_V), jnp.bfloat16),
            jax.ShapeDtypeStruct((bsz, N_HEADS, seq, D_V), kv_dtype),
            jax.ShapeDtypeStruct((bsz, N_HEADS, seq, D_V), kv_dtype),
            jax.ShapeDtypeStruct((bsz, seq, CONV_W), jnp.bfloat16),
        ],
        name="in_proj_rope" if rope else "in_proj",
    )(*args)


def _attn_kernel(n_kv, lam_ref, g_ref, q_ref, *rest):
    kv_refs = rest[:2 * n_kv]
    o_ref = rest[2 * n_kv]
    heads, tq, _ = q_ref.shape
    lane = lax.broadcasted_iota(jnp.int32, (tq, 2 * D_QK), 1)
    for hh in range(heads):
        q = q_ref[hh]
        zero = jnp.zeros_like(q)
        halves = (jnp.where(lane < D_QK, q, zero), jnp.where(lane >= D_QK, q, zero))
        acc = [jnp.zeros((tq, 2 * D_V), jnp.float32) for _ in range(2)]
        m = [jnp.full((tq, 1), -1e30, jnp.float32) for _ in range(2)]
        for j in range(n_kv):
            k_ref, v_ref = kv_refs[2 * j], kv_refs[2 * j + 1]
            n_keys = k_ref.shape[1]
            ck = min(KEY_CHUNK, n_keys)
            ones = jnp.ones((ck, D_V), jnp.bfloat16)
            for c in range(n_keys // ck):
                k = k_ref[hh, c * ck:(c + 1) * ck, :].astype(jnp.bfloat16)
                v1 = jnp.concatenate([v_ref[hh, c * ck:(c + 1) * ck, :].astype(jnp.bfloat16), ones], axis=1)
                for x, qh in enumerate(halves):
                    s = lax.dot_general(qh, k, NT_DIMS, preferred_element_type=jnp.float32)
                    m_new = jnp.maximum(m[x], jnp.max(s, axis=-1, keepdims=True))
                    p = jnp.exp2(s - m_new).astype(jnp.bfloat16)
                    acc[x] = (jnp.exp2(m[x] - m_new) * acc[x]
                              + jnp.dot(p, v1, preferred_element_type=jnp.float32))
                    m[x] = m_new
        o = (acc[0][:, 0:D_V] / acc[0][:, D_V:2 * D_V]
             - lam_ref[0:1, 0:1] * (acc[1][:, 0:D_V] / acc[1][:, D_V:2 * D_V]))
        o_ref[:, hh * D_V:(hh + 1) * D_V] = (_rms(o) * (g_ref[...] * (1.0 - LAM_INIT))).astype(o_ref.dtype)


def _attn_call(lam, subln_g, q, kvs, heads_per_step):
    bsz, _, seq, _ = q.shape
    tq = min(Q_TILE, seq)
    hp = heads_per_step
    in_specs = [
        pl.BlockSpec((1, LANES), lambda b, h, i: (0, 0)),
        pl.BlockSpec((1, D_V), lambda b, h, i: (0, 0)),
        pl.BlockSpec((None, hp, tq, D_V), lambda b, h, i: (b, h, i, 0)),
    ]
    args = [lam, subln_g, q]
    for k, v, spec_fn in kvs:
        in_specs += [spec_fn(hp), spec_fn(hp)]
        args += [k, v]
    return pl.pallas_call(
        functools.partial(_attn_kernel, len(kvs)),
        grid=(bsz, N_HEADS // hp, seq // tq),
        in_specs=in_specs,
        out_specs=pl.BlockSpec((None, tq, hp * D_V), lambda b, h, i: (b, i, h)),
        out_shape=jax.ShapeDtypeStruct((bsz, seq, ATTN_W), jnp.bfloat16),
        name="attn%d" % len(kvs),
    )(*args)


def _route(logits_t):
    lg = [logits_t[g:g + 1, :] for g in range(N_GROUPS)]
    mg = functools.reduce(jnp.maximum, lg)
    p_sel = 1.0 / functools.reduce(jnp.add, [jnp.exp(t - mg) for t in lg])
    g_sel = jnp.full(mg.shape, N_GROUPS - 1, jnp.int32)
    for g in range(N_GROUPS - 2, -1, -1):
        g_sel = jnp.where(lg[g] == mg, g, g_sel)
    le = []
    for j in range(EXP_PER_GROUP):
        t = jnp.zeros_like(mg)
        for g in range(N_GROUPS):
            r = 8 + g * EXP_PER_GROUP + j
            t = jnp.where(g_sel == g, logits_t[r:r + 1, :], t)
        le.append(t)
    m1 = functools.reduce(jnp.maximum, le)
    i1 = jnp.full(mg.shape, EXP_PER_GROUP - 1, jnp.int32)
    for j in range(EXP_PER_GROUP - 2, -1, -1):
        i1 = jnp.where(le[j] == m1, j, i1)
    neg = jnp.float32(-jnp.inf)
    rest = [jnp.where(i1 == j, neg, le[j]) for j in range(EXP_PER_GROUP)]
    m2 = functools.reduce(jnp.maximum, rest)
    i2 = jnp.full(mg.shape, EXP_PER_GROUP - 1, jnp.int32)
    for j in range(EXP_PER_GROUP - 2, -1, -1):
        i2 = jnp.where(rest[j] == m2, j, i2)
    e2 = jnp.exp(m2 - m1)
    w1 = p_sel / (1.0 + e2)
    w2 = p_sel * e2 / (1.0 + e2)
    base = g_sel * EXP_PER_GROUP
    return (base + i1).astype(jnp.float32), (base + i2).astype(jnp.float32), w1, w2


def _out_proj_kernel(o_ref, yc_ref, wo_ref, x_ref, gate1_ref, shift2_ref, scale2_ref, gpost_ref, gpre_ref,
                     wr_ref, *rest):
    x1_ref, h2_ref, route_ref = rest[-3:]
    tm = o_ref.shape[0]
    out = (jnp.dot(o_ref[...], wo_ref[0:ATTN_W, :], preferred_element_type=jnp.float32)
           + jnp.dot(yc_ref[...], wo_ref[ATTN_W:D_MODEL, :], preferred_element_type=jnp.float32))
    x1 = x_ref[...] + _rms(out) * (gate1_ref[...] * gpost_ref[...])
    x1_ref[...] = x1
    h2 = (_rms(x1) * (gpre_ref[...] * (1.0 + scale2_ref[...])) + shift2_ref[...]).astype(jnp.bfloat16)
    h2_ref[...] = h2
    logits = jnp.dot(h2, wr_ref[...], preferred_element_type=jnp.float32)
    e1, e2, w1, w2 = _route(logits.T)
    route_ref[...] = jnp.concatenate([e1, e2, w1, w2, jnp.zeros((4, tm), jnp.float32)], axis=0)


def _out_proj_call(o, yc, w_o, x, mod, mod_per_batch, g_post1, g_pre2, w_router,
                   total_tokens, token_offset, carried):
    bsz, seq, _ = x.shape
    tm = min(TOKEN_TILE, seq)
    n_tiles = seq // tm
    tile0 = token_offset // tm
    per_route = DISPATCH_TILE // tm

    def mod_spec(chunk):
        if mod_per_batch:
            return pl.BlockSpec((None, 1, D_MODEL), lambda b, i: (b, 0, chunk))
        return pl.BlockSpec((None, 1, D_MODEL), lambda b, i: (0, 0, chunk))

    def vec_spec():
        return pl.BlockSpec((1, D_MODEL), lambda b, i: (0, 0))

    tok = lambda w: pl.BlockSpec((None, tm, w), lambda b, i: (b, i, 0))
    flat = lambda w: pl.BlockSpec((tm, w), lambda b, i: (tile0 + b * n_tiles + i, 0))
    route_spec = pl.BlockSpec(
        (None, 8, tm),
        lambda b, i: ((tile0 + b * n_tiles + i) // per_route, 0, (tile0 + b * n_tiles + i) % per_route))
    in_specs = [
        tok(ATTN_W), tok(CONV_W),
        pl.BlockSpec((D_MODEL, D_MODEL), lambda b, i: (0, 0)),
        tok(D_MODEL),
        mod_spec(2), mod_spec(3), mod_spec(4),
        vec_spec(), vec_spec(),
        pl.BlockSpec((D_MODEL, LANES), lambda b, i: (0, 0)),
    ]
    args = [o, yc, w_o, x, mod, mod, mod, g_post1, g_pre2, w_router]
    aliases = {}
    if carried is not None:
        for j, arr in enumerate(carried):
            aliases[len(args)] = j
            in_specs.append(pl.BlockSpec(memory_space=pl.ANY))
            args.append(arr)
    return pl.pallas_call(
        _out_proj_kernel,
        grid=(bsz, n_tiles),
        in_specs=in_specs,
        out_specs=[flat(D_MODEL), flat(D_MODEL), route_spec],
        out_shape=[
            jax.ShapeDtypeStruct((total_tokens, D_MODEL), jnp.float32),
            jax.ShapeDtypeStruct((total_tokens, D_MODEL), jnp.bfloat16),
            jax.ShapeDtypeStruct((total_tokens // DISPATCH_TILE, 8, DISPATCH_TILE), jnp.float32),
        ],
        input_output_aliases=aliases,
        name="out_proj",
    )(*args)


def _plan_kernel(route_ref, slots_ref, slots_t_ref, counts_ref):
    r = route_ref[...]
    t = r.shape[1]
    e1 = r[0:1, :].astype(jnp.int32)
    e2 = r[1:2, :].astype(jnp.int32)
    eid = lax.broadcasted_iota(jnp.int32, (N_EXPERTS, t), 0)
    hot1 = jnp.where(eid == e1, 1.0, 0.0)
    hot2 = jnp.where(eid == e2, 1.0, 0.0)
    hot = jnp.concatenate([hot1, hot2], axis=0).astype(jnp.bfloat16)
    before = (lax.broadcasted_iota(jnp.int32, (t, t), 0) < lax.broadcasted_iota(jnp.int32, (t, t), 1))
    before = jnp.where(before, 1.0, 0.0).astype(jnp.bfloat16)
    rank = jnp.dot(hot, before, preferred_element_type=jnp.float32)
    n1 = jnp.sum(hot1, axis=1, keepdims=True)
    n2 = jnp.sum(hot2, axis=1, keepdims=True)
    ones = jnp.ones((8, t), jnp.bfloat16)
    cnt_row = lax.dot_general(ones, (hot1 + hot2).astype(jnp.bfloat16), NT_DIMS,
                              preferred_element_type=jnp.float32)[0:1, :]
    pad_row = jnp.floor((cnt_row + (CHUNK - 1)) * (1.0 / CHUNK)) * CHUNK
    lower = (lax.broadcasted_iota(jnp.int32, (N_EXPERTS, N_EXPERTS), 1)
             < lax.broadcasted_iota(jnp.int32, (N_EXPERTS, N_EXPERTS), 0))
    base = jnp.sum(jnp.where(lower, pad_row, 0.0), axis=1, keepdims=True)
    slot1 = jnp.sum(hot1 * (base + rank[:N_EXPERTS]), axis=0, keepdims=True)
    slot2 = jnp.sum(hot2 * (base + n1 + rank[N_EXPERTS:]), axis=0, keepdims=True)
    slots_ref[...] = jnp.concatenate([slot1, slot2, r[2:4, :], jnp.zeros((4, t), jnp.float32)], axis=0)
    wide = jnp.concatenate([slot1, slot2, jnp.zeros((LANES - 2, t), jnp.float32)], axis=0)
    slots_t_ref[...] = wide.T
    counts_ref[...] = jnp.broadcast_to(n1 + n2, counts_ref.shape)


def _plan_call(route):
    n_tiles, _, t = route.shape
    return pl.pallas_call(
        _plan_kernel,
        grid=(n_tiles,),
        in_specs=[pl.BlockSpec((None, 8, t), lambda i: (i, 0, 0))],
        out_specs=[
            pl.BlockSpec((None, 8, t), lambda i: (i, 0, 0)),
            pl.BlockSpec((None, t, LANES), lambda i: (i, 0, 0)),
            pl.BlockSpec((None, N_EXPERTS, LANES), lambda i: (i, 0, 0)),
        ],
        out_shape=[
            jax.ShapeDtypeStruct((n_tiles, 8, t), jnp.float32),
            jax.ShapeDtypeStruct((n_tiles, t, LANES), jnp.float32),
            jax.ShapeDtypeStruct((n_tiles, N_EXPERTS, LANES), jnp.float32),
        ],
        name="moe_plan",
    )(route)


def _layout_tables(counts, n_blocks):
    cnt = counts[:, :, 0].astype(jnp.int32)
    n_tiles = cnt.shape[0]
    nch = (cnt + (CHUNK - 1)) // CHUNK
    csum = jnp.cumsum(nch, axis=0)
    rows = CHUNK * csum[-1]
    blocks = (rows + (EXPERT_BLOCK - 1)) // EXPERT_BLOCK
    blk_end = jnp.cumsum(blocks)
    blk_start = blk_end - blocks
    off = blk_start[None, :] * EXPERT_BLOCK + CHUNK * (csum - nch)
    first = jnp.cumsum(nch, axis=1) - nch
    used = jnp.sum(nch, axis=1)
    c = jnp.arange(N_CHUNKS, dtype=jnp.int32)
    e_idx = jnp.sum((c[None, :, None] >= first[:, None, :]).astype(jnp.int32), axis=2) - 1
    hot = e_idx[:, :, None] == jnp.arange(N_EXPERTS, dtype=jnp.int32)[None, None, :]
    pick = lambda a: jnp.sum(jnp.where(hot, a[:, None, :], 0), axis=2)
    row = pick(off) + CHUNK * (c[None, :] - pick(first))
    valid = c[None, :] < used[:, None]
    spare = (n_blocks * EXPERT_BLOCK + (jnp.arange(n_tiles, dtype=jnp.int32) % 2)[:, None] * TILE_SLOTS
             + CHUNK * c[None, :])
    scatter_rows = jnp.where(valid, row, spare)
    gather_rows = jnp.where(valid, row, 0)
    n_used = blk_end[-1]
    bc = jnp.minimum(jnp.arange(n_blocks, dtype=jnp.int32), n_used - 1)
    block_expert = jnp.sum((bc[:, None] >= blk_end[None, :]).astype(jnp.int32), axis=1)
    tail_start = blk_start * EXPERT_BLOCK + rows
    tail_chunks = (blocks * EXPERT_BLOCK - rows) // CHUNK
    flat = lambda a: a.reshape(-1).astype(jnp.int32)
    return (flat(scatter_rows), flat(gather_rows), flat(tail_start), flat(tail_chunks),
            flat(block_expert), flat(n_used))


def _chunk_copies(tile, rows_ref, make_copy, action):
    for c in range(N_CHUNKS):
        action(make_copy(c * CHUNK, pl.multiple_of(rows_ref[tile * N_CHUNKS + c], CHUNK)))


def _wait_chunk_copies(make_copy):
    for _ in range(N_CHUNKS):
        make_copy(0, 0).wait()


def _dispatch_kernel(rows_ref, tail_start_ref, tail_chunks_ref, h_ref, slots_ref, xs_ref, buf, zbuf, sem, zsem):
    i = pl.program_id(0)
    n = pl.num_programs(0)
    slot = i % 2
    s = slots_ref[...]
    t = s.shape[1]
    s1 = s[0:1, :].astype(jnp.int32)
    s2 = s[1:2, :].astype(jnp.int32)
    rid = lax.broadcasted_iota(jnp.int32, (TILE_SLOTS, t), 0)
    m1 = rid == s1
    m2 = rid == s2
    perm = jnp.where(m1, 1.0, jnp.where(m2, 1.0, 0.0)).astype(jnp.bfloat16)
    rows = jnp.dot(perm, h_ref[...], preferred_element_type=jnp.float32)
    w = jnp.sum(jnp.where(m1, s[2:3, :], 0.0) + jnp.where(m2, s[3:4, :], 0.0), axis=1, keepdims=True)
    hi = w.astype(jnp.bfloat16).astype(jnp.float32)
    lane = lax.broadcasted_iota(jnp.int32, (TILE_SLOTS, LANES), 1)
    aux = jnp.where(lane == 0, hi, jnp.where(lane == 1, w - hi, 0.0))
    buf[slot, :, 0:D_MODEL] = rows.astype(jnp.bfloat16)
    buf[slot, :, D_MODEL:XS_W] = aux.astype(jnp.bfloat16)

    def copies(sl):
        def make(src_row, dst_row):
            return pltpu.make_async_copy(buf.at[sl, pl.ds(src_row, CHUNK), :],
                                         xs_ref.at[pl.ds(dst_row, CHUNK), :], sem.at[sl])
        return make

    _chunk_copies(i, rows_ref, copies(slot), lambda c: c.start())

    @pl.when(i > 0)
    def _():
        _wait_chunk_copies(copies(1 - slot))

    @pl.when(i == n - 1)
    def _():
        zbuf[...] = jnp.zeros_like(zbuf)

        def tails(action):
            for e in range(N_EXPERTS):
                start = tail_start_ref[e]

                def body(m, carry, start=start):
                    action(pltpu.make_async_copy(
                        zbuf, xs_ref.at[pl.ds(pl.multiple_of(start + m * CHUNK, CHUNK), CHUNK), :], zsem))
                    return carry

                lax.fori_loop(0, tail_chunks_ref[e], body, 0)

        tails(lambda c: c.start())
        tails(lambda c: c.wait())
        _wait_chunk_copies(copies(slot))


def _dispatch_call(tables, h2, slots, n_blocks):
    scatter_rows, tail_start, tail_chunks = tables
    tokens = h2.shape[0]
    t = DISPATCH_TILE
    return pl.pallas_call(
        _dispatch_kernel,
        grid_spec=pltpu.PrefetchScalarGridSpec(
            num_scalar_prefetch=3,
            grid=(tokens // t,),
            in_specs=[
                pl.BlockSpec((t, D_MODEL), lambda i, *_: (i, 0)),
                pl.BlockSpec((None, 8, t), lambda i, *_: (i, 0, 0)),
            ],
            out_specs=pl.BlockSpec(memory_space=pl.ANY),
            scratch_shapes=[
                pltpu.VMEM((2, TILE_SLOTS, XS_W), jnp.bfloat16),
                pltpu.VMEM((CHUNK, XS_W), jnp.bfloat16),
                pltpu.SemaphoreType.DMA((2,)),
                pltpu.SemaphoreType.DMA(()),
            ],
        ),
        out_shape=jax.ShapeDtypeStruct((n_blocks * EXPERT_BLOCK + 2 * TILE_SLOTS, XS_W), jnp.bfloat16),
        name="moe_dispatch",
    )(scatter_rows, tail_start, tail_chunks, h2, slots)


def _expert_kernel(bexp_ref, nused_ref, xs_ref, wg_ref, wu_ref, wd_ref, ys_ref, wgu_b, wd_b):
    b = pl.program_id(0)
    e = bexp_ref[b]
    prev = bexp_ref[jnp.maximum(b - 1, 0)]

    @pl.when((b == 0) | (e != prev))
    def _():
        wgu_b[:, 0:D_EXPERT] = wg_ref[...].astype(jnp.bfloat16)
        wgu_b[:, D_EXPERT:2 * D_EXPERT] = wu_ref[...].astype(jnp.bfloat16)
        wd_b[...] = wd_ref[...].astype(jnp.bfloat16)

    @pl.when(b < nused_ref[0])
    def _():
        x = xs_ref[:, 0:D_MODEL]
        gu = jnp.dot(x, wgu_b[...], preferred_element_type=jnp.float32)
        act = (_silu(gu[:, 0:D_EXPERT]) * gu[:, D_EXPERT:2 * D_EXPERT]).astype(jnp.bfloat16)
        y = jnp.dot(act, wd_b[...], preferred_element_type=jnp.float32)
        w = (xs_ref[:, D_MODEL:D_MODEL + 1].astype(jnp.float32)
             + xs_ref[:, D_MODEL + 1:D_MODEL + 2].astype(jnp.float32))
        ys_ref[...] = (w * y).astype(jnp.bfloat16)


def _expert_call(block_expert, n_used, xs, w_eg, w_eu, w_ed, n_blocks):
    row_blk = lambda w: pl.BlockSpec((EXPERT_BLOCK, w), lambda b, be, nu: (jnp.minimum(b, nu[0] - 1), 0))
    return pl.pallas_call(
        _expert_kernel,
        grid_spec=pltpu.PrefetchScalarGridSpec(
            num_scalar_prefetch=2,
            grid=(n_blocks,),
            in_specs=[
                row_blk(XS_W),
                pl.BlockSpec((None, D_MODEL, D_EXPERT), lambda b, be, nu: (be[b], 0, 0)),
                pl.BlockSpec((None, D_MODEL, D_EXPERT), lambda b, be, nu: (be[b], 0, 0)),
                pl.BlockSpec((None, D_EXPERT, D_MODEL), lambda b, be, nu: (be[b], 0, 0)),
            ],
            out_specs=row_blk(D_MODEL),
            scratch_shapes=[
                pltpu.VMEM((D_MODEL, 2 * D_EXPERT), jnp.bfloat16),
                pltpu.VMEM((D_EXPERT, D_MODEL), jnp.bfloat16),
            ],
        ),
        out_shape=jax.ShapeDtypeStruct((n_blocks * EXPERT_BLOCK, D_MODEL), jnp.bfloat16),
        name="moe_experts",
    )(block_expert, n_used, xs, w_eg, w_eu, w_ed)


def _combine_kernel(tile0, rows_ref, ys_ref, slots_t_ref, x1_ref, gate2_ref, gpost_ref, out_ref, buf, sem):
    j = pl.program_id(0)
    n = pl.num_programs(0)
    slot = j % 2
    tile = tile0 + j

    def copies(sl):
        def make(dst_row, src_row):
            return pltpu.make_async_copy(ys_ref.at[pl.ds(src_row, CHUNK), :],
                                         buf.at[sl, pl.ds(dst_row, CHUNK), :], sem.at[sl])
        return make

    @pl.when(j == 0)
    def _():
        _chunk_copies(tile, rows_ref, copies(slot), lambda c: c.start())

    @pl.when(j + 1 < n)
    def _():
        _chunk_copies(tile + 1, rows_ref, copies(1 - slot), lambda c: c.start())

    _wait_chunk_copies(copies(slot))
    st = slots_t_ref[...]
    s1 = st[:, 0:1].astype(jnp.int32)
    s2 = st[:, 1:2].astype(jnp.int32)
    lane = lax.broadcasted_iota(jnp.int32, (st.shape[0], TILE_SLOTS), 1)
    unperm = jnp.where(lane == s1, 1.0, jnp.where(lane == s2, 1.0, 0.0)).astype(jnp.bfloat16)
    y = jnp.dot(unperm, buf[slot], preferred_element_type=jnp.float32)
    out_ref[...] = x1_ref[...] + gate2_ref[...] * (_rms(y) * gpost_ref[...])


def _combine_call(gather_rows, ys, slots_t, x1, mod, mod_per_batch, g_post2, token_offset, bsz, seq):
    t = DISPATCH_TILE
    tile0 = token_offset // t
    tokens = bsz * seq
    if mod_per_batch:
        per = seq // t
        gate2_spec = pl.BlockSpec((None, 1, D_MODEL), lambda j, *_: (j // per, 0, 5))
    else:
        gate2_spec = pl.BlockSpec((None, 1, D_MODEL), lambda j, *_: (0, 0, 5))
    y = pl.pallas_call(
        functools.partial(_combine_kernel, tile0),
        grid_spec=pltpu.PrefetchScalarGridSpec(
            num_scalar_prefetch=1,
            grid=(tokens // t,),
            in_specs=[
                pl.BlockSpec(memory_space=pl.ANY),
                pl.BlockSpec((None, t, LANES), lambda j, *_: (tile0 + j, 0, 0)),
                pl.BlockSpec((t, D_MODEL), lambda j, *_: (tile0 + j, 0)),
                gate2_spec,
                pl.BlockSpec((1, D_MODEL), lambda j, *_: (0, 0)),
            ],
            out_specs=pl.BlockSpec((t, D_MODEL), lambda j, *_: (j, 0)),
            scratch_shapes=[
                pltpu.VMEM((2, TILE_SLOTS, D_MODEL), jnp.bfloat16),
                pltpu.SemaphoreType.DMA((2,)),
            ],
        ),
        out_shape=jax.ShapeDtypeStruct((tokens, D_MODEL), jnp.float32),
        name="moe_combine",
    )(gather_rows, ys, slots_t, x1, mod, g_post2)
    return y.reshape(bsz, seq, D_MODEL)


def kernel(x_prompt, x_sample, cache_k, cache_v, c, c_ctx, w_mod, b_mod, g_pre1, g_post1, g_pre2, g_post2,
           w_in, conv_w, lambda_q1, lambda_k1, lambda_q2, lambda_k2, subln_g, w_o, w_router_group,
           w_router_expert, w_exp_gate, w_exp_up, w_exp_down):
    n_lat = c.shape[0]
    cond = jnp.concatenate(
        [c, c_ctx[None, :], jnp.zeros((MOD_ROWS - n_lat - 1, D_MODEL), jnp.float32)], axis=0)
    mod, lam = _mod_call(cond, w_mod[0], b_mod, lambda_q1, lambda_k1, lambda_q2, lambda_k2)
    mod = mod.reshape(MOD_ROWS, 1, 6 * D_MODEL)
    mod_lat, mod_ctx = mod[:n_lat], mod[n_lat:n_lat + 1]

    w_in_b = w_in[0].astype(jnp.bfloat16)
    w_o_b = w_o[0].astype(jnp.bfloat16)
    w_router = jnp.concatenate(
        [w_router_group[0], jnp.zeros((D_MODEL, 8 - N_GROUPS), jnp.float32), w_router_expert[0],
         jnp.zeros((D_MODEL, LANES - 8 - N_EXPERTS), jnp.float32)], axis=1).astype(jnp.bfloat16)

    n_prompt = x_prompt.shape[0] * x_prompt.shape[1]
    n_sample = x_sample.shape[0] * x_sample.shape[1]
    total = n_prompt + n_sample

    def mixer(x, mod_x, per_batch, rope, ctx_kv, token_offset, carried):
        kv_dtype = jnp.bfloat16 if rope else jnp.float32
        q, k, v, yc = _in_proj_call(x, mod_x, per_batch, g_pre1, w_in_b, conv_w[0], rope, kv_dtype)
        kvs = []
        if ctx_kv is not None:
            ck, cv = ctx_kv
            n_ctx = ck.shape[3]
            kvs.append((ck, cv, lambda hp: pl.BlockSpec((None, None, hp, n_ctx, D_V),
                                                        lambda b, h, i: (b, 0, h, 0, 0))))
        seq = x.shape[1]
        kvs.append((k, v, lambda hp: pl.BlockSpec((None, hp, seq, D_V), lambda b, h, i: (b, h, 0, 0))))
        o = _attn_call(lam, subln_g, q, kvs, N_HEADS if seq <= Q_TILE // 2 else 1)
        shared = _out_proj_call(o, yc, w_o_b, x, mod_x, per_batch, g_post1, g_pre2, w_router,
                                total, token_offset, carried)
        return shared, k, v

    shared, kp, vp = mixer(x_prompt, mod_ctx, False, False, None, 0, None)
    (x1, h2, route), _, _ = mixer(x_sample, mod_lat, True, True, (cache_k, cache_v), n_prompt, shared)

    n_tiles = total // DISPATCH_TILE
    max_rows = 2 * total + n_tiles * N_EXPERTS * (CHUNK - 1) + N_EXPERTS * (EXPERT_BLOCK - CHUNK)
    n_blocks = -(-max_rows // EXPERT_BLOCK)
    slots, slots_t, counts = _plan_call(route)
    scatter_rows, gather_rows, tail_start, tail_chunks, block_expert, n_used = _layout_tables(counts, n_blocks)
    xs = _dispatch_call((scatter_rows, tail_start, tail_chunks), h2, slots, n_blocks)
    ys = _expert_call(block_expert, n_used, xs, w_exp_gate[0], w_exp_up[0], w_exp_down[0], n_blocks)
    yp = _combine_call(gather_rows, ys, slots_t, x1, mod_ctx, False, g_post2, 0,
                       x_prompt.shape[0], x_prompt.shape[1])
    ysamp = _combine_call(gather_rows, ys, slots_t, x1, mod_lat, True, g_post2, n_prompt,
                          x_sample.shape[0], x_sample.shape[1])
    return yp, ysamp, kp[:, None], vp[:, None]
```

```python
import functools
import math

import numpy as np
import jax
import jax.numpy as jnp
from jax import lax
from jax.experimental import pallas as pl
from jax.experimental.pallas import tpu as pltpu

D_MODEL = 1024
GRID_W = 64
N_HEADS = 4
D_QK = 64
D_V = 128
ATTN_W = N_HEADS * D_V
CONV_W = D_MODEL - ATTN_W
IN_W = 3 * ATTN_W + 3 * CONV_W
N_GROUPS = 4
EXP_PER_GROUP = 4
N_EXPERTS = N_GROUPS * EXP_PER_GROUP
D_EXPERT = 512
ROPE_BASE = 10000.0
EPS = 1e-6
LAM_INIT = 0.8 - 0.6 * math.exp(-0.3 * 0)

LANES = 128
F32_SUBLANES = 8
BF16_SUBLANES = 16
MOD_ROWS = 16
TOKEN_TILE = 512
OUT_SUB_ROWS = 256
Q_TILE = 2048
KEY_CHUNK = 256
MERGE_HEADS_MAX_SEQ = 512
DISPATCH_TILE = 512
CHUNK = BF16_SUBLANES
TILE_SLOTS = 1280
N_CHUNKS = TILE_SLOTS // CHUNK
EXPERT_BLOCK = 512
XS_W = D_MODEL + LANES
QK_SCALE = (1.0 / math.sqrt(D_QK)) * math.log2(math.e)

NT_DIMS = (((1,), (1,)), ((), ()))


def _rms(x):
    return x * lax.rsqrt(jnp.mean(x * x, axis=-1, keepdims=True) + EPS)


def _silu(x):
    return x * (1.0 / (1.0 + jnp.exp(-x)))


def _mod_kernel(cond_ref, w_ref, b_ref, lq1_ref, lk1_ref, lq2_ref, lk2_ref, mod_ref, lam_ref):
    s = _silu(cond_ref[...])
    m = lax.dot_general(s, w_ref[...], (((1,), (0,)), ((), ())),
                        precision=lax.Precision.HIGHEST, preferred_element_type=jnp.float32)
    mod_ref[...] = m + b_ref[...]
    a = jnp.sum(lq1_ref[...] * lk1_ref[...], axis=-1, keepdims=True)
    b = jnp.sum(lq2_ref[...] * lk2_ref[...], axis=-1, keepdims=True)
    lam_ref[...] = jnp.broadcast_to(jnp.exp(a) - jnp.exp(b) + LAM_INIT, lam_ref.shape)


def _mod_call(cond, w_mod, b_mod, lq1, lk1, lq2, lk2):
    n_col = 6 * D_MODEL
    col_tile = 1536
    small = pl.BlockSpec((1, D_QK), lambda j: (0, 0))
    return pl.pallas_call(
        _mod_kernel,
        grid=(n_col // col_tile,),
        in_specs=[
            pl.BlockSpec((MOD_ROWS, D_MODEL), lambda j: (0, 0)),
            pl.BlockSpec((D_MODEL, col_tile), lambda j: (0, j)),
            pl.BlockSpec((1, col_tile), lambda j: (0, j)),
            small, small, small, small,
        ],
        out_specs=[
            pl.BlockSpec((MOD_ROWS, col_tile), lambda j: (0, j)),
            pl.BlockSpec((1, LANES), lambda j: (0, 0)),
        ],
        out_shape=[
            jax.ShapeDtypeStruct((MOD_ROWS, n_col), jnp.float32),
            jax.ShapeDtypeStruct((1, LANES), jnp.float32),
        ],
        name="mod",
    )(cond, w_mod, b_mod, lq1, lk1, lq2, lk2)


def _in_proj_kernel(rope, n_tiles, x_ref, xp_ref, xn_ref, shift_ref, scale_ref, g_ref, w_ref, cw_ref, *rest):
    if rope:
        cos_ref, sina_ref, sinb_ref, q_ref, k_ref, v_ref, yc_ref = rest
    else:
        q_ref, k_ref, v_ref, yc_ref = rest
    i = pl.program_id(1)
    tm = x_ref.shape[0]
    gain = g_ref[...] * (1.0 + scale_ref[...])
    shift = shift_ref[...]

    def modulate(x):
        return (_rms(x) * gain + shift).astype(jnp.bfloat16)

    h = modulate(x_ref[...])
    h_halo = modulate(jnp.concatenate([xp_ref[...], xn_ref[...]], axis=0))

    def proj(lhs, lo, hi):
        return jnp.dot(lhs, w_ref[:, lo:hi], preferred_element_type=jnp.float32)

    def rot(t):
        return (t * cos_ref[...] + pltpu.roll(t, LANES - 16, axis=1) * sina_ref[...]
                + pltpu.roll(t, 16, axis=1) * sinb_ref[...])

    zq = proj(h, 0, ATTN_W)
    zk = proj(h, ATTN_W, 2 * ATTN_W)
    zv = proj(h, 2 * ATTN_W, 3 * ATTN_W)
    for hd in range(N_HEADS):
        q = zq[:, hd * D_V:(hd + 1) * D_V]
        k = zk[:, hd * D_V:(hd + 1) * D_V]
        v = zv[:, hd * D_V:(hd + 1) * D_V]
        if rope:
            q = rot(q)
            k = rot(k)
        q_ref[hd] = (q * QK_SCALE).astype(q_ref.dtype)
        k_ref[hd] = k.astype(k_ref.dtype)
        v_ref[hd] = v.astype(v_ref.dtype)
    c0 = 3 * ATTN_W
    gb = proj(h, c0, c0 + CONV_W)
    zc = proj(jnp.concatenate([h, h_halo], axis=0), c0 + CONV_W, c0 + 3 * CONV_W)
    cu_all = zc[:, 0:CONV_W] * zc[:, CONV_W:2 * CONV_W]
    cu = cu_all[0:tm]
    prev_row = jnp.where(i > 0, cu_all[tm + 7:tm + 8], 0.0)
    next_row = jnp.where(i < n_tiles - 1, cu_all[tm + 8:tm + 9], 0.0)
    row = lax.broadcasted_iota(jnp.int32, cu.shape, 0)
    prev = jnp.where(row == 0, prev_row, pltpu.roll(cu, 1, axis=0))
    nxt = jnp.where(row == tm - 1, next_row, pltpu.roll(cu, tm - 1, axis=0))
    conv = cw_ref[0:1, :] * prev + cw_ref[1:2, :] * cu + cw_ref[2:3, :] * nxt
    yc_ref[...] = (gb * conv).astype(yc_ref.dtype)


def _rope_tables(seq):
    n_rows = seq // GRID_W
    row = np.repeat(np.arange(n_rows), GRID_W).astype(np.float64)
    col = np.tile(np.arange(GRID_W), n_rows).astype(np.float64)
    nf = D_QK // 4
    inv = ROPE_BASE ** (-np.arange(nf, dtype=np.float64) / nf)
    ar = row[:, None] * inv
    ac = col[:, None] * inv
    ang = np.concatenate([ar, ar, ac, ac], axis=-1)
    ang = np.concatenate([ang, ang], axis=-1)
    first_half = (np.arange(LANES) % 32) < 16
    cos = np.cos(ang)
    sin = np.sin(ang)
    sina = np.where(first_half, -sin, 0.0)
    sinb = np.where(first_half, 0.0, sin)
    return tuple(jnp.asarray(t, dtype=jnp.float32) for t in (cos, sina, sinb))


def _in_proj_call(x, mod, mod_per_batch, g_pre1, w_in, conv_w, rope, kv_dtype):
    bsz, seq, _ = x.shape
    tm = min(TOKEN_TILE, seq)
    n_tiles = seq // tm
    halo = tm // F32_SUBLANES
    n_halo = seq // F32_SUBLANES

    def mod_spec(chunk):
        if mod_per_batch:
            return pl.BlockSpec((None, 1, D_MODEL), lambda b, i: (b, 0, chunk))
        return pl.BlockSpec((None, 1, D_MODEL), lambda b, i: (0, 0, chunk))

    in_specs = [
        pl.BlockSpec((None, tm, D_MODEL), lambda b, i: (b, i, 0)),
        pl.BlockSpec((None, F32_SUBLANES, D_MODEL), lambda b, i: (b, jnp.maximum(i * halo - 1, 0), 0)),
        pl.BlockSpec((None, F32_SUBLANES, D_MODEL), lambda b, i: (b, jnp.minimum((i + 1) * halo, n_halo - 1), 0)),
        mod_spec(0), mod_spec(1),
        pl.BlockSpec((1, D_MODEL), lambda b, i: (0, 0)),
        pl.BlockSpec((D_MODEL, IN_W), lambda b, i: (0, 0)),
        pl.BlockSpec((3, CONV_W), lambda b, i: (0, 0)),
    ]
    args = [x, x, x, mod, mod, g_pre1, w_in, conv_w]
    if rope:
        in_specs += [pl.BlockSpec((tm, LANES), lambda b, i: (i, 0))] * 3
        args += list(_rope_tables(seq))
    head_spec = pl.BlockSpec((None, N_HEADS, tm, D_V), lambda b, i: (b, 0, i, 0))
    return pl.pallas_call(
        functools.partial(_in_proj_kernel, rope, n_tiles),
        grid=(bsz, n_tiles),
        in_specs=in_specs,
        out_specs=[head_spec, head_spec, head_spec, pl.BlockSpec((None, tm, CONV_W), lambda b, i: (b, i, 0))],
        out_shape=[
            jax.ShapeDtypeStruct((bsz, N_HEADS, seq, D_V), jnp.bfloat16),
            jax.ShapeDtypeStruct((bsz, N_HEADS, seq, D_V), kv_dtype),
            jax.ShapeDtypeStruct((bsz, N_HEADS, seq, D_V), kv_dtype),
            jax.ShapeDtypeStruct((bsz, seq, CONV_W), jnp.bfloat16),
        ],
        name="in_proj_rope" if rope else "in_proj",
    )(*args)


def _attn_kernel(n_kv, lam_ref, g_ref, q_ref, *rest):
    kv_refs = rest[:2 * n_kv]
    o_ref = rest[2 * n_kv]
    heads, tq, _ = q_ref.shape
    lane = lax.broadcasted_iota(jnp.int32, (tq, 2 * D_QK), 1)
    for hh in range(heads):
        q = q_ref[hh]
        zero = jnp.zeros_like(q)
        halves = (jnp.where(lane < D_QK, q, zero), jnp.where(lane >= D_QK, q, zero))
        acc = [jnp.zeros((tq, 2 * D_V), jnp.float32) for _ in range(2)]
        m = [jnp.full((tq, 1), -1e30, jnp.float32) for _ in range(2)]
        for j in range(n_kv):
            k_ref, v_ref = kv_refs[2 * j], kv_refs[2 * j + 1]
            n_keys = k_ref.shape[1]
            ck = min(KEY_CHUNK, n_keys)
            ones = jnp.ones((ck, D_V), jnp.bfloat16)
            for c in range(n_keys // ck):
                k = k_ref[hh, c * ck:(c + 1) * ck, :].astype(jnp.bfloat16)
                v1 = jnp.concatenate([v_ref[hh, c * ck:(c + 1) * ck, :].astype(jnp.bfloat16), ones], axis=1)
                for x, qh in enumerate(halves):
                    s = lax.dot_general(qh, k, NT_DIMS, preferred_element_type=jnp.float32)
                    m_new = jnp.maximum(m[x], jnp.max(s, axis=-1, keepdims=True))
                    p = jnp.exp2(s - m_new).astype(jnp.bfloat16)
                    acc[x] = (jnp.exp2(m[x] - m_new) * acc[x]
                              + jnp.dot(p, v1, preferred_element_type=jnp.float32))
                    m[x] = m_new
        o = (acc[0][:, 0:D_V] / acc[0][:, D_V:2 * D_V]
             - lam_ref[0:1, 0:1] * (acc[1][:, 0:D_V] / acc[1][:, D_V:2 * D_V]))
        o_ref[:, hh * D_V:(hh + 1) * D_V] = (_rms(o) * (g_ref[...] * (1.0 - LAM_INIT))).astype(o_ref.dtype)


def _attn_call(lam, subln_g, q, kvs, heads_per_step):
    bsz, _, seq, _ = q.shape
    tq = min(Q_TILE, seq)
    hp = heads_per_step
    in_specs = [
        pl.BlockSpec((1, LANES), lambda b, h, i: (0, 0)),
        pl.BlockSpec((1, D_V), lambda b, h, i: (0, 0)),
        pl.BlockSpec((None, hp, tq, D_V), lambda b, h, i: (b, h, i, 0)),
    ]
    args = [lam, subln_g, q]
    for k, v, spec_fn in kvs:
        in_specs += [spec_fn(hp), spec_fn(hp)]
        args += [k, v]
    return pl.pallas_call(
        functools.partial(_attn_kernel, len(kvs)),
        grid=(bsz, N_HEADS // hp, seq // tq),
        in_specs=in_specs,
        out_specs=pl.BlockSpec((None, tq, hp * D_V), lambda b, h, i: (b, i, h)),
        out_shape=jax.ShapeDtypeStruct((bsz, seq, ATTN_W), jnp.bfloat16),
        name="attn%d" % len(kvs),
    )(*args)


def _route(logits_t):
    lg = [logits_t[g:g + 1, :] for g in range(N_GROUPS)]
    mg = functools.reduce(jnp.maximum, lg)
    p_sel = 1.0 / functools.reduce(jnp.add, [jnp.exp(t - mg) for t in lg])
    g_sel = jnp.full(mg.shape, N_GROUPS - 1, jnp.int32)
    for g in range(N_GROUPS - 2, -1, -1):
        g_sel = jnp.where(lg[g] == mg, g, g_sel)
    le = []
    for j in range(EXP_PER_GROUP):
        t = jnp.zeros_like(mg)
        for g in range(N_GROUPS):
            r = 8 + g * EXP_PER_GROUP + j
            t = jnp.where(g_sel == g, logits_t[r:r + 1, :], t)
        le.append(t)
    m1 = functools.reduce(jnp.maximum, le)
    i1 = jnp.full(mg.shape, EXP_PER_GROUP - 1, jnp.int32)
    for j in range(EXP_PER_GROUP - 2, -1, -1):
        i1 = jnp.where(le[j] == m1, j, i1)
    neg = jnp.float32(-jnp.inf)
    rest = [jnp.where(i1 == j, neg, le[j]) for j in range(EXP_PER_GROUP)]
    m2 = functools.reduce(jnp.maximum, rest)
    i2 = jnp.full(mg.shape, EXP_PER_GROUP - 1, jnp.int32)
    for j in range(EXP_PER_GROUP - 2, -1, -1):
        i2 = jnp.where(rest[j] == m2, j, i2)
    e2 = jnp.exp(m2 - m1)
    w1 = p_sel / (1.0 + e2)
    w2 = p_sel * e2 / (1.0 + e2)
    base = g_sel * EXP_PER_GROUP
    return (base + i1).astype(jnp.float32), (base + i2).astype(jnp.float32), w1, w2


def _out_proj_kernel(o_ref, yc_ref, wo_ref, x_ref, gate1_ref, shift2_ref, scale2_ref, gpost_ref, gpre_ref,
                     wr_ref, *rest):
    x1_ref, h2_ref, slots_ref, slots_t_ref, counts_ref = rest[-5:]
    tm = o_ref.shape[0]
    sub = min(OUT_SUB_ROWS, tm)
    gain1 = gate1_ref[...] * gpost_ref[...]
    gain2 = gpre_ref[...] * (1.0 + scale2_ref[...])
    routes = []
    for r in range(tm // sub):
        rows = slice(r * sub, (r + 1) * sub)
        out = (jnp.dot(o_ref[rows, :], wo_ref[0:ATTN_W, :], preferred_element_type=jnp.float32)
               + jnp.dot(yc_ref[rows, :], wo_ref[ATTN_W:D_MODEL, :], preferred_element_type=jnp.float32))
        x1 = x_ref[rows, :] + _rms(out) * gain1
        x1_ref[rows, :] = x1
        h2 = (_rms(x1) * gain2 + shift2_ref[...]).astype(jnp.bfloat16)
        h2_ref[rows, :] = h2
        logits = jnp.dot(h2, wr_ref[...], preferred_element_type=jnp.float32)
        routes.append(_route(logits.T))
    e1, e2, w1, w2 = (jnp.concatenate(parts, axis=1) for parts in zip(*routes))
    _plan(e1, e2, w1, w2, slots_ref, slots_t_ref, counts_ref)


def _out_proj_call(o, yc, w_o, x, mod, mod_per_batch, g_post1, g_pre2, w_router,
                   total_tokens, token_offset, carried):
    bsz, seq, _ = x.shape
    tm = DISPATCH_TILE
    assert seq % tm == 0 and token_offset % tm == 0
    n_tiles = seq // tm
    tile0 = token_offset // tm
    n_all = total_tokens // tm

    def mod_spec(chunk):
        if mod_per_batch:
            return pl.BlockSpec((None, 1, D_MODEL), lambda b, i: (b, 0, chunk))
        return pl.BlockSpec((None, 1, D_MODEL), lambda b, i: (0, 0, chunk))

    def vec_spec():
        return pl.BlockSpec((1, D_MODEL), lambda b, i: (0, 0))

    tok = lambda w: pl.BlockSpec((None, tm, w), lambda b, i: (b, i, 0))
    flat = lambda w: pl.BlockSpec((tm, w), lambda b, i: (tile0 + b * n_tiles + i, 0))
    per_tile = lambda r, c: pl.BlockSpec((None, r, c), lambda b, i: (tile0 + b * n_tiles + i, 0, 0))
    in_specs = [
        tok(ATTN_W), tok(CONV_W),
        pl.BlockSpec((D_MODEL, D_MODEL), lambda b, i: (0, 0)),
        tok(D_MODEL),
        mod_spec(2), mod_spec(3), mod_spec(4),
        vec_spec(), vec_spec(),
        pl.BlockSpec((D_MODEL, LANES), lambda b, i: (0, 0)),
    ]
    args = [o, yc, w_o, x, mod, mod, mod, g_post1, g_pre2, w_router]
    aliases = {}
    if carried is not None:
        for j, arr in enumerate(carried):
            aliases[len(args)] = j
            in_specs.append(pl.BlockSpec(memory_space=pl.ANY))
            args.append(arr)
    return pl.pallas_call(
        _out_proj_kernel,
        grid=(bsz, n_tiles),
        in_specs=in_specs,
        out_specs=[flat(D_MODEL), flat(D_MODEL), per_tile(8, tm), per_tile(tm, LANES), per_tile(N_EXPERTS, LANES)],
        out_shape=[
            jax.ShapeDtypeStruct((total_tokens, D_MODEL), jnp.float32),
            jax.ShapeDtypeStruct((total_tokens, D_MODEL), jnp.bfloat16),
            jax.ShapeDtypeStruct((n_all, 8, tm), jnp.float32),
            jax.ShapeDtypeStruct((n_all, tm, LANES), jnp.float32),
            jax.ShapeDtypeStruct((n_all, N_EXPERTS, LANES), jnp.float32),
        ],
        input_output_aliases=aliases,
        name="out_proj",
    )(*args)


def _plan(e1, e2, w1, w2, slots_ref, slots_t_ref, counts_ref):
    t = e1.shape[1]
    e1 = e1.astype(jnp.int32)
    e2 = e2.astype(jnp.int32)
    eid = lax.broadcasted_iota(jnp.int32, (N_EXPERTS, t), 0)
    hot1 = jnp.where(eid == e1, 1.0, 0.0)
    hot2 = jnp.where(eid == e2, 1.0, 0.0)
    hot = jnp.concatenate([hot1, hot2], axis=0).astype(jnp.bfloat16)
    before = (lax.broadcasted_iota(jnp.int32, (t, t), 0) < lax.broadcasted_iota(jnp.int32, (t, t), 1))
    before = jnp.where(before, 1.0, 0.0).astype(jnp.bfloat16)
    rank = jnp.dot(hot, before, preferred_element_type=jnp.float32)
    n1 = jnp.sum(hot1, axis=1, keepdims=True)
    n2 = jnp.sum(hot2, axis=1, keepdims=True)
    ones = jnp.ones((8, t), jnp.bfloat16)
    cnt_row = lax.dot_general(ones, (hot1 + hot2).astype(jnp.bfloat16), NT_DIMS,
                              preferred_element_type=jnp.float32)[0:1, :]
    pad_row = jnp.floor((cnt_row + (CHUNK - 1)) * (1.0 / CHUNK)) * CHUNK
    lower = (lax.broadcasted_iota(jnp.int32, (N_EXPERTS, N_EXPERTS), 1)
             < lax.broadcasted_iota(jnp.int32, (N_EXPERTS, N_EXPERTS), 0))
    base = jnp.sum(jnp.where(lower, pad_row, 0.0), axis=1, keepdims=True)
    slot1 = jnp.sum(hot1 * (base + rank[:N_EXPERTS]), axis=0, keepdims=True)
    slot2 = jnp.sum(hot2 * (base + n1 + rank[N_EXPERTS:]), axis=0, keepdims=True)
    slots_ref[...] = jnp.concatenate([slot1, slot2, w1, w2, jnp.zeros((4, t), jnp.float32)], axis=0)
    wide = jnp.concatenate([slot1, slot2, jnp.zeros((LANES - 2, t), jnp.float32)], axis=0)
    slots_t_ref[...] = wide.T
    counts_ref[...] = jnp.broadcast_to(n1 + n2, counts_ref.shape)


def _layout_tables(counts, n_blocks):
    cnt = counts[:, :, 0].astype(jnp.int32)
    n_tiles = cnt.shape[0]
    nch = (cnt + (CHUNK - 1)) // CHUNK
    csum = jnp.cumsum(nch, axis=0)
    rows = CHUNK * csum[-1]
    blocks = (rows + (EXPERT_BLOCK - 1)) // EXPERT_BLOCK
    blk_end = jnp.cumsum(blocks)
    blk_start = blk_end - blocks
    off = blk_start[None, :] * EXPERT_BLOCK + CHUNK * (csum - nch)
    first = jnp.cumsum(nch, axis=1) - nch
    used = jnp.sum(nch, axis=1)
    c = jnp.arange(N_CHUNKS, dtype=jnp.int32)
    e_idx = jnp.sum((c[None, :, None] >= first[:, None, :]).astype(jnp.int32), axis=2) - 1
    hot = e_idx[:, :, None] == jnp.arange(N_EXPERTS, dtype=jnp.int32)[None, None, :]
    pick = lambda a: jnp.sum(jnp.where(hot, a[:, None, :], 0), axis=2)
    row = pick(off) + CHUNK * (c[None, :] - pick(first))
    valid = c[None, :] < used[:, None]
    spare = (n_blocks * EXPERT_BLOCK + (jnp.arange(n_tiles, dtype=jnp.int32) % 2)[:, None] * TILE_SLOTS
             + CHUNK * c[None, :])
    scatter_rows = jnp.where(valid, row, spare)
    gather_rows = jnp.where(valid, row, 0)
    n_used = blk_end[-1]
    bc = jnp.minimum(jnp.arange(n_blocks, dtype=jnp.int32), n_used - 1)
    block_expert = jnp.sum((bc[:, None] >= blk_end[None, :]).astype(jnp.int32), axis=1)
    tail_start = blk_start * EXPERT_BLOCK + rows
    tail_chunks = (blocks * EXPERT_BLOCK - rows) // CHUNK
    flat = lambda a: a.reshape(-1).astype(jnp.int32)
    return (flat(scatter_rows), flat(gather_rows), flat(tail_start), flat(tail_chunks),
            flat(block_expert), flat(n_used))


def _chunk_copies(tile, rows_ref, make_copy, action):
    for c in range(N_CHUNKS):
        action(make_copy(c * CHUNK, pl.multiple_of(rows_ref[tile * N_CHUNKS + c], CHUNK)))


def _wait_chunk_copies(make_copy):
    for _ in range(N_CHUNKS):
        make_copy(0, 0).wait()


def _dispatch_kernel(rows_ref, tail_start_ref, tail_chunks_ref, h_ref, slots_ref, xs_ref, buf, zbuf, sem, zsem):
    i = pl.program_id(0)
    n = pl.num_programs(0)
    slot = i % 2
    s = slots_ref[...]
    t = s.shape[1]
    s1 = s[0:1, :].astype(jnp.int32)
    s2 = s[1:2, :].astype(jnp.int32)
    rid = lax.broadcasted_iota(jnp.int32, (TILE_SLOTS, t), 0)
    m1 = rid == s1
    m2 = rid == s2
    perm = jnp.where(m1, 1.0, jnp.where(m2, 1.0, 0.0)).astype(jnp.bfloat16)
    rows = jnp.dot(perm, h_ref[...], preferred_element_type=jnp.float32)
    w = jnp.sum(jnp.where(m1, s[2:3, :], 0.0) + jnp.where(m2, s[3:4, :], 0.0), axis=1, keepdims=True)
    hi = w.astype(jnp.bfloat16).astype(jnp.float32)
    lane = lax.broadcasted_iota(jnp.int32, (TILE_SLOTS, LANES), 1)
    aux = jnp.where(lane == 0, hi, jnp.where(lane == 1, w - hi, 0.0))
    buf[slot, :, 0:D_MODEL] = rows.astype(jnp.bfloat16)
    buf[slot, :, D_MODEL:XS_W] = aux.astype(jnp.bfloat16)

    def copies(sl):
        def make(src_row, dst_row):
            return pltpu.make_async_copy(buf.at[sl, pl.ds(src_row, CHUNK), :],
                                         xs_ref.at[pl.ds(dst_row, CHUNK), :], sem.at[sl])
        return make

    _chunk_copies(i, rows_ref, copies(slot), lambda c: c.start())

    @pl.when(i > 0)
    def _():
        _wait_chunk_copies(copies(1 - slot))

    @pl.when(i == n - 1)
    def _():
        zbuf[...] = jnp.zeros_like(zbuf)

        def tails(action):
            for e in range(N_EXPERTS):
                start = tail_start_ref[e]

                def body(m, carry, start=start):
                    action(pltpu.make_async_copy(
                        zbuf, xs_ref.at[pl.ds(pl.multiple_of(start + m * CHUNK, CHUNK), CHUNK), :], zsem))
                    return carry

                lax.fori_loop(0, tail_chunks_ref[e], body, 0)

        tails(lambda c: c.start())
        tails(lambda c: c.wait())
        _wait_chunk_copies(copies(slot))


def _dispatch_call(tables, h2, slots, n_blocks):
    scatter_rows, tail_start, tail_chunks = tables
    tokens = h2.shape[0]
    t = DISPATCH_TILE
    return pl.pallas_call(
        _dispatch_kernel,
        grid_spec=pltpu.PrefetchScalarGridSpec(
            num_scalar_prefetch=3,
            grid=(tokens // t,),
            in_specs=[
                pl.BlockSpec((t, D_MODEL), lambda i, *_: (i, 0)),
                pl.BlockSpec((None, 8, t), lambda i, *_: (i, 0, 0)),
            ],
            out_specs=pl.BlockSpec(memory_space=pl.ANY),
            scratch_shapes=[
                pltpu.VMEM((2, TILE_SLOTS, XS_W), jnp.bfloat16),
                pltpu.VMEM((CHUNK, XS_W), jnp.bfloat16),
                pltpu.SemaphoreType.DMA((2,)),
                pltpu.SemaphoreType.DMA(()),
            ],
        ),
        out_shape=jax.ShapeDtypeStruct((n_blocks * EXPERT_BLOCK + 2 * TILE_SLOTS, XS_W), jnp.bfloat16),
        name="moe_dispatch",
    )(scatter_rows, tail_start, tail_chunks, h2, slots)


def _expert_kernel(bexp_ref, nused_ref, xs_ref, wg_ref, wu_ref, wd_ref, ys_ref, wgu_b, wd_b):
    b = pl.program_id(0)
    e = bexp_ref[b]
    prev = bexp_ref[jnp.maximum(b - 1, 0)]

    @pl.when((b == 0) | (e != prev))
    def _():
        wgu_b[:, 0:D_EXPERT] = wg_ref[...].astype(jnp.bfloat16)
        wgu_b[:, D_EXPERT:2 * D_EXPERT] = wu_ref[...].astype(jnp.bfloat16)
        wd_b[...] = wd_ref[...].astype(jnp.bfloat16)

    @pl.when(b < nused_ref[0])
    def _():
        x = xs_ref[:, 0:D_MODEL]
        gu = jnp.dot(x, wgu_b[...], preferred_element_type=jnp.float32)
        act = (_silu(gu[:, 0:D_EXPERT]) * gu[:, D_EXPERT:2 * D_EXPERT]).astype(jnp.bfloat16)
        y = jnp.dot(act, wd_b[...], preferred_element_type=jnp.float32)
        w = (xs_ref[:, D_MODEL:D_MODEL + 1].astype(jnp.float32)
             + xs_ref[:, D_MODEL + 1:D_MODEL + 2].astype(jnp.float32))
        ys_ref[...] = (w * y).astype(jnp.bfloat16)


def _expert_call(block_expert, n_used, xs, w_eg, w_eu, w_ed, n_blocks):
    row_blk = lambda w: pl.BlockSpec((EXPERT_BLOCK, w), lambda b, be, nu: (jnp.minimum(b, nu[0] - 1), 0))
    return pl.pallas_call(
        _expert_kernel,
        grid_spec=pltpu.PrefetchScalarGridSpec(
            num_scalar_prefetch=2,
            grid=(n_blocks,),
            in_specs=[
                row_blk(XS_W),
                pl.BlockSpec((None, D_MODEL, D_EXPERT), lambda b, be, nu: (be[b], 0, 0)),
                pl.BlockSpec((None, D_MODEL, D_EXPERT), lambda b, be, nu: (be[b], 0, 0)),
                pl.BlockSpec((None, D_EXPERT, D_MODEL), lambda b, be, nu: (be[b], 0, 0)),
            ],
            out_specs=row_blk(D_MODEL),
            scratch_shapes=[
                pltpu.VMEM((D_MODEL, 2 * D_EXPERT), jnp.bfloat16),
                pltpu.VMEM((D_EXPERT, D_MODEL), jnp.bfloat16),
            ],
        ),
        out_shape=jax.ShapeDtypeStruct((n_blocks * EXPERT_BLOCK, D_MODEL), jnp.bfloat16),
        name="moe_experts",
    )(block_expert, n_used, xs, w_eg, w_eu, w_ed)


def _combine_kernel(tile0, rows_ref, ys_ref, slots_t_ref, x1_ref, gate2_ref, gpost_ref, out_ref, buf, sem):
    j = pl.program_id(0)
    n = pl.num_programs(0)
    slot = j % 2
    tile = tile0 + j

    def copies(sl):
        def make(dst_row, src_row):
            return pltpu.make_async_copy(ys_ref.at[pl.ds(src_row, CHUNK), :],
                                         buf.at[sl, pl.ds(dst_row, CHUNK), :], sem.at[sl])
        return make

    @pl.when(j == 0)
    def _():
        _chunk_copies(tile, rows_ref, copies(slot), lambda c: c.start())

    @pl.when(j + 1 < n)
    def _():
        _chunk_copies(tile + 1, rows_ref, copies(1 - slot), lambda c: c.start())

    _wait_chunk_copies(copies(slot))
    st = slots_t_ref[...]
    s1 = st[:, 0:1].astype(jnp.int32)
    s2 = st[:, 1:2].astype(jnp.int32)
    lane = lax.broadcasted_iota(jnp.int32, (st.shape[0], TILE_SLOTS), 1)
    unperm = jnp.where(lane == s1, 1.0, jnp.where(lane == s2, 1.0, 0.0)).astype(jnp.bfloat16)
    y = jnp.dot(unperm, buf[slot], preferred_element_type=jnp.float32)
    out_ref[...] = x1_ref[...] + gate2_ref[...] * (_rms(y) * gpost_ref[...])


def _combine_call(gather_rows, ys, slots_t, x1, mod, mod_per_batch, g_post2, token_offset, bsz, seq):
    t = DISPATCH_TILE
    tile0 = token_offset // t
    tokens = bsz * seq
    if mod_per_batch:
        per = seq // t
        gate2_spec = pl.BlockSpec((None, 1, D_MODEL), lambda j, *_: (j // per, 0, 5))
    else:
        gate2_spec = pl.BlockSpec((None, 1, D_MODEL), lambda j, *_: (0, 0, 5))
    y = pl.pallas_call(
        functools.partial(_combine_kernel, tile0),
        grid_spec=pltpu.PrefetchScalarGridSpec(
            num_scalar_prefetch=1,
            grid=(tokens // t,),
            in_specs=[
                pl.BlockSpec(memory_space=pl.ANY),
                pl.BlockSpec((None, t, LANES), lambda j, *_: (tile0 + j, 0, 0)),
                pl.BlockSpec((t, D_MODEL), lambda j, *_: (tile0 + j, 0)),
                gate2_spec,
                pl.BlockSpec((1, D_MODEL), lambda j, *_: (0, 0)),
            ],
            out_specs=pl.BlockSpec((t, D_MODEL), lambda j, *_: (j, 0)),
            scratch_shapes=[
                pltpu.VMEM((2, TILE_SLOTS, D_MODEL), jnp.bfloat16),
                pltpu.SemaphoreType.DMA((2,)),
            ],
        ),
        out_shape=jax.ShapeDtypeStruct((tokens, D_MODEL), jnp.float32),
        name="moe_combine",
    )(gather_rows, ys, slots_t, x1, mod, g_post2)
    return y.reshape(bsz, seq, D_MODEL)


def kernel(x_prompt, x_sample, cache_k, cache_v, c, c_ctx, w_mod, b_mod, g_pre1, g_post1, g_pre2, g_post2,
           w_in, conv_w, lambda_q1, lambda_k1, lambda_q2, lambda_k2, subln_g, w_o, w_router_group,
           w_router_expert, w_exp_gate, w_exp_up, w_exp_down):
    n_lat = c.shape[0]
    cond = jnp.concatenate(
        [c, c_ctx[None, :], jnp.zeros((MOD_ROWS - n_lat - 1, D_MODEL), jnp.float32)], axis=0)
    mod, lam = _mod_call(cond, w_mod[0], b_mod, lambda_q1, lambda_k1, lambda_q2, lambda_k2)
    mod = mod.reshape(MOD_ROWS, 1, 6 * D_MODEL)
    mod_lat, mod_ctx = mod[:n_lat], mod[n_lat:n_lat + 1]

    w_in_b = w_in[0].astype(jnp.bfloat16)
    w_o_b = w_o[0].astype(jnp.bfloat16)
    w_router = jnp.concatenate(
        [w_router_group[0], jnp.zeros((D_MODEL, 8 - N_GROUPS), jnp.float32), w_router_expert[0],
         jnp.zeros((D_MODEL, LANES - 8 - N_EXPERTS), jnp.float32)], axis=1).astype(jnp.bfloat16)

    n_prompt = x_prompt.shape[0] * x_prompt.shape[1]
    n_sample = x_sample.shape[0] * x_sample.shape[1]
    total = n_prompt + n_sample

    def mixer(x, mod_x, per_batch, rope, ctx_kv, token_offset, carried):
        kv_dtype = jnp.bfloat16 if rope else jnp.float32
        q, k, v, yc = _in_proj_call(x, mod_x, per_batch, g_pre1, w_in_b, conv_w[0], rope, kv_dtype)
        kvs = []
        if ctx_kv is not None:
            ck, cv = ctx_kv
            n_ctx = ck.shape[3]
            kvs.append((ck, cv, lambda hp: pl.BlockSpec((None, None, hp, n_ctx, D_V),
                                                        lambda b, h, i: (b, 0, h, 0, 0))))
        seq = x.shape[1]
        kvs.append((k, v, lambda hp: pl.BlockSpec((None, hp, seq, D_V), lambda b, h, i: (b, h, 0, 0))))
        o = _attn_call(lam, subln_g, q, kvs, N_HEADS if seq <= MERGE_HEADS_MAX_SEQ else 1)
        if not per_batch:
            o, yc, x = (a.reshape(1, -1, a.shape[-1]) for a in (o, yc, x))
        shared = _out_proj_call(o, yc, w_o_b, x, mod_x, per_batch, g_post1, g_pre2, w_router,
                                total, token_offset, carried)
        return shared, k, v

    shared, kp, vp = mixer(x_prompt, mod_ctx, False, False, None, 0, None)
    (x1, h2, slots, slots_t, counts), _, _ = mixer(x_sample, mod_lat, True, True, (cache_k, cache_v),
                                                   n_prompt, shared)

    n_tiles = total // DISPATCH_TILE
    max_rows = 2 * total + n_tiles * N_EXPERTS * (CHUNK - 1) + N_EXPERTS * (EXPERT_BLOCK - CHUNK)
    n_blocks = -(-max_rows // EXPERT_BLOCK)
    scatter_rows, gather_rows, tail_start, tail_chunks, block_expert, n_used = _layout_tables(counts, n_blocks)
    xs = _dispatch_call((scatter_rows, tail_start, tail_chunks), h2, slots, n_blocks)
    ys = _expert_call(block_expert, n_used, xs, w_exp_gate[0], w_exp_up[0], w_exp_down[0], n_blocks)
    yp = _combine_call(gather_rows, ys, slots_t, x1, mod_ctx, False, g_post2, 0,
                       x_prompt.shape[0], x_prompt.shape[1])
    ysamp = _combine_call(gather_rows, ys, slots_t, x1, mod_lat, True, g_post2, n_prompt,
                          x_sample.shape[0], x_sample.shape[1])
    return yp, ysamp, kp[:, None], vp[:, None]
```

```python
import functools
import math

import numpy as np
import jax
import jax.numpy as jnp
from jax import lax
from jax.experimental import pallas as pl
from jax.experimental.pallas import tpu as pltpu

D_MODEL = 1024
GRID_W = 64
N_HEADS = 4
D_QK = 64
D_V = 128
ATTN_W = N_HEADS * D_V
CONV_W = D_MODEL - ATTN_W
IN_W = 3 * ATTN_W + 3 * CONV_W
N_GROUPS = 4
EXP_PER_GROUP = 4
N_EXPERTS = N_GROUPS * EXP_PER_GROUP
D_EXPERT = 512
ROPE_BASE = 10000.0
EPS = 1e-6
LAM_INIT = 0.8 - 0.6 * math.exp(-0.3 * 0)

LANES = 128
F32_SUBLANES = 8
BF16_SUBLANES = 16
MOD_ROWS = 16
TOKEN_TILE = 512
OUT_SUB_ROWS = 256
Q_TILE = 2048
KEY_CHUNK = 256
MERGE_HEADS_MAX_SEQ = 512
DISPATCH_TILE = 512
CHUNK = BF16_SUBLANES
TILE_SLOTS = 1280
N_CHUNKS = TILE_SLOTS // CHUNK
SLOT_GROUP = 256
COMBINE_SUB_ROWS = 256
EXPERT_BLOCK = 512
XS_W = D_MODEL + LANES
QK_SCALE = (1.0 / math.sqrt(D_QK)) * math.log2(math.e)

NT_DIMS = (((1,), (1,)), ((), ()))


def _rms(x):
    return x * lax.rsqrt(jnp.mean(x * x, axis=-1, keepdims=True) + EPS)


def _silu(x):
    return x * (1.0 / (1.0 + jnp.exp(-x)))


def _mod_kernel(cond_ref, w_ref, b_ref, lq1_ref, lk1_ref, lq2_ref, lk2_ref, mod_ref, lam_ref):
    s = _silu(cond_ref[...])
    m = lax.dot_general(s, w_ref[...], (((1,), (0,)), ((), ())),
                        precision=lax.Precision.HIGHEST, preferred_element_type=jnp.float32)
    mod_ref[...] = m + b_ref[...]
    a = jnp.sum(lq1_ref[...] * lk1_ref[...], axis=-1, keepdims=True)
    b = jnp.sum(lq2_ref[...] * lk2_ref[...], axis=-1, keepdims=True)
    lam_ref[...] = jnp.broadcast_to(jnp.exp(a) - jnp.exp(b) + LAM_INIT, lam_ref.shape)


def _mod_call(cond, w_mod, b_mod, lq1, lk1, lq2, lk2):
    n_col = 6 * D_MODEL
    col_tile = 1536
    small = pl.BlockSpec((1, D_QK), lambda j: (0, 0))
    return pl.pallas_call(
        _mod_kernel,
        grid=(n_col // col_tile,),
        in_specs=[
            pl.BlockSpec((MOD_ROWS, D_MODEL), lambda j: (0, 0)),
            pl.BlockSpec((D_MODEL, col_tile), lambda j: (0, j)),
            pl.BlockSpec((1, col_tile), lambda j: (0, j)),
            small, small, small, small,
        ],
        out_specs=[
            pl.BlockSpec((MOD_ROWS, col_tile), lambda j: (0, j)),
            pl.BlockSpec((1, LANES), lambda j: (0, 0)),
        ],
        out_shape=[
            jax.ShapeDtypeStruct((MOD_ROWS, n_col), jnp.float32),
            jax.ShapeDtypeStruct((1, LANES), jnp.float32),
        ],
        name="mod",
    )(cond, w_mod, b_mod, lq1, lk1, lq2, lk2)


def _in_proj_kernel(rope, n_tiles, x_ref, xp_ref, xn_ref, shift_ref, scale_ref, g_ref, w_ref, cw_ref, *rest):
    if rope:
        cos_ref, sina_ref, sinb_ref, q_ref, k_ref, v_ref, yc_ref = rest
    else:
        q_ref, k_ref, v_ref, yc_ref = rest
    i = pl.program_id(1)
    tm = x_ref.shape[0]
    gain = g_ref[...] * (1.0 + scale_ref[...])
    shift = shift_ref[...]

    def modulate(x):
        return (_rms(x) * gain + shift).astype(jnp.bfloat16)

    h = modulate(x_ref[...])
    h_halo = modulate(jnp.concatenate([xp_ref[...], xn_ref[...]], axis=0))

    def proj(lhs, lo, hi):
        return jnp.dot(lhs, w_ref[:, lo:hi], preferred_element_type=jnp.float32)

    def rot(t):
        return (t * cos_ref[...] + pltpu.roll(t, LANES - 16, axis=1) * sina_ref[...]
                + pltpu.roll(t, 16, axis=1) * sinb_ref[...])

    c0 = 3 * ATTN_W
    h_ext = jnp.concatenate([h, h_halo], axis=0)
    cu_all = proj(h_ext, c0 + CONV_W, c0 + 2 * CONV_W) * proj(h_ext, c0 + 2 * CONV_W, c0 + 3 * CONV_W)
    gb = proj(h, c0, c0 + CONV_W)
    cu = cu_all[0:tm]
    prev_row = jnp.where(i > 0, cu_all[tm + 7:tm + 8], 0.0)
    next_row = jnp.where(i < n_tiles - 1, cu_all[tm + 8:tm + 9], 0.0)
    row = lax.broadcasted_iota(jnp.int32, cu.shape, 0)
    prev = jnp.where(row == 0, prev_row, pltpu.roll(cu, 1, axis=0))
    nxt = jnp.where(row == tm - 1, next_row, pltpu.roll(cu, tm - 1, axis=0))
    conv = cw_ref[0:1, :] * prev + cw_ref[1:2, :] * cu + cw_ref[2:3, :] * nxt
    yc_ref[...] = (gb * conv).astype(yc_ref.dtype)

    zq = proj(h, 0, ATTN_W)
    for hd in range(N_HEADS):
        q = zq[:, hd * D_V:(hd + 1) * D_V]
        q_ref[hd] = ((rot(q) if rope else q) * QK_SCALE).astype(q_ref.dtype)
    zk = proj(h, ATTN_W, 2 * ATTN_W)
    for hd in range(N_HEADS):
        k = zk[:, hd * D_V:(hd + 1) * D_V]
        k_ref[hd] = (rot(k) if rope else k).astype(k_ref.dtype)
    zv = proj(h, 2 * ATTN_W, 3 * ATTN_W)
    for hd in range(N_HEADS):
        v_ref[hd] = zv[:, hd * D_V:(hd + 1) * D_V].astype(v_ref.dtype)


def _rope_tables(seq):
    n_rows = seq // GRID_W
    row = np.repeat(np.arange(n_rows), GRID_W).astype(np.float64)
    col = np.tile(np.arange(GRID_W), n_rows).astype(np.float64)
    nf = D_QK // 4
    inv = ROPE_BASE ** (-np.arange(nf, dtype=np.float64) / nf)
    ar = row[:, None] * inv
    ac = col[:, None] * inv
    ang = np.concatenate([ar, ar, ac, ac], axis=-1)
    ang = np.concatenate([ang, ang], axis=-1)
    first_half = (np.arange(LANES) % 32) < 16
    cos = np.cos(ang)
    sin = np.sin(ang)
    sina = np.where(first_half, -sin, 0.0)
    sinb = np.where(first_half, 0.0, sin)
    return tuple(jnp.asarray(t, dtype=jnp.float32) for t in (cos, sina, sinb))


def _in_proj_call(x, mod, mod_per_batch, g_pre1, w_in, conv_w, rope, kv_dtype):
    bsz, seq, _ = x.shape
    tm = min(TOKEN_TILE, seq)
    n_tiles = seq // tm
    halo = tm // F32_SUBLANES
    n_halo = seq // F32_SUBLANES

    def mod_spec(chunk):
        if mod_per_batch:
            return pl.BlockSpec((None, 1, D_MODEL), lambda b, i: (b, 0, chunk))
        return pl.BlockSpec((None, 1, D_MODEL), lambda b, i: (0, 0, chunk))

    in_specs = [
        pl.BlockSpec((None, tm, D_MODEL), lambda b, i: (b, i, 0)),
        pl.BlockSpec((None, F32_SUBLANES, D_MODEL), lambda b, i: (b, jnp.maximum(i * halo - 1, 0), 0)),
        pl.BlockSpec((None, F32_SUBLANES, D_MODEL), lambda b, i: (b, jnp.minimum((i + 1) * halo, n_halo - 1), 0)),
        mod_spec(0), mod_spec(1),
        pl.BlockSpec((1, D_MODEL), lambda b, i: (0, 0)),
        pl.BlockSpec((D_MODEL, IN_W), lambda b, i: (0, 0)),
        pl.BlockSpec((3, CONV_W), lambda b, i: (0, 0)),
    ]
    args = [x, x, x, mod, mod, g_pre1, w_in, conv_w]
    if rope:
        in_specs += [pl.BlockSpec((tm, LANES), lambda b, i: (i, 0))] * 3
        args += list(_rope_tables(seq))
    head_spec = pl.BlockSpec((None, N_HEADS, tm, D_V), lambda b, i: (b, 0, i, 0))
    return pl.pallas_call(
        functools.partial(_in_proj_kernel, rope, n_tiles),
        grid=(bsz, n_tiles),
        in_specs=in_specs,
        out_specs=[head_spec, head_spec, head_spec, pl.BlockSpec((None, tm, CONV_W), lambda b, i: (b, i, 0))],
        out_shape=[
            jax.ShapeDtypeStruct((bsz, N_HEADS, seq, D_V), jnp.bfloat16),
            jax.ShapeDtypeStruct((bsz, N_HEADS, seq, D_V), kv_dtype),
            jax.ShapeDtypeStruct((bsz, N_HEADS, seq, D_V), kv_dtype),
            jax.ShapeDtypeStruct((bsz, seq, CONV_W), jnp.bfloat16),
        ],
        name="in_proj_rope" if rope else "in_proj",
    )(*args)


def _attn_kernel(n_kv, lam_ref, g_ref, q_ref, *rest):
    kv_refs = rest[:2 * n_kv]
    o_ref = rest[2 * n_kv]
    heads, tq, _ = q_ref.shape
    lane = lax.broadcasted_iota(jnp.int32, (tq, 2 * D_QK), 1)
    for hh in range(heads):
        q = q_ref[hh]
        zero = jnp.zeros_like(q)
        halves = (jnp.where(lane < D_QK, q, zero), jnp.where(lane >= D_QK, q, zero))
        acc = [jnp.zeros((tq, 2 * D_V), jnp.float32) for _ in range(2)]
        m = [jnp.full((tq, 1), -1e30, jnp.float32) for _ in range(2)]
        for j in range(n_kv):
            k_ref, v_ref = kv_refs[2 * j], kv_refs[2 * j + 1]
            n_keys = k_ref.shape[1]
            ck = min(KEY_CHUNK, n_keys)
            ones = jnp.ones((ck, D_V), jnp.bfloat16)
            for c in range(n_keys // ck):
                k = k_ref[hh, c * ck:(c + 1) * ck, :].astype(jnp.bfloat16)
                v1 = jnp.concatenate([v_ref[hh, c * ck:(c + 1) * ck, :].astype(jnp.bfloat16), ones], axis=1)
                for x, qh in enumerate(halves):
                    s = lax.dot_general(qh, k, NT_DIMS, preferred_element_type=jnp.float32)
                    m_new = jnp.maximum(m[x], jnp.max(s, axis=-1, keepdims=True))
                    p = jnp.exp2(s - m_new).astype(jnp.bfloat16)
                    acc[x] = (jnp.exp2(m[x] - m_new) * acc[x]
                              + jnp.dot(p, v1, preferred_element_type=jnp.float32))
                    m[x] = m_new
        o = (acc[0][:, 0:D_V] / acc[0][:, D_V:2 * D_V]
             - lam_ref[0:1, 0:1] * (acc[1][:, 0:D_V] / acc[1][:, D_V:2 * D_V]))
        o_ref[:, hh * D_V:(hh + 1) * D_V] = (_rms(o) * (g_ref[...] * (1.0 - LAM_INIT))).astype(o_ref.dtype)


def _attn_call(lam, subln_g, q, kvs, heads_per_step):
    bsz, _, seq, _ = q.shape
    tq = min(Q_TILE, seq)
    hp = heads_per_step
    in_specs = [
        pl.BlockSpec((1, LANES), lambda b, h, i: (0, 0)),
        pl.BlockSpec((1, D_V), lambda b, h, i: (0, 0)),
        pl.BlockSpec((None, hp, tq, D_V), lambda b, h, i: (b, h, i, 0)),
    ]
    args = [lam, subln_g, q]
    for k, v, spec_fn in kvs:
        in_specs += [spec_fn(hp), spec_fn(hp)]
        args += [k, v]
    return pl.pallas_call(
        functools.partial(_attn_kernel, len(kvs)),
        grid=(bsz, N_HEADS // hp, seq // tq),
        in_specs=in_specs,
        out_specs=pl.BlockSpec((None, tq, hp * D_V), lambda b, h, i: (b, i, h)),
        out_shape=jax.ShapeDtypeStruct((bsz, seq, ATTN_W), jnp.bfloat16),
        name="attn%d" % len(kvs),
    )(*args)


def _route(logits_t):
    lg = [logits_t[g:g + 1, :] for g in range(N_GROUPS)]
    mg = functools.reduce(jnp.maximum, lg)
    p_sel = 1.0 / functools.reduce(jnp.add, [jnp.exp(t - mg) for t in lg])
    g_sel = jnp.full(mg.shape, N_GROUPS - 1, jnp.int32)
    for g in range(N_GROUPS - 2, -1, -1):
        g_sel = jnp.where(lg[g] == mg, g, g_sel)
    le = []
    for j in range(EXP_PER_GROUP):
        t = jnp.zeros_like(mg)
        for g in range(N_GROUPS):
            r = 8 + g * EXP_PER_GROUP + j
            t = jnp.where(g_sel == g, logits_t[r:r + 1, :], t)
        le.append(t)
    m1 = functools.reduce(jnp.maximum, le)
    i1 = jnp.full(mg.shape, EXP_PER_GROUP - 1, jnp.int32)
    for j in range(EXP_PER_GROUP - 2, -1, -1):
        i1 = jnp.where(le[j] == m1, j, i1)
    neg = jnp.float32(-jnp.inf)
    rest = [jnp.where(i1 == j, neg, le[j]) for j in range(EXP_PER_GROUP)]
    m2 = functools.reduce(jnp.maximum, rest)
    i2 = jnp.full(mg.shape, EXP_PER_GROUP - 1, jnp.int32)
    for j in range(EXP_PER_GROUP - 2, -1, -1):
        i2 = jnp.where(rest[j] == m2, j, i2)
    e2 = jnp.exp(m2 - m1)
    w1 = p_sel / (1.0 + e2)
    w2 = p_sel * e2 / (1.0 + e2)
    base = g_sel * EXP_PER_GROUP
    return (base + i1).astype(jnp.float32), (base + i2).astype(jnp.float32), w1, w2


def _out_proj_kernel(o_ref, yc_ref, wo_ref, x_ref, gate1_ref, shift2_ref, scale2_ref, gpost_ref, gpre_ref,
                     wr_ref, *rest):
    x1_ref, h2_ref, slots_ref, slots_t_ref, counts_ref = rest[-5:]
    tm = o_ref.shape[0]
    sub = min(OUT_SUB_ROWS, tm)
    gain1 = gate1_ref[...] * gpost_ref[...]
    gain2 = gpre_ref[...] * (1.0 + scale2_ref[...])
    routes = []
    for r in range(tm // sub):
        rows = slice(r * sub, (r + 1) * sub)
        out = (jnp.dot(o_ref[rows, :], wo_ref[0:ATTN_W, :], preferred_element_type=jnp.float32)
               + jnp.dot(yc_ref[rows, :], wo_ref[ATTN_W:D_MODEL, :], preferred_element_type=jnp.float32))
        x1 = x_ref[rows, :] + _rms(out) * gain1
        x1_ref[rows, :] = x1
        h2 = (_rms(x1) * gain2 + shift2_ref[...]).astype(jnp.bfloat16)
        h2_ref[rows, :] = h2
        logits = jnp.dot(h2, wr_ref[...], preferred_element_type=jnp.float32)
        routes.append(_route(logits.T))
    e1, e2, w1, w2 = (jnp.concatenate(parts, axis=1) for parts in zip(*routes))
    _plan(e1, e2, w1, w2, slots_ref, slots_t_ref, counts_ref)


def _out_proj_call(o, yc, w_o, x, mod, mod_per_batch, g_post1, g_pre2, w_router,
                   total_tokens, token_offset, carried):
    bsz, seq, _ = x.shape
    tm = DISPATCH_TILE
    assert seq % tm == 0 and token_offset % tm == 0
    n_tiles = seq // tm
    tile0 = token_offset // tm
    n_all = total_tokens // tm

    def mod_spec(chunk):
        if mod_per_batch:
            return pl.BlockSpec((None, 1, D_MODEL), lambda b, i: (b, 0, chunk))
        return pl.BlockSpec((None, 1, D_MODEL), lambda b, i: (0, 0, chunk))

    def vec_spec():
        return pl.BlockSpec((1, D_MODEL), lambda b, i: (0, 0))

    tok = lambda w: pl.BlockSpec((None, tm, w), lambda b, i: (b, i, 0))
    flat = lambda w: pl.BlockSpec((tm, w), lambda b, i: (tile0 + b * n_tiles + i, 0))
    per_tile = lambda r, c: pl.BlockSpec((None, r, c), lambda b, i: (tile0 + b * n_tiles + i, 0, 0))
    in_specs = [
        tok(ATTN_W), tok(CONV_W),
        pl.BlockSpec((D_MODEL, D_MODEL), lambda b, i: (0, 0)),
        tok(D_MODEL),
        mod_spec(2), mod_spec(3), mod_spec(4),
        vec_spec(), vec_spec(),
        pl.BlockSpec((D_MODEL, LANES), lambda b, i: (0, 0)),
    ]
    args = [o, yc, w_o, x, mod, mod, mod, g_post1, g_pre2, w_router]
    aliases = {}
    if carried is not None:
        for j, arr in enumerate(carried):
            aliases[len(args)] = j
            in_specs.append(pl.BlockSpec(memory_space=pl.ANY))
            args.append(arr)
    return pl.pallas_call(
        _out_proj_kernel,
        grid=(bsz, n_tiles),
        in_specs=in_specs,
        out_specs=[flat(D_MODEL), flat(D_MODEL), per_tile(8, tm), per_tile(tm, LANES), per_tile(N_EXPERTS, LANES)],
        out_shape=[
            jax.ShapeDtypeStruct((total_tokens, D_MODEL), jnp.float32),
            jax.ShapeDtypeStruct((total_tokens, D_MODEL), jnp.bfloat16),
            jax.ShapeDtypeStruct((n_all, 8, tm), jnp.float32),
            jax.ShapeDtypeStruct((n_all, tm, LANES), jnp.float32),
            jax.ShapeDtypeStruct((n_all, N_EXPERTS, LANES), jnp.float32),
        ],
        input_output_aliases=aliases,
        name="out_proj",
    )(*args)


def _plan(e1, e2, w1, w2, slots_ref, slots_t_ref, counts_ref):
    t = e1.shape[1]
    e1 = e1.astype(jnp.int32)
    e2 = e2.astype(jnp.int32)
    eid = lax.broadcasted_iota(jnp.int32, (N_EXPERTS, t), 0)
    hot1 = jnp.where(eid == e1, 1.0, 0.0)
    hot2 = jnp.where(eid == e2, 1.0, 0.0)
    hot = jnp.concatenate([hot1, hot2], axis=0).astype(jnp.bfloat16)
    before = (lax.broadcasted_iota(jnp.int32, (t, t), 0) < lax.broadcasted_iota(jnp.int32, (t, t), 1))
    before = jnp.where(before, 1.0, 0.0).astype(jnp.bfloat16)
    rank = jnp.dot(hot, before, preferred_element_type=jnp.float32)
    n1 = jnp.sum(hot1, axis=1, keepdims=True)
    n2 = jnp.sum(hot2, axis=1, keepdims=True)
    ones = jnp.ones((8, t), jnp.bfloat16)
    cnt_row = lax.dot_general(ones, (hot1 + hot2).astype(jnp.bfloat16), NT_DIMS,
                              preferred_element_type=jnp.float32)[0:1, :]
    pad_row = jnp.floor((cnt_row + (CHUNK - 1)) * (1.0 / CHUNK)) * CHUNK
    lower = (lax.broadcasted_iota(jnp.int32, (N_EXPERTS, N_EXPERTS), 1)
             < lax.broadcasted_iota(jnp.int32, (N_EXPERTS, N_EXPERTS), 0))
    base = jnp.sum(jnp.where(lower, pad_row, 0.0), axis=1, keepdims=True)
    slot1 = jnp.sum(hot1 * (base + rank[:N_EXPERTS]), axis=0, keepdims=True)
    slot2 = jnp.sum(hot2 * (base + n1 + rank[N_EXPERTS:]), axis=0, keepdims=True)
    slots_ref[...] = jnp.concatenate([slot1, slot2, w1, w2, jnp.zeros((4, t), jnp.float32)], axis=0)
    wide = jnp.concatenate([slot1, slot2, jnp.zeros((LANES - 2, t), jnp.float32)], axis=0)
    slots_t_ref[...] = wide.T
    counts_ref[...] = jnp.broadcast_to(n1 + n2, counts_ref.shape)


def _layout_tables(counts, n_blocks):
    cnt = counts[:, :, 0].astype(jnp.int32)
    n_tiles = cnt.shape[0]
    nch = (cnt + (CHUNK - 1)) // CHUNK
    csum = jnp.cumsum(nch, axis=0)
    rows = CHUNK * csum[-1]
    blocks = (rows + (EXPERT_BLOCK - 1)) // EXPERT_BLOCK
    blk_end = jnp.cumsum(blocks)
    blk_start = blk_end - blocks
    off = blk_start[None, :] * EXPERT_BLOCK + CHUNK * (csum - nch)
    first = jnp.cumsum(nch, axis=1) - nch
    used = jnp.sum(nch, axis=1)
    c = jnp.arange(N_CHUNKS, dtype=jnp.int32)
    e_idx = jnp.sum((c[None, :, None] >= first[:, None, :]).astype(jnp.int32), axis=2) - 1
    hot = e_idx[:, :, None] == jnp.arange(N_EXPERTS, dtype=jnp.int32)[None, None, :]
    pick = lambda a: jnp.sum(jnp.where(hot, a[:, None, :], 0), axis=2)
    row = pick(off) + CHUNK * (c[None, :] - pick(first))
    valid = c[None, :] < used[:, None]
    spare = (n_blocks * EXPERT_BLOCK + (jnp.arange(n_tiles, dtype=jnp.int32) % 2)[:, None] * TILE_SLOTS
             + CHUNK * c[None, :])
    scatter_rows = jnp.where(valid, row, spare)
    gather_rows = jnp.where(valid, row, 0)
    n_used = blk_end[-1]
    bc = jnp.minimum(jnp.arange(n_blocks, dtype=jnp.int32), n_used - 1)
    block_expert = jnp.sum((bc[:, None] >= blk_end[None, :]).astype(jnp.int32), axis=1)
    tail_start = blk_start * EXPERT_BLOCK + rows
    tail_chunks = (blocks * EXPERT_BLOCK - rows) // CHUNK
    flat = lambda a: a.reshape(-1).astype(jnp.int32)
    return (flat(scatter_rows), flat(gather_rows), flat(tail_start), flat(tail_chunks),
            flat(block_expert), flat(n_used))


def _chunk_copies(tile, rows_ref, make_copy, action):
    for c in range(N_CHUNKS):
        action(make_copy(c * CHUNK, pl.multiple_of(rows_ref[tile * N_CHUNKS + c], CHUNK)))


def _wait_chunk_copies(make_copy):
    for _ in range(N_CHUNKS):
        make_copy(0, 0).wait()


def _dispatch_kernel(rows_ref, tail_start_ref, tail_chunks_ref, h_ref, slots_ref, xs_ref, buf, zbuf, sem, zsem):
    i = pl.program_id(0)
    n = pl.num_programs(0)
    slot = i % 2
    s = slots_ref[...]
    t = s.shape[1]
    s1 = s[0:1, :].astype(jnp.int32)
    s2 = s[1:2, :].astype(jnp.int32)
    h = h_ref[...]
    lane = lax.broadcasted_iota(jnp.int32, (SLOT_GROUP, LANES), 1)
    for g in range(TILE_SLOTS // SLOT_GROUP):
        grp = slice(g * SLOT_GROUP, (g + 1) * SLOT_GROUP)
        rid = lax.broadcasted_iota(jnp.int32, (SLOT_GROUP, t), 0) + g * SLOT_GROUP
        m1 = rid == s1
        m2 = rid == s2
        perm = jnp.where(m1, 1.0, jnp.where(m2, 1.0, 0.0)).astype(jnp.bfloat16)
        rows = jnp.dot(perm, h, preferred_element_type=jnp.float32)
        w = jnp.sum(jnp.where(m1, s[2:3, :], 0.0) + jnp.where(m2, s[3:4, :], 0.0), axis=1, keepdims=True)
        hi = w.astype(jnp.bfloat16).astype(jnp.float32)
        aux = jnp.where(lane == 0, hi, jnp.where(lane == 1, w - hi, 0.0))
        buf[slot, grp, 0:D_MODEL] = rows.astype(jnp.bfloat16)
        buf[slot, grp, D_MODEL:XS_W] = aux.astype(jnp.bfloat16)

    def copies(sl):
        def make(src_row, dst_row):
            return pltpu.make_async_copy(buf.at[sl, pl.ds(src_row, CHUNK), :],
                                         xs_ref.at[pl.ds(dst_row, CHUNK), :], sem.at[sl])
        return make

    _chunk_copies(i, rows_ref, copies(slot), lambda c: c.start())

    @pl.when(i > 0)
    def _():
        _wait_chunk_copies(copies(1 - slot))

    @pl.when(i == n - 1)
    def _():
        zbuf[...] = jnp.zeros_like(zbuf)

        def tails(action):
            for e in range(N_EXPERTS):
                start = tail_start_ref[e]

                def body(m, carry, start=start):
                    action(pltpu.make_async_copy(
                        zbuf, xs_ref.at[pl.ds(pl.multiple_of(start + m * CHUNK, CHUNK), CHUNK), :], zsem))
                    return carry

                lax.fori_loop(0, tail_chunks_ref[e], body, 0)

        tails(lambda c: c.start())
        tails(lambda c: c.wait())
        _wait_chunk_copies(copies(slot))


def _dispatch_call(tables, h2, slots, n_blocks):
    scatter_rows, tail_start, tail_chunks = tables
    tokens = h2.shape[0]
    t = DISPATCH_TILE
    return pl.pallas_call(
        _dispatch_kernel,
        grid_spec=pltpu.PrefetchScalarGridSpec(
            num_scalar_prefetch=3,
            grid=(tokens // t,),
            in_specs=[
                pl.BlockSpec((t, D_MODEL), lambda i, *_: (i, 0)),
                pl.BlockSpec((None, 8, t), lambda i, *_: (i, 0, 0)),
            ],
            out_specs=pl.BlockSpec(memory_space=pl.ANY),
            scratch_shapes=[
                pltpu.VMEM((2, TILE_SLOTS, XS_W), jnp.bfloat16),
                pltpu.VMEM((CHUNK, XS_W), jnp.bfloat16),
                pltpu.SemaphoreType.DMA((2,)),
                pltpu.SemaphoreType.DMA(()),
            ],
        ),
        out_shape=jax.ShapeDtypeStruct((n_blocks * EXPERT_BLOCK + 2 * TILE_SLOTS, XS_W), jnp.bfloat16),
        name="moe_dispatch",
    )(scatter_rows, tail_start, tail_chunks, h2, slots)


def _expert_kernel(bexp_ref, nused_ref, xs_ref, wg_ref, wu_ref, wd_ref, ys_ref, wgu_b, wd_b):
    b = pl.program_id(0)
    e = bexp_ref[b]
    prev = bexp_ref[jnp.maximum(b - 1, 0)]

    @pl.when((b == 0) | (e != prev))
    def _():
        wgu_b[:, 0:D_EXPERT] = wg_ref[...].astype(jnp.bfloat16)
        wgu_b[:, D_EXPERT:2 * D_EXPERT] = wu_ref[...].astype(jnp.bfloat16)
        wd_b[...] = wd_ref[...].astype(jnp.bfloat16)

    @pl.when(b < nused_ref[0])
    def _():
        x = xs_ref[:, 0:D_MODEL]
        y = None
        half = D_EXPERT // 2
        for c in range(2):
            cols = slice(c * half, (c + 1) * half)
            g = jnp.dot(x, wgu_b[:, c * half:(c + 1) * half], preferred_element_type=jnp.float32)
            u = jnp.dot(x, wgu_b[:, D_EXPERT + c * half:D_EXPERT + (c + 1) * half],
                        preferred_element_type=jnp.float32)
            act = (_silu(g) * u).astype(jnp.bfloat16)
            t = jnp.dot(act, wd_b[cols, :], preferred_element_type=jnp.float32)
            y = t if y is None else y + t
        w = (xs_ref[:, D_MODEL:D_MODEL + 1].astype(jnp.float32)
             + xs_ref[:, D_MODEL + 1:D_MODEL + 2].astype(jnp.float32))
        ys_ref[...] = (w * y).astype(jnp.bfloat16)


def _expert_call(block_expert, n_used, xs, w_eg, w_eu, w_ed, n_blocks):
    row_blk = lambda w: pl.BlockSpec((EXPERT_BLOCK, w), lambda b, be, nu: (jnp.minimum(b, nu[0] - 1), 0))
    return pl.pallas_call(
        _expert_kernel,
        grid_spec=pltpu.PrefetchScalarGridSpec(
            num_scalar_prefetch=2,
            grid=(n_blocks,),
            in_specs=[
                row_blk(XS_W),
                pl.BlockSpec((None, D_MODEL, D_EXPERT), lambda b, be, nu: (be[b], 0, 0)),
                pl.BlockSpec((None, D_MODEL, D_EXPERT), lambda b, be, nu: (be[b], 0, 0)),
                pl.BlockSpec((None, D_EXPERT, D_MODEL), lambda b, be, nu: (be[b], 0, 0)),
            ],
            out_specs=row_blk(D_MODEL),
            scratch_shapes=[
                pltpu.VMEM((D_MODEL, 2 * D_EXPERT), jnp.bfloat16),
                pltpu.VMEM((D_EXPERT, D_MODEL), jnp.bfloat16),
            ],
        ),
        out_shape=jax.ShapeDtypeStruct((n_blocks * EXPERT_BLOCK, D_MODEL), jnp.bfloat16),
        name="moe_experts",
    )(block_expert, n_used, xs, w_eg, w_eu, w_ed)


def _combine_kernel(tile0, rows_ref, ys_ref, slots_t_ref, x1_ref, gate2_ref, gpost_ref, out_ref, buf, sem):
    j = pl.program_id(0)
    n = pl.num_programs(0)
    slot = j % 2
    tile = tile0 + j

    def copies(sl):
        def make(dst_row, src_row):
            return pltpu.make_async_copy(ys_ref.at[pl.ds(src_row, CHUNK), :],
                                         buf.at[sl, pl.ds(dst_row, CHUNK), :], sem.at[sl])
        return make

    @pl.when(j == 0)
    def _():
        _chunk_copies(tile, rows_ref, copies(slot), lambda c: c.start())

    @pl.when(j + 1 < n)
    def _():
        _chunk_copies(tile + 1, rows_ref, copies(1 - slot), lambda c: c.start())

    _wait_chunk_copies(copies(slot))
    gain = gate2_ref[...] * gpost_ref[...]
    sorted_rows = buf[slot]
    lane = lax.broadcasted_iota(jnp.int32, (COMBINE_SUB_ROWS, TILE_SLOTS), 1)
    for r in range(out_ref.shape[0] // COMBINE_SUB_ROWS):
        rows = slice(r * COMBINE_SUB_ROWS, (r + 1) * COMBINE_SUB_ROWS)
        s1 = slots_t_ref[rows, 0:1].astype(jnp.int32)
        s2 = slots_t_ref[rows, 1:2].astype(jnp.int32)
        unperm = jnp.where(lane == s1, 1.0, jnp.where(lane == s2, 1.0, 0.0)).astype(jnp.bfloat16)
        y = jnp.dot(unperm, sorted_rows, preferred_element_type=jnp.float32)
        out_ref[rows, :] = x1_ref[rows, :] + _rms(y) * gain


def _combine_call(gather_rows, ys, slots_t, x1, mod, mod_per_batch, g_post2, token_offset, bsz, seq):
    t = DISPATCH_TILE
    tile0 = token_offset // t
    tokens = bsz * seq
    if mod_per_batch:
        per = seq // t
        gate2_spec = pl.BlockSpec((None, 1, D_MODEL), lambda j, *_: (j // per, 0, 5))
    else:
        gate2_spec = pl.BlockSpec((None, 1, D_MODEL), lambda j, *_: (0, 0, 5))
    y = pl.pallas_call(
        functools.partial(_combine_kernel, tile0),
        grid_spec=pltpu.PrefetchScalarGridSpec(
            num_scalar_prefetch=1,
            grid=(tokens // t,),
            in_specs=[
                pl.BlockSpec(memory_space=pl.ANY),
                pl.BlockSpec((None, t, LANES), lambda j, *_: (tile0 + j, 0, 0)),
                pl.BlockSpec((t, D_MODEL), lambda j, *_: (tile0 + j, 0)),
                gate2_spec,
                pl.BlockSpec((1, D_MODEL), lambda j, *_: (0, 0)),
            ],
            out_specs=pl.BlockSpec((t, D_MODEL), lambda j, *_: (j, 0)),
            scratch_shapes=[
                pltpu.VMEM((2, TILE_SLOTS, D_MODEL), jnp.bfloat16),
                pltpu.SemaphoreType.DMA((2,)),
            ],
        ),
        out_shape=jax.ShapeDtypeStruct((tokens, D_MODEL), jnp.float32),
        name="moe_combine",
    )(gather_rows, ys, slots_t, x1, mod, g_post2)
    return y.reshape(bsz, seq, D_MODEL)


def kernel(x_prompt, x_sample, cache_k, cache_v, c, c_ctx, w_mod, b_mod, g_pre1, g_post1, g_pre2, g_post2,
           w_in, conv_w, lambda_q1, lambda_k1, lambda_q2, lambda_k2, subln_g, w_o, w_router_group,
           w_router_expert, w_exp_gate, w_exp_up, w_exp_down):
    n_lat = c.shape[0]
    cond = jnp.concatenate(
        [c, c_ctx[None, :], jnp.zeros((MOD_ROWS - n_lat - 1, D_MODEL), jnp.float32)], axis=0)
    mod, lam = _mod_call(cond, w_mod[0], b_mod, lambda_q1, lambda_k1, lambda_q2, lambda_k2)
    mod = mod.reshape(MOD_ROWS, 1, 6 * D_MODEL)
    mod_lat, mod_ctx = mod[:n_lat], mod[n_lat:n_lat + 1]

    w_in_b = w_in[0].astype(jnp.bfloat16)
    w_o_b = w_o[0].astype(jnp.bfloat16)
    w_router = jnp.concatenate(
        [w_router_group[0], jnp.zeros((D_MODEL, 8 - N_GROUPS), jnp.float32), w_router_expert[0],
         jnp.zeros((D_MODEL, LANES - 8 - N_EXPERTS), jnp.float32)], axis=1).astype(jnp.bfloat16)

    n_prompt = x_prompt.shape[0] * x_prompt.shape[1]
    n_sample = x_sample.shape[0] * x_sample.shape[1]
    total = n_prompt + n_sample

    def mixer(x, mod_x, per_batch, rope, ctx_kv, token_offset, carried):
        kv_dtype = jnp.bfloat16 if rope else jnp.float32
        q, k, v, yc = _in_proj_call(x, mod_x, per_batch, g_pre1, w_in_b, conv_w[0], rope, kv_dtype)
        kvs = []
        if ctx_kv is not None:
            ck, cv = ctx_kv
            n_ctx = ck.shape[3]
            kvs.append((ck, cv, lambda hp: pl.BlockSpec((None, None, hp, n_ctx, D_V),
                                                        lambda b, h, i: (b, 0, h, 0, 0))))
        seq = x.shape[1]
        kvs.append((k, v, lambda hp: pl.BlockSpec((None, hp, seq, D_V), lambda b, h, i: (b, h, 0, 0))))
        o = _attn_call(lam, subln_g, q, kvs, N_HEADS if seq <= MERGE_HEADS_MAX_SEQ else 1)
        if not per_batch:
            o, yc, x = (a.reshape(1, -1, a.shape[-1]) for a in (o, yc, x))
        shared = _out_proj_call(o, yc, w_o_b, x, mod_x, per_batch, g_post1, g_pre2, w_router,
                                total, token_offset, carried)
        return shared, k, v

    shared, kp, vp = mixer(x_prompt, mod_ctx, False, False, None, 0, None)
    (x1, h2, slots, slots_t, counts), _, _ = mixer(x_sample, mod_lat, True, True, (cache_k, cache_v),
                                                   n_prompt, shared)

    n_tiles = total // DISPATCH_TILE
    max_rows = 2 * total + n_tiles * N_EXPERTS * (CHUNK - 1) + N_EXPERTS * (EXPERT_BLOCK - CHUNK)
    n_blocks = -(-max_rows // EXPERT_BLOCK)
    scatter_rows, gather_rows, tail_start, tail_chunks, block_expert, n_used = _layout_tables(counts, n_blocks)
    xs = _dispatch_call((scatter_rows, tail_start, tail_chunks), h2, slots, n_blocks)
    ys = _expert_call(block_expert, n_used, xs, w_exp_gate[0], w_exp_up[0], w_exp_down[0], n_blocks)
    yp = _combine_call(gather_rows, ys, slots_t, x1, mod_ctx, False, g_post2, 0,
                       x_prompt.shape[0], x_prompt.shape[1])
    ysamp = _combine_call(gather_rows, ys, slots_t, x1, mod_lat, True, g_post2, n_prompt,
                          x_sample.shape[0], x_sample.shape[1])
    return yp, ysamp, kp[:, None], vp[:, None]
```

```python
import functools
import math

import numpy as np
import jax
import jax.numpy as jnp
from jax import lax
from jax.experimental import pallas as pl
from jax.experimental.pallas import tpu as pltpu

D_MODEL = 1024
GRID_W = 64
N_HEADS = 4
D_QK = 64
D_V = 128
ATTN_W = N_HEADS * D_V
CONV_W = D_MODEL - ATTN_W
IN_W = 3 * ATTN_W + 3 * CONV_W
N_GROUPS = 4
EXP_PER_GROUP = 4
N_EXPERTS = N_GROUPS * EXP_PER_GROUP
D_EXPERT = 512
ROPE_BASE = 10000.0
EPS = 1e-6
LAM_INIT = 0.8 - 0.6 * math.exp(-0.3 * 0)

LANES = 128
F32_SUBLANES = 8
BF16_SUBLANES = 16
MOD_ROWS = 16
TOKEN_TILE = 1024
OUT_SUB_ROWS = 256
Q_TILE = 2048
KEY_CHUNK = 256
MERGE_HEADS_MAX_SEQ = 512
DISPATCH_TILE = 512
CHUNK = BF16_SUBLANES
TILE_SLOTS = 1280
N_CHUNKS = TILE_SLOTS // CHUNK
SLOT_GROUP = 256
COMBINE_SUB_ROWS = 256
EXPERT_BLOCK = 1024
XS_W = D_MODEL + LANES
QK_SCALE = (1.0 / math.sqrt(D_QK)) * math.log2(math.e)

NT_DIMS = (((1,), (1,)), ((), ()))


def _rms(x):
    return x * lax.rsqrt(jnp.mean(x * x, axis=-1, keepdims=True) + EPS)


def _silu(x):
    return x * (1.0 / (1.0 + jnp.exp(-x)))


def _mod_kernel(cond_ref, w_ref, b_ref, lq1_ref, lk1_ref, lq2_ref, lk2_ref, mod_ref, lam_ref):
    s = _silu(cond_ref[...])
    m = lax.dot_general(s, w_ref[...], (((1,), (0,)), ((), ())),
                        precision=lax.Precision.HIGHEST, preferred_element_type=jnp.float32)
    mod_ref[...] = m + b_ref[...]
    a = jnp.sum(lq1_ref[...] * lk1_ref[...], axis=-1, keepdims=True)
    b = jnp.sum(lq2_ref[...] * lk2_ref[...], axis=-1, keepdims=True)
    lam_ref[...] = jnp.broadcast_to(jnp.exp(a) - jnp.exp(b) + LAM_INIT, lam_ref.shape)


def _mod_call(cond, w_mod, b_mod, lq1, lk1, lq2, lk2):
    n_col = 6 * D_MODEL
    col_tile = 1536
    small = pl.BlockSpec((1, D_QK), lambda j: (0, 0))
    return pl.pallas_call(
        _mod_kernel,
        grid=(n_col // col_tile,),
        in_specs=[
            pl.BlockSpec((MOD_ROWS, D_MODEL), lambda j: (0, 0)),
            pl.BlockSpec((D_MODEL, col_tile), lambda j: (0, j)),
            pl.BlockSpec((1, col_tile), lambda j: (0, j)),
            small, small, small, small,
        ],
        out_specs=[
            pl.BlockSpec((MOD_ROWS, col_tile), lambda j: (0, j)),
            pl.BlockSpec((1, LANES), lambda j: (0, 0)),
        ],
        out_shape=[
            jax.ShapeDtypeStruct((MOD_ROWS, n_col), jnp.float32),
            jax.ShapeDtypeStruct((1, LANES), jnp.float32),
        ],
        name="mod",
    )(cond, w_mod, b_mod, lq1, lk1, lq2, lk2)


def _in_proj_kernel(rope, n_tiles, x_ref, xp_ref, xn_ref, shift_ref, scale_ref, g_ref, w_ref, cw_ref, *rest):
    if rope:
        cos_ref, sina_ref, sinb_ref, q_ref, k_ref, v_ref, yc_ref = rest
    else:
        q_ref, k_ref, v_ref, yc_ref = rest
    i = pl.program_id(1)
    tm = x_ref.shape[0]
    gain = g_ref[...] * (1.0 + scale_ref[...])
    shift = shift_ref[...]

    def modulate(x):
        return (_rms(x) * gain + shift).astype(jnp.bfloat16)

    h = modulate(x_ref[...])
    h_halo = modulate(jnp.concatenate([xp_ref[...], xn_ref[...]], axis=0))

    def proj(lhs, lo, hi):
        return jnp.dot(lhs, w_ref[:, lo:hi], preferred_element_type=jnp.float32)

    def rot(t):
        return (t * cos_ref[...] + pltpu.roll(t, LANES - 16, axis=1) * sina_ref[...]
                + pltpu.roll(t, 16, axis=1) * sinb_ref[...])

    c0 = 3 * ATTN_W
    h_ext = jnp.concatenate([h, h_halo], axis=0)
    cu_all = proj(h_ext, c0 + CONV_W, c0 + 2 * CONV_W) * proj(h_ext, c0 + 2 * CONV_W, c0 + 3 * CONV_W)
    gb = proj(h, c0, c0 + CONV_W)
    cu = cu_all[0:tm]
    prev_row = jnp.where(i > 0, cu_all[tm + 7:tm + 8], 0.0)
    next_row = jnp.where(i < n_tiles - 1, cu_all[tm + 8:tm + 9], 0.0)
    row = lax.broadcasted_iota(jnp.int32, cu.shape, 0)
    prev = jnp.where(row == 0, prev_row, pltpu.roll(cu, 1, axis=0))
    nxt = jnp.where(row == tm - 1, next_row, pltpu.roll(cu, tm - 1, axis=0))
    conv = cw_ref[0:1, :] * prev + cw_ref[1:2, :] * cu + cw_ref[2:3, :] * nxt
    yc_ref[...] = (gb * conv).astype(yc_ref.dtype)

    zq = proj(h, 0, ATTN_W)
    for hd in range(N_HEADS):
        q = zq[:, hd * D_V:(hd + 1) * D_V]
        q_ref[hd] = ((rot(q) if rope else q) * QK_SCALE).astype(q_ref.dtype)
    zk = proj(h, ATTN_W, 2 * ATTN_W)
    for hd in range(N_HEADS):
        k = zk[:, hd * D_V:(hd + 1) * D_V]
        k_ref[hd] = (rot(k) if rope else k).astype(k_ref.dtype)
    zv = proj(h, 2 * ATTN_W, 3 * ATTN_W)
    for hd in range(N_HEADS):
        v_ref[hd] = zv[:, hd * D_V:(hd + 1) * D_V].astype(v_ref.dtype)


def _rope_tables(seq):
    n_rows = seq // GRID_W
    row = np.repeat(np.arange(n_rows), GRID_W).astype(np.float64)
    col = np.tile(np.arange(GRID_W), n_rows).astype(np.float64)
    nf = D_QK // 4
    inv = ROPE_BASE ** (-np.arange(nf, dtype=np.float64) / nf)
    ar = row[:, None] * inv
    ac = col[:, None] * inv
    ang = np.concatenate([ar, ar, ac, ac], axis=-1)
    ang = np.concatenate([ang, ang], axis=-1)
    first_half = (np.arange(LANES) % 32) < 16
    cos = np.cos(ang)
    sin = np.sin(ang)
    sina = np.where(first_half, -sin, 0.0)
    sinb = np.where(first_half, 0.0, sin)
    return tuple(jnp.asarray(t, dtype=jnp.float32) for t in (cos, sina, sinb))


def _in_proj_call(x, mod, mod_per_batch, g_pre1, w_in, conv_w, rope, kv_dtype):
    bsz, seq, _ = x.shape
    tm = min(TOKEN_TILE, seq)
    n_tiles = seq // tm
    halo = tm // F32_SUBLANES
    n_halo = seq // F32_SUBLANES

    def mod_spec(chunk):
        if mod_per_batch:
            return pl.BlockSpec((None, 1, D_MODEL), lambda b, i: (b, 0, chunk))
        return pl.BlockSpec((None, 1, D_MODEL), lambda b, i: (0, 0, chunk))

    in_specs = [
        pl.BlockSpec((None, tm, D_MODEL), lambda b, i: (b, i, 0)),
        pl.BlockSpec((None, F32_SUBLANES, D_MODEL), lambda b, i: (b, jnp.maximum(i * halo - 1, 0), 0)),
        pl.BlockSpec((None, F32_SUBLANES, D_MODEL), lambda b, i: (b, jnp.minimum((i + 1) * halo, n_halo - 1), 0)),
        mod_spec(0), mod_spec(1),
        pl.BlockSpec((1, D_MODEL), lambda b, i: (0, 0)),
        pl.BlockSpec((D_MODEL, IN_W), lambda b, i: (0, 0)),
        pl.BlockSpec((3, CONV_W), lambda b, i: (0, 0)),
    ]
    args = [x, x, x, mod, mod, g_pre1, w_in, conv_w]
    if rope:
        in_specs += [pl.BlockSpec((tm, LANES), lambda b, i: (i, 0))] * 3
        args += list(_rope_tables(seq))
    head_spec = pl.BlockSpec((None, N_HEADS, tm, D_V), lambda b, i: (b, 0, i, 0))
    return pl.pallas_call(
        functools.partial(_in_proj_kernel, rope, n_tiles),
        grid=(bsz, n_tiles),
        in_specs=in_specs,
        out_specs=[head_spec, head_spec, head_spec, pl.BlockSpec((None, tm, CONV_W), lambda b, i: (b, i, 0))],
        out_shape=[
            jax.ShapeDtypeStruct((bsz, N_HEADS, seq, D_V), jnp.bfloat16),
            jax.ShapeDtypeStruct((bsz, N_HEADS, seq, D_V), kv_dtype),
            jax.ShapeDtypeStruct((bsz, N_HEADS, seq, D_V), kv_dtype),
            jax.ShapeDtypeStruct((bsz, seq, CONV_W), jnp.bfloat16),
        ],
        name="in_proj_rope" if rope else "in_proj",
    )(*args)


def _attn_kernel(n_kv, lam_ref, g_ref, q_ref, *rest):
    kv_refs = rest[:2 * n_kv]
    o_ref = rest[2 * n_kv]
    heads, tq, _ = q_ref.shape
    lane = lax.broadcasted_iota(jnp.int32, (tq, 2 * D_QK), 1)
    for hh in range(heads):
        q = q_ref[hh]
        zero = jnp.zeros_like(q)
        halves = (jnp.where(lane < D_QK, q, zero), jnp.where(lane >= D_QK, q, zero))
        acc = [jnp.zeros((tq, 2 * D_V), jnp.float32) for _ in range(2)]
        m = [jnp.full((tq, 1), -1e30, jnp.float32) for _ in range(2)]
        for j in range(n_kv):
            k_ref, v_ref = kv_refs[2 * j], kv_refs[2 * j + 1]
            n_keys = k_ref.shape[1]
            ck = min(KEY_CHUNK, n_keys)
            ones = jnp.ones((ck, D_V), jnp.bfloat16)
            for c in range(n_keys // ck):
                k = k_ref[hh, c * ck:(c + 1) * ck, :].astype(jnp.bfloat16)
                v1 = jnp.concatenate([v_ref[hh, c * ck:(c + 1) * ck, :].astype(jnp.bfloat16), ones], axis=1)
                for x, qh in enumerate(halves):
                    s = lax.dot_general(qh, k, NT_DIMS, preferred_element_type=jnp.float32)
                    m_new = jnp.maximum(m[x], jnp.max(s, axis=-1, keepdims=True))
                    p = jnp.exp2(s - m_new).astype(jnp.bfloat16)
                    acc[x] = (jnp.exp2(m[x] - m_new) * acc[x]
                              + jnp.dot(p, v1, preferred_element_type=jnp.float32))
                    m[x] = m_new
        o = (acc[0][:, 0:D_V] / acc[0][:, D_V:2 * D_V]
             - lam_ref[0:1, 0:1] * (acc[1][:, 0:D_V] / acc[1][:, D_V:2 * D_V]))
        o_ref[:, hh * D_V:(hh + 1) * D_V] = (_rms(o) * (g_ref[...] * (1.0 - LAM_INIT))).astype(o_ref.dtype)


def _attn_call(lam, subln_g, q, kvs, heads_per_step):
    bsz, _, seq, _ = q.shape
    tq = min(Q_TILE, seq)
    hp = heads_per_step
    in_specs = [
        pl.BlockSpec((1, LANES), lambda b, h, i: (0, 0)),
        pl.BlockSpec((1, D_V), lambda b, h, i: (0, 0)),
        pl.BlockSpec((None, hp, tq, D_V), lambda b, h, i: (b, h, i, 0)),
    ]
    args = [lam, subln_g, q]
    for k, v, spec_fn in kvs:
        in_specs += [spec_fn(hp), spec_fn(hp)]
        args += [k, v]
    return pl.pallas_call(
        functools.partial(_attn_kernel, len(kvs)),
        grid=(bsz, N_HEADS // hp, seq // tq),
        in_specs=in_specs,
        out_specs=pl.BlockSpec((None, tq, hp * D_V), lambda b, h, i: (b, i, h)),
        out_shape=jax.ShapeDtypeStruct((bsz, seq, ATTN_W), jnp.bfloat16),
        name="attn%d" % len(kvs),
    )(*args)


def _route(logits_t):
    lg = [logits_t[g:g + 1, :] for g in range(N_GROUPS)]
    mg = functools.reduce(jnp.maximum, lg)
    p_sel = 1.0 / functools.reduce(jnp.add, [jnp.exp(t - mg) for t in lg])
    g_sel = jnp.full(mg.shape, N_GROUPS - 1, jnp.int32)
    for g in range(N_GROUPS - 2, -1, -1):
        g_sel = jnp.where(lg[g] == mg, g, g_sel)
    le = []
    for j in range(EXP_PER_GROUP):
        t = jnp.zeros_like(mg)
        for g in range(N_GROUPS):
            r = 8 + g * EXP_PER_GROUP + j
            t = jnp.where(g_sel == g, logits_t[r:r + 1, :], t)
        le.append(t)
    m1 = functools.reduce(jnp.maximum, le)
    i1 = jnp.full(mg.shape, EXP_PER_GROUP - 1, jnp.int32)
    for j in range(EXP_PER_GROUP - 2, -1, -1):
        i1 = jnp.where(le[j] == m1, j, i1)
    neg = jnp.float32(-jnp.inf)
    rest = [jnp.where(i1 == j, neg, le[j]) for j in range(EXP_PER_GROUP)]
    m2 = functools.reduce(jnp.maximum, rest)
    i2 = jnp.full(mg.shape, EXP_PER_GROUP - 1, jnp.int32)
    for j in range(EXP_PER_GROUP - 2, -1, -1):
        i2 = jnp.where(rest[j] == m2, j, i2)
    e2 = jnp.exp(m2 - m1)
    w1 = p_sel / (1.0 + e2)
    w2 = p_sel * e2 / (1.0 + e2)
    base = g_sel * EXP_PER_GROUP
    return (base + i1).astype(jnp.float32), (base + i2).astype(jnp.float32), w1, w2


def _out_proj_kernel(o_ref, yc_ref, wo_ref, x_ref, gate1_ref, shift2_ref, scale2_ref, gpost_ref, gpre_ref,
                     wr_ref, *rest):
    x1_ref, h2_ref, slots_ref, slots_t_ref, counts_ref = rest[-5:]
    tm = o_ref.shape[0]
    sub = min(OUT_SUB_ROWS, tm)
    gain1 = gate1_ref[...] * gpost_ref[...]
    gain2 = gpre_ref[...] * (1.0 + scale2_ref[...])
    routes = []
    for r in range(tm // sub):
        rows = slice(r * sub, (r + 1) * sub)
        out = (jnp.dot(o_ref[rows, :], wo_ref[0:ATTN_W, :], preferred_element_type=jnp.float32)
               + jnp.dot(yc_ref[rows, :], wo_ref[ATTN_W:D_MODEL, :], preferred_element_type=jnp.float32))
        x1 = x_ref[rows, :] + _rms(out) * gain1
        x1_ref[rows, :] = x1
        h2 = (_rms(x1) * gain2 + shift2_ref[...]).astype(jnp.bfloat16)
        h2_ref[rows, :] = h2
        logits = jnp.dot(h2, wr_ref[...], preferred_element_type=jnp.float32)
        routes.append(_route(logits.T))
    e1, e2, w1, w2 = (jnp.concatenate(parts, axis=1) for parts in zip(*routes))
    _plan(e1, e2, w1, w2, slots_ref, slots_t_ref, counts_ref)


def _out_proj_call(o, yc, w_o, x, mod, mod_per_batch, g_post1, g_pre2, w_router,
                   total_tokens, token_offset, carried):
    bsz, seq, _ = x.shape
    tm = DISPATCH_TILE
    assert seq % tm == 0 and token_offset % tm == 0
    n_tiles = seq // tm
    tile0 = token_offset // tm
    n_all = total_tokens // tm

    def mod_spec(chunk):
        if mod_per_batch:
            return pl.BlockSpec((None, 1, D_MODEL), lambda b, i: (b, 0, chunk))
        return pl.BlockSpec((None, 1, D_MODEL), lambda b, i: (0, 0, chunk))

    def vec_spec():
        return pl.BlockSpec((1, D_MODEL), lambda b, i: (0, 0))

    tok = lambda w: pl.BlockSpec((None, tm, w), lambda b, i: (b, i, 0))
    flat = lambda w: pl.BlockSpec((tm, w), lambda b, i: (tile0 + b * n_tiles + i, 0))
    per_tile = lambda r, c: pl.BlockSpec((None, r, c), lambda b, i: (tile0 + b * n_tiles + i, 0, 0))
    in_specs = [
        tok(ATTN_W), tok(CONV_W),
        pl.BlockSpec((D_MODEL, D_MODEL), lambda b, i: (0, 0)),
        tok(D_MODEL),
        mod_spec(2), mod_spec(3), mod_spec(4),
        vec_spec(), vec_spec(),
        pl.BlockSpec((D_MODEL, LANES), lambda b, i: (0, 0)),
    ]
    args = [o, yc, w_o, x, mod, mod, mod, g_post1, g_pre2, w_router]
    aliases = {}
    if carried is not None:
        for j, arr in enumerate(carried):
            aliases[len(args)] = j
            in_specs.append(pl.BlockSpec(memory_space=pl.ANY))
            args.append(arr)
    return pl.pallas_call(
        _out_proj_kernel,
        grid=(bsz, n_tiles),
        in_specs=in_specs,
        out_specs=[flat(D_MODEL), flat(D_MODEL), per_tile(8, tm), per_tile(tm, LANES), per_tile(N_EXPERTS, LANES)],
        out_shape=[
            jax.ShapeDtypeStruct((total_tokens, D_MODEL), jnp.float32),
            jax.ShapeDtypeStruct((total_tokens, D_MODEL), jnp.bfloat16),
            jax.ShapeDtypeStruct((n_all, 8, tm), jnp.float32),
            jax.ShapeDtypeStruct((n_all, tm, LANES), jnp.float32),
            jax.ShapeDtypeStruct((n_all, N_EXPERTS, LANES), jnp.float32),
        ],
        input_output_aliases=aliases,
        name="out_proj",
    )(*args)


def _plan(e1, e2, w1, w2, slots_ref, slots_t_ref, counts_ref):
    t = e1.shape[1]
    e1 = e1.astype(jnp.int32)
    e2 = e2.astype(jnp.int32)
    eid = lax.broadcasted_iota(jnp.int32, (N_EXPERTS, t), 0)
    hot1 = jnp.where(eid == e1, 1.0, 0.0)
    hot2 = jnp.where(eid == e2, 1.0, 0.0)
    hot = jnp.concatenate([hot1, hot2], axis=0).astype(jnp.bfloat16)
    before = (lax.broadcasted_iota(jnp.int32, (t, t), 0) < lax.broadcasted_iota(jnp.int32, (t, t), 1))
    before = jnp.where(before, 1.0, 0.0).astype(jnp.bfloat16)
    rank = jnp.dot(hot, before, preferred_element_type=jnp.float32)
    n1 = jnp.sum(hot1, axis=1, keepdims=True)
    n2 = jnp.sum(hot2, axis=1, keepdims=True)
    ones = jnp.ones((8, t), jnp.bfloat16)
    cnt_row = lax.dot_general(ones, (hot1 + hot2).astype(jnp.bfloat16), NT_DIMS,
                              preferred_element_type=jnp.float32)[0:1, :]
    pad_row = jnp.floor((cnt_row + (CHUNK - 1)) * (1.0 / CHUNK)) * CHUNK
    lower = (lax.broadcasted_iota(jnp.int32, (N_EXPERTS, N_EXPERTS), 1)
             < lax.broadcasted_iota(jnp.int32, (N_EXPERTS, N_EXPERTS), 0))
    base = jnp.sum(jnp.where(lower, pad_row, 0.0), axis=1, keepdims=True)
    slot1 = jnp.sum(hot1 * (base + rank[:N_EXPERTS]), axis=0, keepdims=True)
    slot2 = jnp.sum(hot2 * (base + n1 + rank[N_EXPERTS:]), axis=0, keepdims=True)
    slots_ref[...] = jnp.concatenate([slot1, slot2, w1, w2, jnp.zeros((4, t), jnp.float32)], axis=0)
    wide = jnp.concatenate([slot1, slot2, jnp.zeros((LANES - 2, t), jnp.float32)], axis=0)
    slots_t_ref[...] = wide.T
    counts_ref[...] = jnp.broadcast_to(n1 + n2, counts_ref.shape)


def _layout_tables(counts, n_blocks):
    cnt = counts[:, :, 0].astype(jnp.int32)
    n_tiles = cnt.shape[0]
    nch = (cnt + (CHUNK - 1)) // CHUNK
    csum = jnp.cumsum(nch, axis=0)
    rows = CHUNK * csum[-1]
    blocks = (rows + (EXPERT_BLOCK - 1)) // EXPERT_BLOCK
    blk_end = jnp.cumsum(blocks)
    blk_start = blk_end - blocks
    off = blk_start[None, :] * EXPERT_BLOCK + CHUNK * (csum - nch)
    first = jnp.cumsum(nch, axis=1) - nch
    used = jnp.sum(nch, axis=1)
    c = jnp.arange(N_CHUNKS, dtype=jnp.int32)
    e_idx = jnp.sum((c[None, :, None] >= first[:, None, :]).astype(jnp.int32), axis=2) - 1
    hot = e_idx[:, :, None] == jnp.arange(N_EXPERTS, dtype=jnp.int32)[None, None, :]
    pick = lambda a: jnp.sum(jnp.where(hot, a[:, None, :], 0), axis=2)
    row = pick(off) + CHUNK * (c[None, :] - pick(first))
    valid = c[None, :] < used[:, None]
    spare = (n_blocks * EXPERT_BLOCK + (jnp.arange(n_tiles, dtype=jnp.int32) % 2)[:, None] * TILE_SLOTS
             + CHUNK * c[None, :])
    scatter_rows = jnp.where(valid, row, spare)
    gather_rows = jnp.where(valid, row, 0)
    n_used = blk_end[-1]
    bc = jnp.minimum(jnp.arange(n_blocks, dtype=jnp.int32), n_used - 1)
    block_expert = jnp.sum((bc[:, None] >= blk_end[None, :]).astype(jnp.int32), axis=1)
    tail_start = blk_start * EXPERT_BLOCK + rows
    tail_chunks = (blocks * EXPERT_BLOCK - rows) // CHUNK
    flat = lambda a: a.reshape(-1).astype(jnp.int32)
    return (flat(scatter_rows), flat(gather_rows), flat(tail_start), flat(tail_chunks),
            flat(block_expert), flat(n_used))


def _chunk_copies(tile, rows_ref, make_copy, action):
    for c in range(N_CHUNKS):
        action(make_copy(c * CHUNK, pl.multiple_of(rows_ref[tile * N_CHUNKS + c], CHUNK)))


def _wait_chunk_copies(make_copy):
    for _ in range(N_CHUNKS):
        make_copy(0, 0).wait()


def _dispatch_kernel(rows_ref, tail_start_ref, tail_chunks_ref, h_ref, slots_ref, xs_ref, buf, zbuf, sem, zsem):
    i = pl.program_id(0)
    n = pl.num_programs(0)
    slot = i % 2
    s = slots_ref[...]
    t = s.shape[1]
    s1 = s[0:1, :].astype(jnp.int32)
    s2 = s[1:2, :].astype(jnp.int32)
    h = h_ref[...]
    lane = lax.broadcasted_iota(jnp.int32, (SLOT_GROUP, LANES), 1)
    for g in range(TILE_SLOTS // SLOT_GROUP):
        grp = slice(g * SLOT_GROUP, (g + 1) * SLOT_GROUP)
        rid = lax.broadcasted_iota(jnp.int32, (SLOT_GROUP, t), 0) + g * SLOT_GROUP
        m1 = rid == s1
        m2 = rid == s2
        perm = jnp.where(m1, 1.0, jnp.where(m2, 1.0, 0.0)).astype(jnp.bfloat16)
        rows = jnp.dot(perm, h, preferred_element_type=jnp.float32)
        w = jnp.sum(jnp.where(m1, s[2:3, :], 0.0) + jnp.where(m2, s[3:4, :], 0.0), axis=1, keepdims=True)
        hi = w.astype(jnp.bfloat16).astype(jnp.float32)
        aux = jnp.where(lane == 0, hi, jnp.where(lane == 1, w - hi, 0.0))
        buf[slot, grp, 0:D_MODEL] = rows.astype(jnp.bfloat16)
        buf[slot, grp, D_MODEL:XS_W] = aux.astype(jnp.bfloat16)

    def copies(sl):
        def make(src_row, dst_row):
            return pltpu.make_async_copy(buf.at[sl, pl.ds(src_row, CHUNK), :],
                                         xs_ref.at[pl.ds(dst_row, CHUNK), :], sem.at[sl])
        return make

    _chunk_copies(i, rows_ref, copies(slot), lambda c: c.start())

    @pl.when(i > 0)
    def _():
        _wait_chunk_copies(copies(1 - slot))

    @pl.when(i == n - 1)
    def _():
        zbuf[...] = jnp.zeros_like(zbuf)

        def tails(action):
            for e in range(N_EXPERTS):
                start = tail_start_ref[e]

                def body(m, carry, start=start):
                    action(pltpu.make_async_copy(
                        zbuf, xs_ref.at[pl.ds(pl.multiple_of(start + m * CHUNK, CHUNK), CHUNK), :], zsem))
                    return carry

                lax.fori_loop(0, tail_chunks_ref[e], body, 0)

        tails(lambda c: c.start())
        tails(lambda c: c.wait())
        _wait_chunk_copies(copies(slot))


def _dispatch_call(tables, h2, slots, n_blocks):
    scatter_rows, tail_start, tail_chunks = tables
    tokens = h2.shape[0]
    t = DISPATCH_TILE
    return pl.pallas_call(
        _dispatch_kernel,
        grid_spec=pltpu.PrefetchScalarGridSpec(
            num_scalar_prefetch=3,
            grid=(tokens // t,),
            in_specs=[
                pl.BlockSpec((t, D_MODEL), lambda i, *_: (i, 0)),
                pl.BlockSpec((None, 8, t), lambda i, *_: (i, 0, 0)),
            ],
            out_specs=pl.BlockSpec(memory_space=pl.ANY),
            scratch_shapes=[
                pltpu.VMEM((2, TILE_SLOTS, XS_W), jnp.bfloat16),
                pltpu.VMEM((CHUNK, XS_W), jnp.bfloat16),
                pltpu.SemaphoreType.DMA((2,)),
                pltpu.SemaphoreType.DMA(()),
            ],
        ),
        out_shape=jax.ShapeDtypeStruct((n_blocks * EXPERT_BLOCK + 2 * TILE_SLOTS, XS_W), jnp.bfloat16),
        name="moe_dispatch",
    )(scatter_rows, tail_start, tail_chunks, h2, slots)


def _expert_kernel(bexp_ref, nused_ref, xs_ref, wg_ref, wu_ref, wd_ref, ys_ref, wgu_b, wd_b):
    b = pl.program_id(0)
    e = bexp_ref[b]
    prev = bexp_ref[jnp.maximum(b - 1, 0)]

    @pl.when((b == 0) | (e != prev))
    def _():
        wgu_b[:, 0:D_EXPERT] = wg_ref[...].astype(jnp.bfloat16)
        wgu_b[:, D_EXPERT:2 * D_EXPERT] = wu_ref[...].astype(jnp.bfloat16)
        wd_b[...] = wd_ref[...].astype(jnp.bfloat16)

    @pl.when(b < nused_ref[0])
    def _():
        x = xs_ref[:, 0:D_MODEL]
        y = None
        half = D_EXPERT // 2
        for c in range(2):
            cols = slice(c * half, (c + 1) * half)
            g = jnp.dot(x, wgu_b[:, c * half:(c + 1) * half], preferred_element_type=jnp.float32)
            u = jnp.dot(x, wgu_b[:, D_EXPERT + c * half:D_EXPERT + (c + 1) * half],
                        preferred_element_type=jnp.float32)
            act = (_silu(g) * u).astype(jnp.bfloat16)
            t = jnp.dot(act, wd_b[cols, :], preferred_element_type=jnp.float32)
            y = t if y is None else y + t
        w = (xs_ref[:, D_MODEL:D_MODEL + 1].astype(jnp.float32)
             + xs_ref[:, D_MODEL + 1:D_MODEL + 2].astype(jnp.float32))
        ys_ref[...] = (w * y).astype(jnp.bfloat16)


def _expert_call(block_expert, n_used, xs, w_eg, w_eu, w_ed, n_blocks):
    row_blk = lambda w: pl.BlockSpec((EXPERT_BLOCK, w), lambda b, be, nu: (jnp.minimum(b, nu[0] - 1), 0))
    return pl.pallas_call(
        _expert_kernel,
        grid_spec=pltpu.PrefetchScalarGridSpec(
            num_scalar_prefetch=2,
            grid=(n_blocks,),
            in_specs=[
                row_blk(XS_W),
                pl.BlockSpec((None, D_MODEL, D_EXPERT), lambda b, be, nu: (be[b], 0, 0)),
                pl.BlockSpec((None, D_MODEL, D_EXPERT), lambda b, be, nu: (be[b], 0, 0)),
                pl.BlockSpec((None, D_EXPERT, D_MODEL), lambda b, be, nu: (be[b], 0, 0)),
            ],
            out_specs=row_blk(D_MODEL),
            scratch_shapes=[
                pltpu.VMEM((D_MODEL, 2 * D_EXPERT), jnp.bfloat16),
                pltpu.VMEM((D_EXPERT, D_MODEL), jnp.bfloat16),
            ],
        ),
        out_shape=jax.ShapeDtypeStruct((n_blocks * EXPERT_BLOCK, D_MODEL), jnp.bfloat16),
        name="moe_experts",
    )(block_expert, n_used, xs, w_eg, w_eu, w_ed)


def _combine_kernel(tile0, rows_ref, ys_ref, slots_t_ref, x1_ref, gate2_ref, gpost_ref, out_ref, buf, sem):
    j = pl.program_id(0)
    n = pl.num_programs(0)
    slot = j % 2
    tile = tile0 + j

    def copies(sl):
        def make(dst_row, src_row):
            return pltpu.make_async_copy(ys_ref.at[pl.ds(src_row, CHUNK), :],
                                         buf.at[sl, pl.ds(dst_row, CHUNK), :], sem.at[sl])
        return make

    @pl.when(j == 0)
    def _():
        _chunk_copies(tile, rows_ref, copies(slot), lambda c: c.start())

    @pl.when(j + 1 < n)
    def _():
        _chunk_copies(tile + 1, rows_ref, copies(1 - slot), lambda c: c.start())

    _wait_chunk_copies(copies(slot))
    gain = gate2_ref[...] * gpost_ref[...]
    sorted_rows = buf[slot]
    lane = lax.broadcasted_iota(jnp.int32, (COMBINE_SUB_ROWS, TILE_SLOTS), 1)
    for r in range(out_ref.shape[0] // COMBINE_SUB_ROWS):
        rows = slice(r * COMBINE_SUB_ROWS, (r + 1) * COMBINE_SUB_ROWS)
        s1 = slots_t_ref[rows, 0:1].astype(jnp.int32)
        s2 = slots_t_ref[rows, 1:2].astype(jnp.int32)
        unperm = jnp.where(lane == s1, 1.0, jnp.where(lane == s2, 1.0, 0.0)).astype(jnp.bfloat16)
        y = jnp.dot(unperm, sorted_rows, preferred_element_type=jnp.float32)
        out_ref[rows, :] = x1_ref[rows, :] + _rms(y) * gain


def _combine_call(gather_rows, ys, slots_t, x1, mod, mod_per_batch, g_post2, token_offset, bsz, seq):
    t = DISPATCH_TILE
    tile0 = token_offset // t
    tokens = bsz * seq
    if mod_per_batch:
        per = seq // t
        gate2_spec = pl.BlockSpec((None, 1, D_MODEL), lambda j, *_: (j // per, 0, 5))
    else:
        gate2_spec = pl.BlockSpec((None, 1, D_MODEL), lambda j, *_: (0, 0, 5))
    y = pl.pallas_call(
        functools.partial(_combine_kernel, tile0),
        grid_spec=pltpu.PrefetchScalarGridSpec(
            num_scalar_prefetch=1,
            grid=(tokens // t,),
            in_specs=[
                pl.BlockSpec(memory_space=pl.ANY),
                pl.BlockSpec((None, t, LANES), lambda j, *_: (tile0 + j, 0, 0)),
                pl.BlockSpec((t, D_MODEL), lambda j, *_: (tile0 + j, 0)),
                gate2_spec,
                pl.BlockSpec((1, D_MODEL), lambda j, *_: (0, 0)),
            ],
            out_specs=pl.BlockSpec((t, D_MODEL), lambda j, *_: (j, 0)),
            scratch_shapes=[
                pltpu.VMEM((2, TILE_SLOTS, D_MODEL), jnp.bfloat16),
                pltpu.SemaphoreType.DMA((2,)),
            ],
        ),
        out_shape=jax.ShapeDtypeStruct((tokens, D_MODEL), jnp.float32),
        name="moe_combine",
    )(gather_rows, ys, slots_t, x1, mod, g_post2)
    return y.reshape(bsz, seq, D_MODEL)


def kernel(x_prompt, x_sample, cache_k, cache_v, c, c_ctx, w_mod, b_mod, g_pre1, g_post1, g_pre2, g_post2,
           w_in, conv_w, lambda_q1, lambda_k1, lambda_q2, lambda_k2, subln_g, w_o, w_router_group,
           w_router_expert, w_exp_gate, w_exp_up, w_exp_down):
    n_lat = c.shape[0]
    cond = jnp.concatenate(
        [c, c_ctx[None, :], jnp.zeros((MOD_ROWS - n_lat - 1, D_MODEL), jnp.float32)], axis=0)
    mod, lam = _mod_call(cond, w_mod[0], b_mod, lambda_q1, lambda_k1, lambda_q2, lambda_k2)
    mod = mod.reshape(MOD_ROWS, 1, 6 * D_MODEL)
    mod_lat, mod_ctx = mod[:n_lat], mod[n_lat:n_lat + 1]

    w_in_b = w_in[0].astype(jnp.bfloat16)
    w_o_b = w_o[0].astype(jnp.bfloat16)
    w_router = jnp.concatenate(
        [w_router_group[0], jnp.zeros((D_MODEL, 8 - N_GROUPS), jnp.float32), w_router_expert[0],
         jnp.zeros((D_MODEL, LANES - 8 - N_EXPERTS), jnp.float32)], axis=1).astype(jnp.bfloat16)

    n_prompt = x_prompt.shape[0] * x_prompt.shape[1]
    n_sample = x_sample.shape[0] * x_sample.shape[1]
    total = n_prompt + n_sample

    def mixer(x, mod_x, per_batch, rope, ctx_kv, token_offset, carried):
        kv_dtype = jnp.bfloat16 if rope else jnp.float32
        q, k, v, yc = _in_proj_call(x, mod_x, per_batch, g_pre1, w_in_b, conv_w[0], rope, kv_dtype)
        kvs = []
        if ctx_kv is not None:
            ck, cv = ctx_kv
            n_ctx = ck.shape[3]
            kvs.append((ck, cv, lambda hp: pl.BlockSpec((None, None, hp, n_ctx, D_V),
                                                        lambda b, h, i: (b, 0, h, 0, 0))))
        seq = x.shape[1]
        kvs.append((k, v, lambda hp: pl.BlockSpec((None, hp, seq, D_V), lambda b, h, i: (b, h, 0, 0))))
        o = _attn_call(lam, subln_g, q, kvs, N_HEADS if seq <= MERGE_HEADS_MAX_SEQ else 1)
        if not per_batch:
            o, yc, x = (a.reshape(1, -1, a.shape[-1]) for a in (o, yc, x))
        shared = _out_proj_call(o, yc, w_o_b, x, mod_x, per_batch, g_post1, g_pre2, w_router,
                                total, token_offset, carried)
        return shared, k, v

    shared, kp, vp = mixer(x_prompt, mod_ctx, False, False, None, 0, None)
    (x1, h2, slots, slots_t, counts), _, _ = mixer(x_sample, mod_lat, True, True, (cache_k, cache_v),
                                                   n_prompt, shared)

    n_tiles = total // DISPATCH_TILE
    max_rows = 2 * total + n_tiles * N_EXPERTS * (CHUNK - 1) + N_EXPERTS * (EXPERT_BLOCK - CHUNK)
    n_blocks = -(-max_rows // EXPERT_BLOCK)
    scatter_rows, gather_rows, tail_start, tail_chunks, block_expert, n_used = _layout_tables(counts, n_blocks)
    xs = _dispatch_call((scatter_rows, tail_start, tail_chunks), h2, slots, n_blocks)
    ys = _expert_call(block_expert, n_used, xs, w_exp_gate[0], w_exp_up[0], w_exp_down[0], n_blocks)
    yp = _combine_call(gather_rows, ys, slots_t, x1, mod_ctx, False, g_post2, 0,
                       x_prompt.shape[0], x_prompt.shape[1])
    ysamp = _combine_call(gather_rows, ys, slots_t, x1, mod_lat, True, g_post2, n_prompt,
                          x_sample.shape[0], x_sample.shape[1])
    return yp, ysamp, kp[:, None], vp[:, None]
```

```python
import functools
import math

import numpy as np
import jax
import jax.numpy as jnp
from jax import lax
from jax.experimental import pallas as pl
from jax.experimental.pallas import tpu as pltpu

D_MODEL = 1024
GRID_W = 64
N_HEADS = 4
D_QK = 64
D_V = 128
ATTN_W = N_HEADS * D_V
CONV_W = D_MODEL - ATTN_W
IN_W = 3 * ATTN_W + 3 * CONV_W
N_GROUPS = 4
EXP_PER_GROUP = 4
N_EXPERTS = N_GROUPS * EXP_PER_GROUP
D_EXPERT = 512
ROPE_BASE = 10000.0
EPS = 1e-6
LAM_INIT = 0.8 - 0.6 * math.exp(-0.3 * 0)

LANES = 128
F32_SUBLANES = 8
BF16_SUBLANES = 16
MOD_ROWS = 16
TOKEN_TILE = 1024
OUT_SUB_ROWS = 256
Q_TILE = 2048
KEY_CHUNK = 256
MERGE_HEADS_MAX_SEQ = 512
DISPATCH_TILE = 512
CHUNK = BF16_SUBLANES
TILE_SLOTS = 1280
N_CHUNKS = TILE_SLOTS // CHUNK
SLOT_GROUP = 256
COMBINE_SUB_ROWS = 256
EXPERT_BLOCK = 1024
XS_W = D_MODEL + LANES
QK_SCALE = (1.0 / math.sqrt(D_QK)) * math.log2(math.e)

NT_DIMS = (((1,), (1,)), ((), ()))


def _rms(x):
    return x * lax.rsqrt(jnp.mean(x * x, axis=-1, keepdims=True) + EPS)


def _silu(x):
    return x * (1.0 / (1.0 + jnp.exp(-x)))


def _mod_kernel(cond_ref, w_ref, b_ref, lq1_ref, lk1_ref, lq2_ref, lk2_ref, mod_ref, lam_ref):
    s = _silu(cond_ref[...])
    s_hi = s.astype(jnp.bfloat16)
    s_lo = (s - s_hi.astype(jnp.float32)).astype(jnp.bfloat16)
    w = w_ref[...]
    w_hi = w.astype(jnp.bfloat16)
    w_lo = (w - w_hi.astype(jnp.float32)).astype(jnp.bfloat16)
    both = jnp.dot(jnp.concatenate([s_hi, s_lo], axis=0), w_hi, preferred_element_type=jnp.float32)
    m = both[0:MOD_ROWS] + both[MOD_ROWS:2 * MOD_ROWS] + jnp.dot(s_hi, w_lo, preferred_element_type=jnp.float32)
    mod_ref[...] = m + b_ref[...]
    a = jnp.sum(lq1_ref[...] * lk1_ref[...], axis=-1, keepdims=True)
    b = jnp.sum(lq2_ref[...] * lk2_ref[...], axis=-1, keepdims=True)
    lam_ref[...] = jnp.broadcast_to(jnp.exp(a) - jnp.exp(b) + LAM_INIT, lam_ref.shape)


def _mod_call(cond, w_mod, b_mod, lq1, lk1, lq2, lk2):
    n_col = 6 * D_MODEL
    col_tile = 1536
    small = pl.BlockSpec((1, D_QK), lambda j: (0, 0))
    return pl.pallas_call(
        _mod_kernel,
        grid=(n_col // col_tile,),
        in_specs=[
            pl.BlockSpec((MOD_ROWS, D_MODEL), lambda j: (0, 0)),
            pl.BlockSpec((D_MODEL, col_tile), lambda j: (0, j)),
            pl.BlockSpec((1, col_tile), lambda j: (0, j)),
            small, small, small, small,
        ],
        out_specs=[
            pl.BlockSpec((MOD_ROWS, col_tile), lambda j: (0, j)),
            pl.BlockSpec((1, LANES), lambda j: (0, 0)),
        ],
        out_shape=[
            jax.ShapeDtypeStruct((MOD_ROWS, n_col), jnp.float32),
            jax.ShapeDtypeStruct((1, LANES), jnp.float32),
        ],
        name="mod",
    )(cond, w_mod, b_mod, lq1, lk1, lq2, lk2)


def _in_proj_kernel(rope, n_tiles, x_ref, xp_ref, xn_ref, shift_ref, scale_ref, g_ref, w_ref, cw_ref, *rest):
    if rope:
        cos_ref, sina_ref, sinb_ref, q_ref, k_ref, v_ref, yc_ref = rest
    else:
        q_ref, k_ref, v_ref, yc_ref = rest
    i = pl.program_id(1)
    tm = x_ref.shape[0]
    gain = g_ref[...] * (1.0 + scale_ref[...])
    shift = shift_ref[...]

    def modulate(x):
        return (_rms(x) * gain + shift).astype(jnp.bfloat16)

    h = modulate(x_ref[...])
    h_halo = modulate(jnp.concatenate([xp_ref[...], xn_ref[...]], axis=0))

    def proj(lhs, lo, hi):
        return jnp.dot(lhs, w_ref[:, lo:hi], preferred_element_type=jnp.float32)

    def rot(t):
        return (t * cos_ref[...] + pltpu.roll(t, LANES - 16, axis=1) * sina_ref[...]
                + pltpu.roll(t, 16, axis=1) * sinb_ref[...])

    c0 = 3 * ATTN_W
    h_ext = jnp.concatenate([h, h_halo], axis=0)
    cu_all = proj(h_ext, c0 + CONV_W, c0 + 2 * CONV_W) * proj(h_ext, c0 + 2 * CONV_W, c0 + 3 * CONV_W)
    gb = proj(h, c0, c0 + CONV_W)
    cu = cu_all[0:tm]
    prev_row = jnp.where(i > 0, cu_all[tm + 7:tm + 8], 0.0)
    next_row = jnp.where(i < n_tiles - 1, cu_all[tm + 8:tm + 9], 0.0)
    row = lax.broadcasted_iota(jnp.int32, cu.shape, 0)
    prev = jnp.where(row == 0, prev_row, pltpu.roll(cu, 1, axis=0))
    nxt = jnp.where(row == tm - 1, next_row, pltpu.roll(cu, tm - 1, axis=0))
    conv = cw_ref[0:1, :] * prev + cw_ref[1:2, :] * cu + cw_ref[2:3, :] * nxt
    yc_ref[...] = (gb * conv).astype(yc_ref.dtype)

    zq = proj(h, 0, ATTN_W)
    for hd in range(N_HEADS):
        q = zq[:, hd * D_V:(hd + 1) * D_V]
        q_ref[hd] = ((rot(q) if rope else q) * QK_SCALE).astype(q_ref.dtype)
    zk = proj(h, ATTN_W, 2 * ATTN_W)
    for hd in range(N_HEADS):
        k = zk[:, hd * D_V:(hd + 1) * D_V]
        k_ref[hd] = (rot(k) if rope else k).astype(k_ref.dtype)
    zv = proj(h, 2 * ATTN_W, 3 * ATTN_W)
    for hd in range(N_HEADS):
        v_ref[hd] = zv[:, hd * D_V:(hd + 1) * D_V].astype(v_ref.dtype)


def _rope_tables(seq):
    n_rows = seq // GRID_W
    row = np.repeat(np.arange(n_rows), GRID_W).astype(np.float64)
    col = np.tile(np.arange(GRID_W), n_rows).astype(np.float64)
    nf = D_QK // 4
    inv = ROPE_BASE ** (-np.arange(nf, dtype=np.float64) / nf)
    ar = row[:, None] * inv
    ac = col[:, None] * inv
    ang = np.concatenate([ar, ar, ac, ac], axis=-1)
    ang = np.concatenate([ang, ang], axis=-1)
    first_half = (np.arange(LANES) % 32) < 16
    cos = np.cos(ang)
    sin = np.sin(ang)
    sina = np.where(first_half, -sin, 0.0)
    sinb = np.where(first_half, 0.0, sin)
    return tuple(jnp.asarray(t, dtype=jnp.float32) for t in (cos, sina, sinb))


def _in_proj_call(x, mod, mod_per_batch, g_pre1, w_in, conv_w, rope, kv_dtype):
    bsz, seq, _ = x.shape
    tm = min(TOKEN_TILE, seq)
    n_tiles = seq // tm
    halo = tm // F32_SUBLANES
    n_halo = seq // F32_SUBLANES

    def mod_spec(chunk):
        if mod_per_batch:
            return pl.BlockSpec((None, 1, D_MODEL), lambda b, i: (b, 0, chunk))
        return pl.BlockSpec((None, 1, D_MODEL), lambda b, i: (0, 0, chunk))

    in_specs = [
        pl.BlockSpec((None, tm, D_MODEL), lambda b, i: (b, i, 0)),
        pl.BlockSpec((None, F32_SUBLANES, D_MODEL), lambda b, i: (b, jnp.maximum(i * halo - 1, 0), 0)),
        pl.BlockSpec((None, F32_SUBLANES, D_MODEL), lambda b, i: (b, jnp.minimum((i + 1) * halo, n_halo - 1), 0)),
        mod_spec(0), mod_spec(1),
        pl.BlockSpec((1, D_MODEL), lambda b, i: (0, 0)),
        pl.BlockSpec((D_MODEL, IN_W), lambda b, i: (0, 0)),
        pl.BlockSpec((3, CONV_W), lambda b, i: (0, 0)),
    ]
    args = [x, x, x, mod, mod, g_pre1, w_in, conv_w]
    if rope:
        in_specs += [pl.BlockSpec((tm, LANES), lambda b, i: (i, 0))] * 3
        args += list(_rope_tables(seq))
    head_spec = pl.BlockSpec((None, N_HEADS, tm, D_V), lambda b, i: (b, 0, i, 0))
    return pl.pallas_call(
        functools.partial(_in_proj_kernel, rope, n_tiles),
        grid=(bsz, n_tiles),
        in_specs=in_specs,
        out_specs=[head_spec, head_spec, head_spec, pl.BlockSpec((None, tm, CONV_W), lambda b, i: (b, i, 0))],
        out_shape=[
            jax.ShapeDtypeStruct((bsz, N_HEADS, seq, D_V), jnp.bfloat16),
            jax.ShapeDtypeStruct((bsz, N_HEADS, seq, D_V), kv_dtype),
            jax.ShapeDtypeStruct((bsz, N_HEADS, seq, D_V), kv_dtype),
            jax.ShapeDtypeStruct((bsz, seq, CONV_W), jnp.bfloat16),
        ],
        name="in_proj_rope" if rope else "in_proj",
    )(*args)


def _attn_kernel(n_kv, lam_ref, g_ref, q_ref, *rest):
    kv_refs = rest[:2 * n_kv]
    o_ref = rest[2 * n_kv]
    heads, tq, _ = q_ref.shape
    lane = lax.broadcasted_iota(jnp.int32, (tq, 2 * D_QK), 1)
    for hh in range(heads):
        q = q_ref[hh]
        zero = jnp.zeros_like(q)
        halves = (jnp.where(lane < D_QK, q, zero), jnp.where(lane >= D_QK, q, zero))
        acc = [jnp.zeros((tq, 2 * D_V), jnp.float32) for _ in range(2)]
        m = [jnp.full((tq, 1), -1e30, jnp.float32) for _ in range(2)]
        for j in range(n_kv):
            k_ref, v_ref = kv_refs[2 * j], kv_refs[2 * j + 1]
            n_keys = k_ref.shape[1]
            ck = min(KEY_CHUNK, n_keys)
            ones = jnp.ones((ck, D_V), jnp.bfloat16)
            for c in range(n_keys // ck):
                k = k_ref[hh, c * ck:(c + 1) * ck, :].astype(jnp.bfloat16)
                v1 = jnp.concatenate([v_ref[hh, c * ck:(c + 1) * ck, :].astype(jnp.bfloat16), ones], axis=1)
                for x, qh in enumerate(halves):
                    s = lax.dot_general(qh, k, NT_DIMS, preferred_element_type=jnp.float32)
                    m_new = jnp.maximum(m[x], jnp.max(s, axis=-1, keepdims=True))
                    p = jnp.exp2(s - m_new).astype(jnp.bfloat16)
                    acc[x] = (jnp.exp2(m[x] - m_new) * acc[x]
                              + jnp.dot(p, v1, preferred_element_type=jnp.float32))
                    m[x] = m_new
        o = (acc[0][:, 0:D_V] / acc[0][:, D_V:2 * D_V]
             - lam_ref[0:1, 0:1] * (acc[1][:, 0:D_V] / acc[1][:, D_V:2 * D_V]))
        o_ref[:, hh * D_V:(hh + 1) * D_V] = (_rms(o) * (g_ref[...] * (1.0 - LAM_INIT))).astype(o_ref.dtype)


def _attn_call(lam, subln_g, q, kvs, heads_per_step):
    bsz, _, seq, _ = q.shape
    tq = min(Q_TILE, seq)
    hp = heads_per_step
    in_specs = [
        pl.BlockSpec((1, LANES), lambda b, h, i: (0, 0)),
        pl.BlockSpec((1, D_V), lambda b, h, i: (0, 0)),
        pl.BlockSpec((None, hp, tq, D_V), lambda b, h, i: (b, h, i, 0)),
    ]
    args = [lam, subln_g, q]
    for k, v, spec_fn in kvs:
        in_specs += [spec_fn(hp), spec_fn(hp)]
        args += [k, v]
    return pl.pallas_call(
        functools.partial(_attn_kernel, len(kvs)),
        grid=(bsz, N_HEADS // hp, seq // tq),
        in_specs=in_specs,
        out_specs=pl.BlockSpec((None, tq, hp * D_V), lambda b, h, i: (b, i, h)),
        out_shape=jax.ShapeDtypeStruct((bsz, seq, ATTN_W), jnp.bfloat16),
        name="attn%d" % len(kvs),
    )(*args)


def _route(logits_t):
    lg = [logits_t[g:g + 1, :] for g in range(N_GROUPS)]
    mg = functools.reduce(jnp.maximum, lg)
    p_sel = 1.0 / functools.reduce(jnp.add, [jnp.exp(t - mg) for t in lg])
    g_sel = jnp.full(mg.shape, N_GROUPS - 1, jnp.int32)
    for g in range(N_GROUPS - 2, -1, -1):
        g_sel = jnp.where(lg[g] == mg, g, g_sel)
    le = []
    for j in range(EXP_PER_GROUP):
        t = jnp.zeros_like(mg)
        for g in range(N_GROUPS):
            r = 8 + g * EXP_PER_GROUP + j
            t = jnp.where(g_sel == g, logits_t[r:r + 1, :], t)
        le.append(t)
    m1 = functools.reduce(jnp.maximum, le)
    i1 = jnp.full(mg.shape, EXP_PER_GROUP - 1, jnp.int32)
    for j in range(EXP_PER_GROUP - 2, -1, -1):
        i1 = jnp.where(le[j] == m1, j, i1)
    neg = jnp.float32(-jnp.inf)
    rest = [jnp.where(i1 == j, neg, le[j]) for j in range(EXP_PER_GROUP)]
    m2 = functools.reduce(jnp.maximum, rest)
    i2 = jnp.full(mg.shape, EXP_PER_GROUP - 1, jnp.int32)
    for j in range(EXP_PER_GROUP - 2, -1, -1):
        i2 = jnp.where(rest[j] == m2, j, i2)
    e2 = jnp.exp(m2 - m1)
    w1 = p_sel / (1.0 + e2)
    w2 = p_sel * e2 / (1.0 + e2)
    base = g_sel * EXP_PER_GROUP
    return (base + i1).astype(jnp.float32), (base + i2).astype(jnp.float32), w1, w2


def _out_proj_kernel(o_ref, yc_ref, wo_ref, x_ref, gate1_ref, shift2_ref, scale2_ref, gpost_ref, gpre_ref,
                     wr_ref, *rest):
    x1_ref, h2_ref, slots_ref, slots_t_ref, counts_ref = rest[-5:]
    tm = o_ref.shape[0]
    sub = min(OUT_SUB_ROWS, tm)
    gain1 = gate1_ref[...] * gpost_ref[...]
    gain2 = gpre_ref[...] * (1.0 + scale2_ref[...])
    routes = []
    for r in range(tm // sub):
        rows = slice(r * sub, (r + 1) * sub)
        out = (jnp.dot(o_ref[rows, :], wo_ref[0:ATTN_W, :], preferred_element_type=jnp.float32)
               + jnp.dot(yc_ref[rows, :], wo_ref[ATTN_W:D_MODEL, :], preferred_element_type=jnp.float32))
        x1 = x_ref[rows, :] + _rms(out) * gain1
        x1_ref[rows, :] = x1
        h2 = (_rms(x1) * gain2 + shift2_ref[...]).astype(jnp.bfloat16)
        h2_ref[rows, :] = h2
        logits = jnp.dot(h2, wr_ref[...], preferred_element_type=jnp.float32)
        routes.append(_route(logits.T))
    e1, e2, w1, w2 = (jnp.concatenate(parts, axis=1) for parts in zip(*routes))
    _plan(e1, e2, w1, w2, slots_ref, slots_t_ref, counts_ref)


def _out_proj_call(o, yc, w_o, x, mod, mod_per_batch, g_post1, g_pre2, w_router,
                   total_tokens, token_offset, carried):
    bsz, seq, _ = x.shape
    tm = DISPATCH_TILE
    assert seq % tm == 0 and token_offset % tm == 0
    n_tiles = seq // tm
    tile0 = token_offset // tm
    n_all = total_tokens // tm

    def mod_spec(chunk):
        if mod_per_batch:
            return pl.BlockSpec((None, 1, D_MODEL), lambda b, i: (b, 0, chunk))
        return pl.BlockSpec((None, 1, D_MODEL), lambda b, i: (0, 0, chunk))

    def vec_spec():
        return pl.BlockSpec((1, D_MODEL), lambda b, i: (0, 0))

    tok = lambda w: pl.BlockSpec((None, tm, w), lambda b, i: (b, i, 0))
    flat = lambda w: pl.BlockSpec((tm, w), lambda b, i: (tile0 + b * n_tiles + i, 0))
    per_tile = lambda r, c: pl.BlockSpec((None, r, c), lambda b, i: (tile0 + b * n_tiles + i, 0, 0))
    in_specs = [
        tok(ATTN_W), tok(CONV_W),
        pl.BlockSpec((D_MODEL, D_MODEL), lambda b, i: (0, 0)),
        tok(D_MODEL),
        mod_spec(2), mod_spec(3), mod_spec(4),
        vec_spec(), vec_spec(),
        pl.BlockSpec((D_MODEL, LANES), lambda b, i: (0, 0)),
    ]
    args = [o, yc, w_o, x, mod, mod, mod, g_post1, g_pre2, w_router]
    aliases = {}
    if carried is not None:
        for j, arr in enumerate(carried):
            aliases[len(args)] = j
            in_specs.append(pl.BlockSpec(memory_space=pl.ANY))
            args.append(arr)
    return pl.pallas_call(
        _out_proj_kernel,
        grid=(bsz, n_tiles),
        in_specs=in_specs,
        out_specs=[flat(D_MODEL), flat(D_MODEL), per_tile(8, tm), per_tile(tm, LANES), per_tile(N_EXPERTS, LANES)],
        out_shape=[
            jax.ShapeDtypeStruct((total_tokens, D_MODEL), jnp.float32),
            jax.ShapeDtypeStruct((total_tokens, D_MODEL), jnp.bfloat16),
            jax.ShapeDtypeStruct((n_all, 8, tm), jnp.float32),
            jax.ShapeDtypeStruct((n_all, tm, LANES), jnp.float32),
            jax.ShapeDtypeStruct((n_all, N_EXPERTS, LANES), jnp.float32),
        ],
        input_output_aliases=aliases,
        name="out_proj",
    )(*args)


def _plan(e1, e2, w1, w2, slots_ref, slots_t_ref, counts_ref):
    t = e1.shape[1]
    e1 = e1.astype(jnp.int32)
    e2 = e2.astype(jnp.int32)
    eid = lax.broadcasted_iota(jnp.int32, (N_EXPERTS, t), 0)
    hot1 = jnp.where(eid == e1, 1.0, 0.0)
    hot2 = jnp.where(eid == e2, 1.0, 0.0)
    hot = jnp.concatenate([hot1, hot2], axis=0).astype(jnp.bfloat16)
    before = (lax.broadcasted_iota(jnp.int32, (t, t), 0) < lax.broadcasted_iota(jnp.int32, (t, t), 1))
    before = jnp.where(before, 1.0, 0.0).astype(jnp.bfloat16)
    rank = jnp.dot(hot, before, preferred_element_type=jnp.float32)
    n1 = jnp.sum(hot1, axis=1, keepdims=True)
    n2 = jnp.sum(hot2, axis=1, keepdims=True)
    ones = jnp.ones((8, t), jnp.bfloat16)
    cnt_row = lax.dot_general(ones, (hot1 + hot2).astype(jnp.bfloat16), NT_DIMS,
                              preferred_element_type=jnp.float32)[0:1, :]
    pad_row = jnp.floor((cnt_row + (CHUNK - 1)) * (1.0 / CHUNK)) * CHUNK
    lower = (lax.broadcasted_iota(jnp.int32, (N_EXPERTS, N_EXPERTS), 1)
             < lax.broadcasted_iota(jnp.int32, (N_EXPERTS, N_EXPERTS), 0))
    base = jnp.sum(jnp.where(lower, pad_row, 0.0), axis=1, keepdims=True)
    slot1 = jnp.sum(hot1 * (base + rank[:N_EXPERTS]), axis=0, keepdims=True)
    slot2 = jnp.sum(hot2 * (base + n1 + rank[N_EXPERTS:]), axis=0, keepdims=True)
    slots_ref[...] = jnp.concatenate([slot1, slot2, w1, w2, jnp.zeros((4, t), jnp.float32)], axis=0)
    wide = jnp.concatenate([slot1, slot2, jnp.zeros((LANES - 2, t), jnp.float32)], axis=0)
    slots_t_ref[...] = wide.T
    counts_ref[...] = jnp.broadcast_to(n1 + n2, counts_ref.shape)


def _layout_tables(counts, n_blocks):
    cnt = counts[:, :, 0].astype(jnp.int32)
    n_tiles = cnt.shape[0]
    nch = (cnt + (CHUNK - 1)) // CHUNK
    csum = jnp.cumsum(nch, axis=0)
    rows = CHUNK * csum[-1]
    blocks = (rows + (EXPERT_BLOCK - 1)) // EXPERT_BLOCK
    blk_end = jnp.cumsum(blocks)
    blk_start = blk_end - blocks
    off = blk_start[None, :] * EXPERT_BLOCK + CHUNK * (csum - nch)
    first = jnp.cumsum(nch, axis=1) - nch
    used = jnp.sum(nch, axis=1)
    c = jnp.arange(N_CHUNKS, dtype=jnp.int32)
    e_idx = jnp.sum((c[None, :, None] >= first[:, None, :]).astype(jnp.int32), axis=2) - 1
    hot = e_idx[:, :, None] == jnp.arange(N_EXPERTS, dtype=jnp.int32)[None, None, :]
    pick = lambda a: jnp.sum(jnp.where(hot, a[:, None, :], 0), axis=2)
    row = pick(off) + CHUNK * (c[None, :] - pick(first))
    valid = c[None, :] < used[:, None]
    spare = (n_blocks * EXPERT_BLOCK + (jnp.arange(n_tiles, dtype=jnp.int32) % 2)[:, None] * TILE_SLOTS
             + CHUNK * c[None, :])
    scatter_rows = jnp.where(valid, row, spare)
    gather_rows = jnp.where(valid, row, 0)
    n_used = blk_end[-1]
    bc = jnp.minimum(jnp.arange(n_blocks, dtype=jnp.int32), n_used - 1)
    block_expert = jnp.sum((bc[:, None] >= blk_end[None, :]).astype(jnp.int32), axis=1)
    own = block_expert[:, None] == jnp.arange(N_EXPERTS, dtype=jnp.int32)[None, :]
    of_block = lambda a: jnp.sum(jnp.where(own, a[None, :], 0), axis=1)
    b = jnp.arange(n_blocks, dtype=jnp.int32)
    block_rows = jnp.clip(of_block(rows) - (b - of_block(blk_start)) * EXPERT_BLOCK, 0, EXPERT_BLOCK)
    block_rows = jnp.where(b < n_used, block_rows, 0)
    half = EXPERT_BLOCK // 2
    tail_start = blk_start * EXPERT_BLOCK + rows
    tail_chunks = (-(-rows // half) * half - rows) // CHUNK
    flat = lambda a: a.reshape(-1).astype(jnp.int32)
    return (flat(scatter_rows), flat(gather_rows), flat(tail_start), flat(tail_chunks),
            flat(block_expert), flat(block_rows), flat(n_used))


def _chunk_copies(tile, rows_ref, make_copy, action):
    for c in range(N_CHUNKS):
        action(make_copy(c * CHUNK, pl.multiple_of(rows_ref[tile * N_CHUNKS + c], CHUNK)))


def _wait_chunk_copies(make_copy):
    for _ in range(N_CHUNKS):
        make_copy(0, 0).wait()


def _dispatch_kernel(rows_ref, tail_start_ref, tail_chunks_ref, h_ref, slots_ref, xs_ref, buf, zbuf, sem, zsem):
    i = pl.program_id(0)
    n = pl.num_programs(0)
    slot = i % 2
    s = slots_ref[...]
    t = s.shape[1]
    s1 = s[0:1, :].astype(jnp.int32)
    s2 = s[1:2, :].astype(jnp.int32)
    h = h_ref[...]
    lane = lax.broadcasted_iota(jnp.int32, (SLOT_GROUP, LANES), 1)
    for g in range(TILE_SLOTS // SLOT_GROUP):
        grp = slice(g * SLOT_GROUP, (g + 1) * SLOT_GROUP)
        rid = lax.broadcasted_iota(jnp.int32, (SLOT_GROUP, t), 0) + g * SLOT_GROUP
        m1 = rid == s1
        m2 = rid == s2
        perm = jnp.where(m1, 1.0, jnp.where(m2, 1.0, 0.0)).astype(jnp.bfloat16)
        rows = jnp.dot(perm, h, preferred_element_type=jnp.float32)
        w = jnp.sum(jnp.where(m1, s[2:3, :], 0.0) + jnp.where(m2, s[3:4, :], 0.0), axis=1, keepdims=True)
        hi = w.astype(jnp.bfloat16).astype(jnp.float32)
        aux = jnp.where(lane == 0, hi, jnp.where(lane == 1, w - hi, 0.0))
        buf[slot, grp, 0:D_MODEL] = rows.astype(jnp.bfloat16)
        buf[slot, grp, D_MODEL:XS_W] = aux.astype(jnp.bfloat16)

    def copies(sl):
        def make(src_row, dst_row):
            return pltpu.make_async_copy(buf.at[sl, pl.ds(src_row, CHUNK), :],
                                         xs_ref.at[pl.ds(dst_row, CHUNK), :], sem.at[sl])
        return make

    _chunk_copies(i, rows_ref, copies(slot), lambda c: c.start())

    @pl.when(i > 0)
    def _():
        _wait_chunk_copies(copies(1 - slot))

    @pl.when(i == n - 1)
    def _():
        zbuf[...] = jnp.zeros_like(zbuf)

        def tails(action):
            for e in range(N_EXPERTS):
                start = tail_start_ref[e]

                def body(m, carry, start=start):
                    action(pltpu.make_async_copy(
                        zbuf, xs_ref.at[pl.ds(pl.multiple_of(start + m * CHUNK, CHUNK), CHUNK), :], zsem))
                    return carry

                lax.fori_loop(0, tail_chunks_ref[e], body, 0)

        tails(lambda c: c.start())
        tails(lambda c: c.wait())
        _wait_chunk_copies(copies(slot))


def _dispatch_call(tables, h2, slots, n_blocks):
    scatter_rows, tail_start, tail_chunks = tables
    tokens = h2.shape[0]
    t = DISPATCH_TILE
    return pl.pallas_call(
        _dispatch_kernel,
        grid_spec=pltpu.PrefetchScalarGridSpec(
            num_scalar_prefetch=3,
            grid=(tokens // t,),
            in_specs=[
                pl.BlockSpec((t, D_MODEL), lambda i, *_: (i, 0)),
                pl.BlockSpec((None, 8, t), lambda i, *_: (i, 0, 0)),
            ],
            out_specs=pl.BlockSpec(memory_space=pl.ANY),
            scratch_shapes=[
                pltpu.VMEM((2, TILE_SLOTS, XS_W), jnp.bfloat16),
                pltpu.VMEM((CHUNK, XS_W), jnp.bfloat16),
                pltpu.SemaphoreType.DMA((2,)),
                pltpu.SemaphoreType.DMA(()),
            ],
        ),
        out_shape=jax.ShapeDtypeStruct((n_blocks * EXPERT_BLOCK + 2 * TILE_SLOTS, XS_W), jnp.bfloat16),
        name="moe_dispatch",
    )(scatter_rows, tail_start, tail_chunks, h2, slots)


def _expert_kernel(bexp_ref, brows_ref, nused_ref, xs_ref, wg_ref, wu_ref, wd_ref, ys_ref, wgu_b, wd_b):
    b = pl.program_id(0)
    e = bexp_ref[b]
    prev = bexp_ref[jnp.maximum(b - 1, 0)]
    filled = brows_ref[b]
    half_rows = EXPERT_BLOCK // 2

    @pl.when((b == 0) | (e != prev))
    def _():
        wgu_b[:, 0:D_EXPERT] = wg_ref[...].astype(jnp.bfloat16)
        wgu_b[:, D_EXPERT:2 * D_EXPERT] = wu_ref[...].astype(jnp.bfloat16)
        wd_b[...] = wd_ref[...].astype(jnp.bfloat16)

    def mlp(rows):
        x = xs_ref[rows, 0:D_MODEL]
        y = None
        half = D_EXPERT // 2
        for c in range(2):
            cols = slice(c * half, (c + 1) * half)
            g = jnp.dot(x, wgu_b[:, c * half:(c + 1) * half], preferred_element_type=jnp.float32)
            u = jnp.dot(x, wgu_b[:, D_EXPERT + c * half:D_EXPERT + (c + 1) * half],
                        preferred_element_type=jnp.float32)
            act = (_silu(g) * u).astype(jnp.bfloat16)
            t = jnp.dot(act, wd_b[cols, :], preferred_element_type=jnp.float32)
            y = t if y is None else y + t
        w = (xs_ref[rows, D_MODEL:D_MODEL + 1].astype(jnp.float32)
             + xs_ref[rows, D_MODEL + 1:D_MODEL + 2].astype(jnp.float32))
        ys_ref[rows, :] = (w * y).astype(jnp.bfloat16)

    @pl.when(filled > half_rows)
    def _():
        mlp(slice(0, EXPERT_BLOCK))

    @pl.when((filled > 0) & (filled <= half_rows))
    def _():
        mlp(slice(0, half_rows))


def _expert_call(block_expert, block_rows, n_used, xs, w_eg, w_eu, w_ed, n_blocks):
    row_blk = lambda w: pl.BlockSpec((EXPERT_BLOCK, w), lambda b, be, br, nu: (jnp.minimum(b, nu[0] - 1), 0))
    wt_blk = lambda r, c: pl.BlockSpec((None, r, c), lambda b, be, br, nu: (be[b], 0, 0))
    return pl.pallas_call(
        _expert_kernel,
        grid_spec=pltpu.PrefetchScalarGridSpec(
            num_scalar_prefetch=3,
            grid=(n_blocks,),
            in_specs=[
                row_blk(XS_W),
                wt_blk(D_MODEL, D_EXPERT), wt_blk(D_MODEL, D_EXPERT), wt_blk(D_EXPERT, D_MODEL),
            ],
            out_specs=row_blk(D_MODEL),
            scratch_shapes=[
                pltpu.VMEM((D_MODEL, 2 * D_EXPERT), jnp.bfloat16),
                pltpu.VMEM((D_EXPERT, D_MODEL), jnp.bfloat16),
            ],
        ),
        out_shape=jax.ShapeDtypeStruct((n_blocks * EXPERT_BLOCK, D_MODEL), jnp.bfloat16),
        name="moe_experts",
    )(block_expert, block_rows, n_used, xs, w_eg, w_eu, w_ed)


def _combine_kernel(tile0, rows_ref, ys_ref, slots_t_ref, x1_ref, gate2_ref, gpost_ref, out_ref, buf, sem):
    j = pl.program_id(0)
    n = pl.num_programs(0)
    slot = j % 2
    tile = tile0 + j

    def copies(sl):
        def make(dst_row, src_row):
            return pltpu.make_async_copy(ys_ref.at[pl.ds(src_row, CHUNK), :],
                                         buf.at[sl, pl.ds(dst_row, CHUNK), :], sem.at[sl])
        return make

    @pl.when(j == 0)
    def _():
        _chunk_copies(tile, rows_ref, copies(slot), lambda c: c.start())

    @pl.when(j + 1 < n)
    def _():
        _chunk_copies(tile + 1, rows_ref, copies(1 - slot), lambda c: c.start())

    _wait_chunk_copies(copies(slot))
    gain = gate2_ref[...] * gpost_ref[...]
    sorted_rows = buf[slot]
    lane = lax.broadcasted_iota(jnp.int32, (COMBINE_SUB_ROWS, TILE_SLOTS), 1)
    for r in range(out_ref.shape[0] // COMBINE_SUB_ROWS):
        rows = slice(r * COMBINE_SUB_ROWS, (r + 1) * COMBINE_SUB_ROWS)
        s1 = slots_t_ref[rows, 0:1].astype(jnp.int32)
        s2 = slots_t_ref[rows, 1:2].astype(jnp.int32)
        unperm = jnp.where(lane == s1, 1.0, jnp.where(lane == s2, 1.0, 0.0)).astype(jnp.bfloat16)
        y = jnp.dot(unperm, sorted_rows, preferred_element_type=jnp.float32)
        out_ref[rows, :] = x1_ref[rows, :] + _rms(y) * gain


def _combine_call(gather_rows, ys, slots_t, x1, mod, mod_per_batch, g_post2, token_offset, bsz, seq):
    t = DISPATCH_TILE
    tile0 = token_offset // t
    tokens = bsz * seq
    if mod_per_batch:
        per = seq // t
        gate2_spec = pl.BlockSpec((None, 1, D_MODEL), lambda j, *_: (j // per, 0, 5))
    else:
        gate2_spec = pl.BlockSpec((None, 1, D_MODEL), lambda j, *_: (0, 0, 5))
    y = pl.pallas_call(
        functools.partial(_combine_kernel, tile0),
        grid_spec=pltpu.PrefetchScalarGridSpec(
            num_scalar_prefetch=1,
            grid=(tokens // t,),
            in_specs=[
                pl.BlockSpec(memory_space=pl.ANY),
                pl.BlockSpec((None, t, LANES), lambda j, *_: (tile0 + j, 0, 0)),
                pl.BlockSpec((t, D_MODEL), lambda j, *_: (tile0 + j, 0)),
                gate2_spec,
                pl.BlockSpec((1, D_MODEL), lambda j, *_: (0, 0)),
            ],
            out_specs=pl.BlockSpec((t, D_MODEL), lambda j, *_: (j, 0)),
            scratch_shapes=[
                pltpu.VMEM((2, TILE_SLOTS, D_MODEL), jnp.bfloat16),
                pltpu.SemaphoreType.DMA((2,)),
            ],
        ),
        out_shape=jax.ShapeDtypeStruct((tokens, D_MODEL), jnp.float32),
        name="moe_combine",
    )(gather_rows, ys, slots_t, x1, mod, g_post2)
    return y.reshape(bsz, seq, D_MODEL)


def kernel(x_prompt, x_sample, cache_k, cache_v, c, c_ctx, w_mod, b_mod, g_pre1, g_post1, g_pre2, g_post2,
           w_in, conv_w, lambda_q1, lambda_k1, lambda_q2, lambda_k2, subln_g, w_o, w_router_group,
           w_router_expert, w_exp_gate, w_exp_up, w_exp_down):
    n_lat = c.shape[0]
    cond = jnp.concatenate(
        [c, c_ctx[None, :], jnp.zeros((MOD_ROWS - n_lat - 1, D_MODEL), jnp.float32)], axis=0)
    mod, lam = _mod_call(cond, w_mod[0], b_mod, lambda_q1, lambda_k1, lambda_q2, lambda_k2)
    mod = mod.reshape(MOD_ROWS, 1, 6 * D_MODEL)
    mod_lat, mod_ctx = mod[:n_lat], mod[n_lat:n_lat + 1]

    w_in_b = w_in[0].astype(jnp.bfloat16)
    w_o_b = w_o[0].astype(jnp.bfloat16)
    w_router = jnp.concatenate(
        [w_router_group[0], jnp.zeros((D_MODEL, 8 - N_GROUPS), jnp.float32), w_router_expert[0],
         jnp.zeros((D_MODEL, LANES - 8 - N_EXPERTS), jnp.float32)], axis=1).astype(jnp.bfloat16)

    n_prompt = x_prompt.shape[0] * x_prompt.shape[1]
    n_sample = x_sample.shape[0] * x_sample.shape[1]
    total = n_prompt + n_sample

    def mixer(x, mod_x, per_batch, rope, ctx_kv, token_offset, carried):
        kv_dtype = jnp.bfloat16 if rope else jnp.float32
        q, k, v, yc = _in_proj_call(x, mod_x, per_batch, g_pre1, w_in_b, conv_w[0], rope, kv_dtype)
        kvs = []
        if ctx_kv is not None:
            ck, cv = ctx_kv
            n_ctx = ck.shape[3]
            kvs.append((ck, cv, lambda hp: pl.BlockSpec((None, None, hp, n_ctx, D_V),
                                                        lambda b, h, i: (b, 0, h, 0, 0))))
        seq = x.shape[1]
        kvs.append((k, v, lambda hp: pl.BlockSpec((None, hp, seq, D_V), lambda b, h, i: (b, h, 0, 0))))
        o = _attn_call(lam, subln_g, q, kvs, N_HEADS if seq <= MERGE_HEADS_MAX_SEQ else 1)
        if not per_batch:
            o, yc, x = (a.reshape(1, -1, a.shape[-1]) for a in (o, yc, x))
        shared = _out_proj_call(o, yc, w_o_b, x, mod_x, per_batch, g_post1, g_pre2, w_router,
                                total, token_offset, carried)
        return shared, k, v

    shared, kp, vp = mixer(x_prompt, mod_ctx, False, False, None, 0, None)
    (x1, h2, slots, slots_t, counts), _, _ = mixer(x_sample, mod_lat, True, True, (cache_k, cache_v),
                                                   n_prompt, shared)

    n_tiles = total // DISPATCH_TILE
    max_rows = 2 * total + n_tiles * N_EXPERTS * (CHUNK - 1) + N_EXPERTS * (EXPERT_BLOCK - CHUNK)
    n_blocks = -(-max_rows // EXPERT_BLOCK)
    (scatter_rows, gather_rows, tail_start, tail_chunks, block_expert, block_rows,
     n_used) = _layout_tables(counts, n_blocks)
    xs = _dispatch_call((scatter_rows, tail_start, tail_chunks), h2, slots, n_blocks)
    ys = _expert_call(block_expert, block_rows, n_used, xs, w_exp_gate[0], w_exp_up[0], w_exp_down[0], n_blocks)
    yp = _combine_call(gather_rows, ys, slots_t, x1, mod_ctx, False, g_post2, 0,
                       x_prompt.shape[0], x_prompt.shape[1])
    ysamp = _combine_call(gather_rows, ys, slots_t, x1, mod_lat, True, g_post2, n_prompt,
                          x_sample.shape[0], x_sample.shape[1])
    return yp, ysamp, kp[:, None], vp[:, None]
```

```python
import functools
import math

import numpy as np
import jax
import jax.numpy as jnp
from jax import lax
from jax.experimental import pallas as pl
from jax.experimental.pallas import tpu as pltpu

D_MODEL = 1024
GRID_W = 64
N_HEADS = 4
D_QK = 64
D_V = 128
ATTN_W = N_HEADS * D_V
CONV_W = D_MODEL - ATTN_W
IN_W = 3 * ATTN_W + 3 * CONV_W
N_GROUPS = 4
EXP_PER_GROUP = 4
N_EXPERTS = N_GROUPS * EXP_PER_GROUP
D_EXPERT = 512
ROPE_BASE = 10000.0
EPS = 1e-6
LAM_INIT = 0.8 - 0.6 * math.exp(-0.3 * 0)

LANES = 128
F32_SUBLANES = 8
BF16_SUBLANES = 16
MOD_ROWS = 16
TOKEN_TILE = 1024
OUT_SUB_ROWS = 256
Q_TILE = 2048
KEY_CHUNK = 256
MERGE_HEADS_MAX_SEQ = 512
DISPATCH_TILE = 512
CHUNK = BF16_SUBLANES
TILE_SLOTS = 1280
N_CHUNKS = TILE_SLOTS // CHUNK
SLOT_GROUP = 256
COMBINE_SUB_ROWS = 256
EXPERT_BLOCK = 1024
XS_W = D_MODEL + LANES
QK_SCALE = (1.0 / math.sqrt(D_QK)) * math.log2(math.e)

NT_DIMS = (((1,), (1,)), ((), ()))


def _rms(x):
    return x * lax.rsqrt(jnp.mean(x * x, axis=-1, keepdims=True) + EPS)


def _silu(x):
    return x * (1.0 / (1.0 + jnp.exp(-x)))


def _mod_kernel(cond_ref, w_ref, b_ref, lq1_ref, lk1_ref, lq2_ref, lk2_ref, mod_ref, lam_ref):
    s = _silu(cond_ref[...])
    s_hi = s.astype(jnp.bfloat16)
    s_lo = (s - s_hi.astype(jnp.float32)).astype(jnp.bfloat16)
    w = w_ref[...]
    w_hi = w.astype(jnp.bfloat16)
    w_lo = (w - w_hi.astype(jnp.float32)).astype(jnp.bfloat16)
    both = jnp.dot(jnp.concatenate([s_hi, s_lo], axis=0), w_hi, preferred_element_type=jnp.float32)
    m = both[0:MOD_ROWS] + both[MOD_ROWS:2 * MOD_ROWS] + jnp.dot(s_hi, w_lo, preferred_element_type=jnp.float32)
    mod_ref[...] = m + b_ref[...]
    a = jnp.sum(lq1_ref[...] * lk1_ref[...], axis=-1, keepdims=True)
    b = jnp.sum(lq2_ref[...] * lk2_ref[...], axis=-1, keepdims=True)
    lam_ref[...] = jnp.broadcast_to(jnp.exp(a) - jnp.exp(b) + LAM_INIT, lam_ref.shape)


def _mod_call(cond, w_mod, b_mod, lq1, lk1, lq2, lk2):
    n_col = 6 * D_MODEL
    col_tile = 1536
    small = pl.BlockSpec((1, D_QK), lambda j: (0, 0))
    return pl.pallas_call(
        _mod_kernel,
        grid=(n_col // col_tile,),
        in_specs=[
            pl.BlockSpec((MOD_ROWS, D_MODEL), lambda j: (0, 0)),
            pl.BlockSpec((D_MODEL, col_tile), lambda j: (0, j)),
            pl.BlockSpec((1, col_tile), lambda j: (0, j)),
            small, small, small, small,
        ],
        out_specs=[
            pl.BlockSpec((MOD_ROWS, col_tile), lambda j: (0, j)),
            pl.BlockSpec((1, LANES), lambda j: (0, 0)),
        ],
        out_shape=[
            jax.ShapeDtypeStruct((MOD_ROWS, n_col), jnp.float32),
            jax.ShapeDtypeStruct((1, LANES), jnp.float32),
        ],
        name="mod",
    )(cond, w_mod, b_mod, lq1, lk1, lq2, lk2)


def _in_proj_kernel(rope, n_tiles, x_ref, xp_ref, xn_ref, shift_ref, scale_ref, g_ref, w_ref, cw_ref, *rest):
    if rope:
        cos_ref, sina_ref, sinb_ref, q_ref, k_ref, v_ref, yc_ref = rest
    else:
        q_ref, k_ref, v_ref, yc_ref = rest
    i = pl.program_id(1)
    tm = x_ref.shape[0]
    gain = g_ref[...] * (1.0 + scale_ref[...])
    shift = shift_ref[...]

    def modulate(x):
        return (_rms(x) * gain + shift).astype(jnp.bfloat16)

    h = modulate(x_ref[...])
    h_halo = modulate(jnp.concatenate([xp_ref[...], xn_ref[...]], axis=0))

    def proj(lhs, lo, hi):
        return jnp.dot(lhs, w_ref[:, lo:hi], preferred_element_type=jnp.float32)

    def rot(t):
        return (t * cos_ref[...] + pltpu.roll(t, LANES - 16, axis=1) * sina_ref[...]
                + pltpu.roll(t, 16, axis=1) * sinb_ref[...])

    c0 = 3 * ATTN_W
    h_ext = jnp.concatenate([h, h_halo], axis=0)
    cu_all = proj(h_ext, c0 + CONV_W, c0 + 2 * CONV_W) * proj(h_ext, c0 + 2 * CONV_W, c0 + 3 * CONV_W)
    gb = proj(h, c0, c0 + CONV_W)
    cu = cu_all[0:tm]
    prev_row = jnp.where(i > 0, cu_all[tm + 7:tm + 8], 0.0)
    next_row = jnp.where(i < n_tiles - 1, cu_all[tm + 8:tm + 9], 0.0)
    row = lax.broadcasted_iota(jnp.int32, cu.shape, 0)
    prev = jnp.where(row == 0, prev_row, pltpu.roll(cu, 1, axis=0))
    nxt = jnp.where(row == tm - 1, next_row, pltpu.roll(cu, tm - 1, axis=0))
    conv = cw_ref[0:1, :] * prev + cw_ref[1:2, :] * cu + cw_ref[2:3, :] * nxt
    yc_ref[...] = (gb * conv).astype(yc_ref.dtype)

    zq = proj(h, 0, ATTN_W)
    for hd in range(N_HEADS):
        q = zq[:, hd * D_V:(hd + 1) * D_V]
        q_ref[hd] = ((rot(q) if rope else q) * QK_SCALE).astype(q_ref.dtype)
    zk = proj(h, ATTN_W, 2 * ATTN_W)
    for hd in range(N_HEADS):
        k = zk[:, hd * D_V:(hd + 1) * D_V]
        k_ref[hd] = (rot(k) if rope else k).astype(k_ref.dtype)
    zv = proj(h, 2 * ATTN_W, 3 * ATTN_W)
    for hd in range(N_HEADS):
        v_ref[hd] = zv[:, hd * D_V:(hd + 1) * D_V].astype(v_ref.dtype)


def _rope_tables(seq):
    n_rows = seq // GRID_W
    row = np.repeat(np.arange(n_rows), GRID_W).astype(np.float64)
    col = np.tile(np.arange(GRID_W), n_rows).astype(np.float64)
    nf = D_QK // 4
    inv = ROPE_BASE ** (-np.arange(nf, dtype=np.float64) / nf)
    ar = row[:, None] * inv
    ac = col[:, None] * inv
    ang = np.concatenate([ar, ar, ac, ac], axis=-1)
    ang = np.concatenate([ang, ang], axis=-1)
    first_half = (np.arange(LANES) % 32) < 16
    cos = np.cos(ang)
    sin = np.sin(ang)
    sina = np.where(first_half, -sin, 0.0)
    sinb = np.where(first_half, 0.0, sin)
    return tuple(jnp.asarray(t, dtype=jnp.float32) for t in (cos, sina, sinb))


def _in_proj_call(x, mod, mod_per_batch, g_pre1, w_in, conv_w, rope, kv_dtype):
    bsz, seq, _ = x.shape
    tm = min(TOKEN_TILE, seq)
    n_tiles = seq // tm
    halo = tm // F32_SUBLANES
    n_halo = seq // F32_SUBLANES

    def mod_spec(chunk):
        if mod_per_batch:
            return pl.BlockSpec((None, 1, D_MODEL), lambda b, i: (b, 0, chunk))
        return pl.BlockSpec((None, 1, D_MODEL), lambda b, i: (0, 0, chunk))

    in_specs = [
        pl.BlockSpec((None, tm, D_MODEL), lambda b, i: (b, i, 0)),
        pl.BlockSpec((None, F32_SUBLANES, D_MODEL), lambda b, i: (b, jnp.maximum(i * halo - 1, 0), 0)),
        pl.BlockSpec((None, F32_SUBLANES, D_MODEL), lambda b, i: (b, jnp.minimum((i + 1) * halo, n_halo - 1), 0)),
        mod_spec(0), mod_spec(1),
        pl.BlockSpec((1, D_MODEL), lambda b, i: (0, 0)),
        pl.BlockSpec((D_MODEL, IN_W), lambda b, i: (0, 0)),
        pl.BlockSpec((3, CONV_W), lambda b, i: (0, 0)),
    ]
    args = [x, x, x, mod, mod, g_pre1, w_in, conv_w]
    if rope:
        in_specs += [pl.BlockSpec((tm, LANES), lambda b, i: (i, 0))] * 3
        args += list(_rope_tables(seq))
    head_spec = pl.BlockSpec((None, N_HEADS, tm, D_V), lambda b, i: (b, 0, i, 0))
    return pl.pallas_call(
        functools.partial(_in_proj_kernel, rope, n_tiles),
        grid=(bsz, n_tiles),
        in_specs=in_specs,
        out_specs=[head_spec, head_spec, head_spec, pl.BlockSpec((None, tm, CONV_W), lambda b, i: (b, i, 0))],
        out_shape=[
            jax.ShapeDtypeStruct((bsz, N_HEADS, seq, D_V), jnp.bfloat16),
            jax.ShapeDtypeStruct((bsz, N_HEADS, seq, D_V), kv_dtype),
            jax.ShapeDtypeStruct((bsz, N_HEADS, seq, D_V), kv_dtype),
            jax.ShapeDtypeStruct((bsz, seq, CONV_W), jnp.bfloat16),
        ],
        name="in_proj_rope" if rope else "in_proj",
    )(*args)


def _attn_kernel(n_kv, lam_ref, g_ref, q_ref, *rest):
    kv_refs = rest[:2 * n_kv]
    o_ref = rest[2 * n_kv]
    heads, tq, _ = q_ref.shape
    lane = lax.broadcasted_iota(jnp.int32, (tq, 2 * D_QK), 1)
    for hh in range(heads):
        q = q_ref[hh]
        zero = jnp.zeros_like(q)
        halves = (jnp.where(lane < D_QK, q, zero), jnp.where(lane >= D_QK, q, zero))
        acc = [jnp.zeros((tq, 2 * D_V), jnp.float32) for _ in range(2)]
        m = [jnp.full((tq, 1), -1e30, jnp.float32) for _ in range(2)]
        for j in range(n_kv):
            k_ref, v_ref = kv_refs[2 * j], kv_refs[2 * j + 1]
            n_keys = k_ref.shape[1]
            ck = min(KEY_CHUNK, n_keys)
            ones = jnp.ones((ck, D_V), jnp.bfloat16)
            for c in range(n_keys // ck):
                k = k_ref[hh, c * ck:(c + 1) * ck, :].astype(jnp.bfloat16)
                v1 = jnp.concatenate([v_ref[hh, c * ck:(c + 1) * ck, :].astype(jnp.bfloat16), ones], axis=1)
                for x, qh in enumerate(halves):
                    s = lax.dot_general(qh, k, NT_DIMS, preferred_element_type=jnp.float32)
                    m_new = jnp.maximum(m[x], jnp.max(s, axis=-1, keepdims=True))
                    p = jnp.exp2(s - m_new).astype(jnp.bfloat16)
                    acc[x] = (jnp.exp2(m[x] - m_new) * acc[x]
                              + jnp.dot(p, v1, preferred_element_type=jnp.float32))
                    m[x] = m_new
        o = (acc[0][:, 0:D_V] / acc[0][:, D_V:2 * D_V]
             - lam_ref[0:1, 0:1] * (acc[1][:, 0:D_V] / acc[1][:, D_V:2 * D_V]))
        o_ref[:, hh * D_V:(hh + 1) * D_V] = (_rms(o) * (g_ref[...] * (1.0 - LAM_INIT))).astype(o_ref.dtype)


def _attn_call(lam, subln_g, q, kvs, heads_per_step):
    bsz, _, seq, _ = q.shape
    tq = min(Q_TILE, seq)
    hp = heads_per_step
    in_specs = [
        pl.BlockSpec((1, LANES), lambda b, h, i: (0, 0)),
        pl.BlockSpec((1, D_V), lambda b, h, i: (0, 0)),
        pl.BlockSpec((None, hp, tq, D_V), lambda b, h, i: (b, h, i, 0)),
    ]
    args = [lam, subln_g, q]
    for k, v, spec_fn in kvs:
        in_specs += [spec_fn(hp), spec_fn(hp)]
        args += [k, v]
    return pl.pallas_call(
        functools.partial(_attn_kernel, len(kvs)),
        grid=(bsz, N_HEADS // hp, seq // tq),
        in_specs=in_specs,
        out_specs=pl.BlockSpec((None, tq, hp * D_V), lambda b, h, i: (b, i, h)),
        out_shape=jax.ShapeDtypeStruct((bsz, seq, ATTN_W), jnp.bfloat16),
        name="attn%d" % len(kvs),
    )(*args)


def _route(logits_t):
    lg = [logits_t[g:g + 1, :] for g in range(N_GROUPS)]
    mg = functools.reduce(jnp.maximum, lg)
    p_sel = 1.0 / functools.reduce(jnp.add, [jnp.exp(t - mg) for t in lg])
    g_sel = jnp.full(mg.shape, N_GROUPS - 1, jnp.int32)
    for g in range(N_GROUPS - 2, -1, -1):
        g_sel = jnp.where(lg[g] == mg, g, g_sel)
    le = []
    for j in range(EXP_PER_GROUP):
        t = jnp.zeros_like(mg)
        for g in range(N_GROUPS):
            r = 8 + g * EXP_PER_GROUP + j
            t = jnp.where(g_sel == g, logits_t[r:r + 1, :], t)
        le.append(t)
    m1 = functools.reduce(jnp.maximum, le)
    i1 = jnp.full(mg.shape, EXP_PER_GROUP - 1, jnp.int32)
    for j in range(EXP_PER_GROUP - 2, -1, -1):
        i1 = jnp.where(le[j] == m1, j, i1)
    neg = jnp.float32(-jnp.inf)
    rest = [jnp.where(i1 == j, neg, le[j]) for j in range(EXP_PER_GROUP)]
    m2 = functools.reduce(jnp.maximum, rest)
    i2 = jnp.full(mg.shape, EXP_PER_GROUP - 1, jnp.int32)
    for j in range(EXP_PER_GROUP - 2, -1, -1):
        i2 = jnp.where(rest[j] == m2, j, i2)
    e2 = jnp.exp(m2 - m1)
    w1 = p_sel / (1.0 + e2)
    w2 = p_sel * e2 / (1.0 + e2)
    base = g_sel * EXP_PER_GROUP
    return (base + i1).astype(jnp.float32), (base + i2).astype(jnp.float32), w1, w2


def _out_proj_kernel(o_ref, yc_ref, wo_ref, x_ref, gate1_ref, shift2_ref, scale2_ref, gpost_ref, gpre_ref,
                     wr_ref, *rest):
    x1_ref, h2_ref, slots_ref, slots_t_ref, counts_ref = rest[-5:]
    tm = o_ref.shape[0]
    sub = min(OUT_SUB_ROWS, tm)
    gain1 = gate1_ref[...] * gpost_ref[...]
    gain2 = gpre_ref[...] * (1.0 + scale2_ref[...])
    routes = []
    for r in range(tm // sub):
        rows = slice(r * sub, (r + 1) * sub)
        out = (jnp.dot(o_ref[rows, :], wo_ref[0:ATTN_W, :], preferred_element_type=jnp.float32)
               + jnp.dot(yc_ref[rows, :], wo_ref[ATTN_W:D_MODEL, :], preferred_element_type=jnp.float32))
        x1 = x_ref[rows, :] + _rms(out) * gain1
        x1_ref[rows, :] = x1
        h2 = (_rms(x1) * gain2 + shift2_ref[...]).astype(jnp.bfloat16)
        h2_ref[rows, :] = h2
        logits = jnp.dot(h2, wr_ref[...], preferred_element_type=jnp.float32)
        routes.append(_route(logits.T))
    e1, e2, w1, w2 = (jnp.concatenate(parts, axis=1) for parts in zip(*routes))
    _plan(e1, e2, w1, w2, slots_ref, slots_t_ref, counts_ref)


def _out_proj_call(o, yc, w_o, x, mod, mod_per_batch, g_post1, g_pre2, w_router,
                   total_tokens, token_offset, carried):
    bsz, seq, _ = x.shape
    tm = DISPATCH_TILE
    assert seq % tm == 0 and token_offset % tm == 0
    n_tiles = seq // tm
    tile0 = token_offset // tm
    n_all = total_tokens // tm

    def mod_spec(chunk):
        if mod_per_batch:
            return pl.BlockSpec((None, 1, D_MODEL), lambda b, i: (b, 0, chunk))
        return pl.BlockSpec((None, 1, D_MODEL), lambda b, i: (0, 0, chunk))

    def vec_spec():
        return pl.BlockSpec((1, D_MODEL), lambda b, i: (0, 0))

    tok = lambda w: pl.BlockSpec((None, tm, w), lambda b, i: (b, i, 0))
    flat = lambda w: pl.BlockSpec((tm, w), lambda b, i: (tile0 + b * n_tiles + i, 0))
    per_tile = lambda r, c: pl.BlockSpec((None, r, c), lambda b, i: (tile0 + b * n_tiles + i, 0, 0))
    in_specs = [
        tok(ATTN_W), tok(CONV_W),
        pl.BlockSpec((D_MODEL, D_MODEL), lambda b, i: (0, 0)),
        tok(D_MODEL),
        mod_spec(2), mod_spec(3), mod_spec(4),
        vec_spec(), vec_spec(),
        pl.BlockSpec((D_MODEL, LANES), lambda b, i: (0, 0)),
    ]
    args = [o, yc, w_o, x, mod, mod, mod, g_post1, g_pre2, w_router]
    aliases = {}
    if carried is not None:
        for j, arr in enumerate(carried):
            aliases[len(args)] = j
            in_specs.append(pl.BlockSpec(memory_space=pl.ANY))
            args.append(arr)
    return pl.pallas_call(
        _out_proj_kernel,
        grid=(bsz, n_tiles),
        in_specs=in_specs,
        out_specs=[flat(D_MODEL), flat(D_MODEL), per_tile(8, tm), per_tile(tm, LANES), per_tile(N_EXPERTS, LANES)],
        out_shape=[
            jax.ShapeDtypeStruct((total_tokens, D_MODEL), jnp.float32),
            jax.ShapeDtypeStruct((total_tokens, D_MODEL), jnp.bfloat16),
            jax.ShapeDtypeStruct((n_all, 8, tm), jnp.float32),
            jax.ShapeDtypeStruct((n_all, tm, LANES), jnp.float32),
            jax.ShapeDtypeStruct((n_all, N_EXPERTS, LANES), jnp.float32),
        ],
        input_output_aliases=aliases,
        name="out_proj",
    )(*args)


def _plan(e1, e2, w1, w2, slots_ref, slots_t_ref, counts_ref):
    t = e1.shape[1]
    e1 = e1.astype(jnp.int32)
    e2 = e2.astype(jnp.int32)
    eid = lax.broadcasted_iota(jnp.int32, (N_EXPERTS, t), 0)
    hot1 = jnp.where(eid == e1, 1.0, 0.0)
    hot2 = jnp.where(eid == e2, 1.0, 0.0)
    hot = jnp.concatenate([hot1, hot2], axis=0).astype(jnp.bfloat16)
    before = (lax.broadcasted_iota(jnp.int32, (t, t), 0) < lax.broadcasted_iota(jnp.int32, (t, t), 1))
    before = jnp.where(before, 1.0, 0.0).astype(jnp.bfloat16)
    rank = jnp.dot(hot, before, preferred_element_type=jnp.float32)
    n1 = jnp.sum(hot1, axis=1, keepdims=True)
    n2 = jnp.sum(hot2, axis=1, keepdims=True)
    ones = jnp.ones((8, t), jnp.bfloat16)
    cnt_row = lax.dot_general(ones, (hot1 + hot2).astype(jnp.bfloat16), NT_DIMS,
                              preferred_element_type=jnp.float32)[0:1, :]
    pad_row = jnp.floor((cnt_row + (CHUNK - 1)) * (1.0 / CHUNK)) * CHUNK
    lower = (lax.broadcasted_iota(jnp.int32, (N_EXPERTS, N_EXPERTS), 1)
             < lax.broadcasted_iota(jnp.int32, (N_EXPERTS, N_EXPERTS), 0))
    base = jnp.sum(jnp.where(lower, pad_row, 0.0), axis=1, keepdims=True)
    slot1 = jnp.sum(hot1 * (base + rank[:N_EXPERTS]), axis=0, keepdims=True)
    slot2 = jnp.sum(hot2 * (base + n1 + rank[N_EXPERTS:]), axis=0, keepdims=True)
    slots_ref[...] = jnp.concatenate([slot1, slot2, w1, w2, jnp.zeros((4, t), jnp.float32)], axis=0)
    wide = jnp.concatenate([slot1, slot2, jnp.zeros((LANES - 2, t), jnp.float32)], axis=0)
    slots_t_ref[...] = wide.T
    counts_ref[...] = jnp.broadcast_to(n1 + n2, counts_ref.shape)


def _layout_tables(counts, n_blocks):
    cnt = counts[:, :, 0].astype(jnp.int32)
    n_tiles = cnt.shape[0]
    nch = (cnt + (CHUNK - 1)) // CHUNK
    csum = jnp.cumsum(nch, axis=0)
    rows = CHUNK * csum[-1]
    blocks = (rows + (EXPERT_BLOCK - 1)) // EXPERT_BLOCK
    blk_end = jnp.cumsum(blocks)
    blk_start = blk_end - blocks
    off = blk_start[None, :] * EXPERT_BLOCK + CHUNK * (csum - nch)
    first = jnp.cumsum(nch, axis=1) - nch
    used = jnp.sum(nch, axis=1)
    c = jnp.arange(N_CHUNKS, dtype=jnp.int32)
    e_idx = jnp.sum((c[None, :, None] >= first[:, None, :]).astype(jnp.int32), axis=2) - 1
    hot = e_idx[:, :, None] == jnp.arange(N_EXPERTS, dtype=jnp.int32)[None, None, :]
    pick = lambda a: jnp.sum(jnp.where(hot, a[:, None, :], 0), axis=2)
    row = pick(off) + CHUNK * (c[None, :] - pick(first))
    valid = c[None, :] < used[:, None]
    spare = (n_blocks * EXPERT_BLOCK + (jnp.arange(n_tiles, dtype=jnp.int32) % 2)[:, None] * TILE_SLOTS
             + CHUNK * c[None, :])
    scatter_rows = jnp.where(valid, row, spare)
    gather_rows = jnp.where(valid, row, 0)
    n_used = blk_end[-1]
    bc = jnp.minimum(jnp.arange(n_blocks, dtype=jnp.int32), n_used - 1)
    block_expert = jnp.sum((bc[:, None] >= blk_end[None, :]).astype(jnp.int32), axis=1)
    tail_start = blk_start * EXPERT_BLOCK + rows
    tail_chunks = (blocks * EXPERT_BLOCK - rows) // CHUNK
    flat = lambda a: a.reshape(-1).astype(jnp.int32)
    return (flat(scatter_rows), flat(gather_rows), flat(tail_start), flat(tail_chunks),
            flat(block_expert), flat(n_used))


def _chunk_copies(tile, rows_ref, make_copy, action):
    for c in range(N_CHUNKS):
        action(make_copy(c * CHUNK, pl.multiple_of(rows_ref[tile * N_CHUNKS + c], CHUNK)))


def _wait_chunk_copies(make_copy):
    for _ in range(N_CHUNKS):
        make_copy(0, 0).wait()


def _dispatch_kernel(rows_ref, tail_start_ref, tail_chunks_ref, h_ref, slots_ref, xs_ref, buf, zbuf, sem, zsem):
    i = pl.program_id(0)
    n = pl.num_programs(0)
    slot = i % 2
    s = slots_ref[...]
    t = s.shape[1]
    s1 = s[0:1, :].astype(jnp.int32)
    s2 = s[1:2, :].astype(jnp.int32)
    h = h_ref[...]
    lane = lax.broadcasted_iota(jnp.int32, (SLOT_GROUP, LANES), 1)
    for g in range(TILE_SLOTS // SLOT_GROUP):
        grp = slice(g * SLOT_GROUP, (g + 1) * SLOT_GROUP)
        rid = lax.broadcasted_iota(jnp.int32, (SLOT_GROUP, t), 0) + g * SLOT_GROUP
        m1 = rid == s1
        m2 = rid == s2
        perm = jnp.where(m1, 1.0, jnp.where(m2, 1.0, 0.0)).astype(jnp.bfloat16)
        rows = jnp.dot(perm, h, preferred_element_type=jnp.float32)
        w = jnp.sum(jnp.where(m1, s[2:3, :], 0.0) + jnp.where(m2, s[3:4, :], 0.0), axis=1, keepdims=True)
        hi = w.astype(jnp.bfloat16).astype(jnp.float32)
        aux = jnp.where(lane == 0, hi, jnp.where(lane == 1, w - hi, 0.0))
        buf[slot, grp, 0:D_MODEL] = rows.astype(jnp.bfloat16)
        buf[slot, grp, D_MODEL:XS_W] = aux.astype(jnp.bfloat16)

    def copies(sl):
        def make(src_row, dst_row):
            return pltpu.make_async_copy(buf.at[sl, pl.ds(src_row, CHUNK), :],
                                         xs_ref.at[pl.ds(dst_row, CHUNK), :], sem.at[sl])
        return make

    _chunk_copies(i, rows_ref, copies(slot), lambda c: c.start())

    @pl.when(i > 0)
    def _():
        _wait_chunk_copies(copies(1 - slot))

    @pl.when(i == n - 1)
    def _():
        zbuf[...] = jnp.zeros_like(zbuf)

        def tails(action):
            for e in range(N_EXPERTS):
                start = tail_start_ref[e]

                def body(m, carry, start=start):
                    action(pltpu.make_async_copy(
                        zbuf, xs_ref.at[pl.ds(pl.multiple_of(start + m * CHUNK, CHUNK), CHUNK), :], zsem))
                    return carry

                lax.fori_loop(0, tail_chunks_ref[e], body, 0)

        tails(lambda c: c.start())
        tails(lambda c: c.wait())
        _wait_chunk_copies(copies(slot))


def _dispatch_call(tables, h2, slots, n_blocks):
    scatter_rows, tail_start, tail_chunks = tables
    tokens = h2.shape[0]
    t = DISPATCH_TILE
    return pl.pallas_call(
        _dispatch_kernel,
        grid_spec=pltpu.PrefetchScalarGridSpec(
            num_scalar_prefetch=3,
            grid=(tokens // t,),
            in_specs=[
                pl.BlockSpec((t, D_MODEL), lambda i, *_: (i, 0)),
                pl.BlockSpec((None, 8, t), lambda i, *_: (i, 0, 0)),
            ],
            out_specs=pl.BlockSpec(memory_space=pl.ANY),
            scratch_shapes=[
                pltpu.VMEM((2, TILE_SLOTS, XS_W), jnp.bfloat16),
                pltpu.VMEM((CHUNK, XS_W), jnp.bfloat16),
                pltpu.SemaphoreType.DMA((2,)),
                pltpu.SemaphoreType.DMA(()),
            ],
        ),
        out_shape=jax.ShapeDtypeStruct((n_blocks * EXPERT_BLOCK + 2 * TILE_SLOTS, XS_W), jnp.bfloat16),
        name="moe_dispatch",
    )(scatter_rows, tail_start, tail_chunks, h2, slots)


def _expert_kernel(bexp_ref, nused_ref, xs_ref, wg_ref, wu_ref, wd_ref, ys_ref, wgu_b, wd_b):
    b = pl.program_id(0)
    e = bexp_ref[b]
    prev = bexp_ref[jnp.maximum(b - 1, 0)]

    @pl.when((b == 0) | (e != prev))
    def _():
        wgu_b[:, 0:D_EXPERT] = wg_ref[...].astype(jnp.bfloat16)
        wgu_b[:, D_EXPERT:2 * D_EXPERT] = wu_ref[...].astype(jnp.bfloat16)
        wd_b[...] = wd_ref[...].astype(jnp.bfloat16)

    @pl.when(b < nused_ref[0])
    def _():
        x = xs_ref[:, 0:D_MODEL]
        y = None
        half = D_EXPERT // 2
        for c in range(2):
            cols = slice(c * half, (c + 1) * half)
            g = jnp.dot(x, wgu_b[:, c * half:(c + 1) * half], preferred_element_type=jnp.float32)
            u = jnp.dot(x, wgu_b[:, D_EXPERT + c * half:D_EXPERT + (c + 1) * half],
                        preferred_element_type=jnp.float32)
            act = (_silu(g) * u).astype(jnp.bfloat16)
            t = jnp.dot(act, wd_b[cols, :], preferred_element_type=jnp.float32)
            y = t if y is None else y + t
        w = (xs_ref[:, D_MODEL:D_MODEL + 1].astype(jnp.float32)
             + xs_ref[:, D_MODEL + 1:D_MODEL + 2].astype(jnp.float32))
        ys_ref[...] = (w * y).astype(jnp.bfloat16)


def _expert_call(block_expert, n_used, xs, w_eg, w_eu, w_ed, n_blocks):
    row_blk = lambda w: pl.BlockSpec((EXPERT_BLOCK, w), lambda b, be, nu: (jnp.minimum(b, nu[0] - 1), 0))
    wt_blk = lambda r, c: pl.BlockSpec((None, r, c), lambda b, be, nu: (be[b], 0, 0))
    return pl.pallas_call(
        _expert_kernel,
        grid_spec=pltpu.PrefetchScalarGridSpec(
            num_scalar_prefetch=2,
            grid=(n_blocks,),
            in_specs=[
                row_blk(XS_W),
                wt_blk(D_MODEL, D_EXPERT), wt_blk(D_MODEL, D_EXPERT), wt_blk(D_EXPERT, D_MODEL),
            ],
            out_specs=row_blk(D_MODEL),
            scratch_shapes=[
                pltpu.VMEM((D_MODEL, 2 * D_EXPERT), jnp.bfloat16),
                pltpu.VMEM((D_EXPERT, D_MODEL), jnp.bfloat16),
            ],
        ),
        out_shape=jax.ShapeDtypeStruct((n_blocks * EXPERT_BLOCK, D_MODEL), jnp.bfloat16),
        name="moe_experts",
    )(block_expert, n_used, xs, w_eg, w_eu, w_ed)


def _combine_kernel(tile0, rows_ref, ys_ref, slots_t_ref, x1_ref, gate2_ref, gpost_ref, out_ref, buf, sem):
    j = pl.program_id(0)
    n = pl.num_programs(0)
    slot = j % 2
    tile = tile0 + j

    def copies(sl):
        def make(dst_row, src_row):
            return pltpu.make_async_copy(ys_ref.at[pl.ds(src_row, CHUNK), :],
                                         buf.at[sl, pl.ds(dst_row, CHUNK), :], sem.at[sl])
        return make

    @pl.when(j == 0)
    def _():
        _chunk_copies(tile, rows_ref, copies(slot), lambda c: c.start())

    @pl.when(j + 1 < n)
    def _():
        _chunk_copies(tile + 1, rows_ref, copies(1 - slot), lambda c: c.start())

    _wait_chunk_copies(copies(slot))
    gain = gate2_ref[...] * gpost_ref[...]
    sorted_rows = buf[slot]
    lane = lax.broadcasted_iota(jnp.int32, (COMBINE_SUB_ROWS, TILE_SLOTS), 1)
    for r in range(out_ref.shape[0] // COMBINE_SUB_ROWS):
        rows = slice(r * COMBINE_SUB_ROWS, (r + 1) * COMBINE_SUB_ROWS)
        s1 = slots_t_ref[rows, 0:1].astype(jnp.int32)
        s2 = slots_t_ref[rows, 1:2].astype(jnp.int32)
        unperm = jnp.where(lane == s1, 1.0, jnp.where(lane == s2, 1.0, 0.0)).astype(jnp.bfloat16)
        y = jnp.dot(unperm, sorted_rows, preferred_element_type=jnp.float32)
        out_ref[rows, :] = x1_ref[rows, :] + _rms(y) * gain


def _combine_call(gather_rows, ys, slots_t, x1, mod, mod_per_batch, g_post2, token_offset, bsz, seq):
    t = DISPATCH_TILE
    tile0 = token_offset // t
    tokens = bsz * seq
    if mod_per_batch:
        per = seq // t
        gate2_spec = pl.BlockSpec((None, 1, D_MODEL), lambda j, *_: (j // per, 0, 5))
    else:
        gate2_spec = pl.BlockSpec((None, 1, D_MODEL), lambda j, *_: (0, 0, 5))
    y = pl.pallas_call(
        functools.partial(_combine_kernel, tile0),
        grid_spec=pltpu.PrefetchScalarGridSpec(
            num_scalar_prefetch=1,
            grid=(tokens // t,),
            in_specs=[
                pl.BlockSpec(memory_space=pl.ANY),
                pl.BlockSpec((None, t, LANES), lambda j, *_: (tile0 + j, 0, 0)),
                pl.BlockSpec((t, D_MODEL), lambda j, *_: (tile0 + j, 0)),
                gate2_spec,
                pl.BlockSpec((1, D_MODEL), lambda j, *_: (0, 0)),
            ],
            out_specs=pl.BlockSpec((t, D_MODEL), lambda j, *_: (j, 0)),
            scratch_shapes=[
                pltpu.VMEM((2, TILE_SLOTS, D_MODEL), jnp.bfloat16),
                pltpu.SemaphoreType.DMA((2,)),
            ],
        ),
        out_shape=jax.ShapeDtypeStruct((tokens, D_MODEL), jnp.float32),
        name="moe_combine",
    )(gather_rows, ys, slots_t, x1, mod, g_post2)
    return y.reshape(bsz, seq, D_MODEL)


def kernel(x_prompt, x_sample, cache_k, cache_v, c, c_ctx, w_mod, b_mod, g_pre1, g_post1, g_pre2, g_post2,
           w_in, conv_w, lambda_q1, lambda_k1, lambda_q2, lambda_k2, subln_g, w_o, w_router_group,
           w_router_expert, w_exp_gate, w_exp_up, w_exp_down):
    n_lat = c.shape[0]
    cond = jnp.concatenate(
        [c, c_ctx[None, :], jnp.zeros((MOD_ROWS - n_lat - 1, D_MODEL), jnp.float32)], axis=0)
    mod, lam = _mod_call(cond, w_mod[0], b_mod, lambda_q1, lambda_k1, lambda_q2, lambda_k2)
    mod = mod.reshape(MOD_ROWS, 1, 6 * D_MODEL)
    mod_lat, mod_ctx = mod[:n_lat], mod[n_lat:n_lat + 1]

    w_in_b = w_in[0].astype(jnp.bfloat16)
    w_o_b = w_o[0].astype(jnp.bfloat16)
    w_router = jnp.concatenate(
        [w_router_group[0], jnp.zeros((D_MODEL, 8 - N_GROUPS), jnp.float32), w_router_expert[0],
         jnp.zeros((D_MODEL, LANES - 8 - N_EXPERTS), jnp.float32)], axis=1).astype(jnp.bfloat16)

    n_prompt = x_prompt.shape[0] * x_prompt.shape[1]
    n_sample = x_sample.shape[0] * x_sample.shape[1]
    total = n_prompt + n_sample

    def mixer(x, mod_x, per_batch, rope, ctx_kv, token_offset, carried):
        kv_dtype = jnp.bfloat16 if rope else jnp.float32
        q, k, v, yc = _in_proj_call(x, mod_x, per_batch, g_pre1, w_in_b, conv_w[0], rope, kv_dtype)
        kvs = []
        if ctx_kv is not None:
            ck, cv = ctx_kv
            n_ctx = ck.shape[3]
            kvs.append((ck, cv, lambda hp: pl.BlockSpec((None, None, hp, n_ctx, D_V),
                                                        lambda b, h, i: (b, 0, h, 0, 0))))
        seq = x.shape[1]
        kvs.append((k, v, lambda hp: pl.BlockSpec((None, hp, seq, D_V), lambda b, h, i: (b, h, 0, 0))))
        o = _attn_call(lam, subln_g, q, kvs, N_HEADS if seq <= MERGE_HEADS_MAX_SEQ else 1)
        if not per_batch:
            o, yc, x = (a.reshape(1, -1, a.shape[-1]) for a in (o, yc, x))
        shared = _out_proj_call(o, yc, w_o_b, x, mod_x, per_batch, g_post1, g_pre2, w_router,
                                total, token_offset, carried)
        return shared, k, v

    shared, kp, vp = mixer(x_prompt, mod_ctx, False, False, None, 0, None)
    (x1, h2, slots, slots_t, counts), _, _ = mixer(x_sample, mod_lat, True, True, (cache_k, cache_v),
                                                   n_prompt, shared)

    n_tiles = total // DISPATCH_TILE
    max_rows = 2 * total + n_tiles * N_EXPERTS * (CHUNK - 1) + N_EXPERTS * (EXPERT_BLOCK - CHUNK)
    n_blocks = -(-max_rows // EXPERT_BLOCK)
    scatter_rows, gather_rows, tail_start, tail_chunks, block_expert, n_used = _layout_tables(counts, n_blocks)
    xs = _dispatch_call((scatter_rows, tail_start, tail_chunks), h2, slots, n_blocks)
    ys = _expert_call(block_expert, n_used, xs, w_exp_gate[0], w_exp_up[0], w_exp_down[0], n_blocks)
    yp = _combine_call(gather_rows, ys, slots_t, x1, mod_ctx, False, g_post2, 0,
                       x_prompt.shape[0], x_prompt.shape[1])
    ysamp = _combine_call(gather_rows, ys, slots_t, x1, mod_lat, True, g_post2, n_prompt,
                          x_sample.shape[0], x_sample.shape[1])
    return yp, ysamp, kp[:, None], vp[:, None]
```

```python
import functools
import math

import numpy as np
import jax
import jax.numpy as jnp
from jax import lax
from jax.experimental import pallas as pl
from jax.experimental.pallas import tpu as pltpu

D_MODEL = 1024
GRID_W = 64
N_HEADS = 4
D_QK = 64
D_V = 128
ATTN_W = N_HEADS * D_V
CONV_W = D_MODEL - ATTN_W
IN_W = 3 * ATTN_W + 3 * CONV_W
N_GROUPS = 4
EXP_PER_GROUP = 4
N_EXPERTS = N_GROUPS * EXP_PER_GROUP
D_EXPERT = 512
ROPE_BASE = 10000.0
EPS = 1e-6
LAM_INIT = 0.8 - 0.6 * math.exp(-0.3 * 0)

LANES = 128
F32_SUBLANES = 8
BF16_SUBLANES = 16
MOD_ROWS = 16
TOKEN_TILE = 1024
OUT_SUB_ROWS = 256
OUT_TILES = 2
Q_TILE = 2048
KEY_CHUNK = 256
MERGE_HEADS_MAX_SEQ = 512
DISPATCH_TILE = 512
CHUNK = BF16_SUBLANES
TILE_SLOTS = 1280
N_CHUNKS = TILE_SLOTS // CHUNK
SLOT_GROUP = 256
GROUP_CHUNKS = SLOT_GROUP // CHUNK
MOE_TILES = 2
SPARE_SETS = 2 * MOE_TILES
COMBINE_SUB_ROWS = 256
EXPERT_BLOCK = 1024
XS_W = D_MODEL + LANES
QK_SCALE = (1.0 / math.sqrt(D_QK)) * math.log2(math.e)

NT_DIMS = (((1,), (1,)), ((), ()))


def _rms(x):
    return x * lax.rsqrt(jnp.mean(x * x, axis=-1, keepdims=True) + EPS)


def _silu(x):
    return x * (1.0 / (1.0 + jnp.exp(-x)))


def _mod_kernel(cond_ref, w_ref, b_ref, lq1_ref, lk1_ref, lq2_ref, lk2_ref, mod_ref, lam_ref):
    s = _silu(cond_ref[...])
    s_hi = s.astype(jnp.bfloat16)
    s_lo = (s - s_hi.astype(jnp.float32)).astype(jnp.bfloat16)
    w = w_ref[...]
    w_hi = w.astype(jnp.bfloat16)
    w_lo = (w - w_hi.astype(jnp.float32)).astype(jnp.bfloat16)
    both = jnp.dot(jnp.concatenate([s_hi, s_lo], axis=0), w_hi, preferred_element_type=jnp.float32)
    m = both[0:MOD_ROWS] + both[MOD_ROWS:2 * MOD_ROWS] + jnp.dot(s_hi, w_lo, preferred_element_type=jnp.float32)
    mod_ref[...] = m + b_ref[...]
    a = jnp.sum(lq1_ref[...] * lk1_ref[...], axis=-1, keepdims=True)
    b = jnp.sum(lq2_ref[...] * lk2_ref[...], axis=-1, keepdims=True)
    lam_ref[...] = jnp.broadcast_to(jnp.exp(a) - jnp.exp(b) + LAM_INIT, lam_ref.shape)


def _mod_call(cond, w_mod, b_mod, lq1, lk1, lq2, lk2):
    n_col = 6 * D_MODEL
    col_tile = 1536
    small = pl.BlockSpec((1, D_QK), lambda j: (0, 0))
    return pl.pallas_call(
        _mod_kernel,
        grid=(n_col // col_tile,),
        in_specs=[
            pl.BlockSpec((MOD_ROWS, D_MODEL), lambda j: (0, 0)),
            pl.BlockSpec((D_MODEL, col_tile), lambda j: (0, j)),
            pl.BlockSpec((1, col_tile), lambda j: (0, j)),
            small, small, small, small,
        ],
        out_specs=[
            pl.BlockSpec((MOD_ROWS, col_tile), lambda j: (0, j)),
            pl.BlockSpec((1, LANES), lambda j: (0, 0)),
        ],
        out_shape=[
            jax.ShapeDtypeStruct((MOD_ROWS, n_col), jnp.float32),
            jax.ShapeDtypeStruct((1, LANES), jnp.float32),
        ],
        name="mod",
    )(cond, w_mod, b_mod, lq1, lk1, lq2, lk2)


def _in_proj_kernel(rope, n_tiles, x_ref, xp_ref, xn_ref, shift_ref, scale_ref, g_ref, w_ref, cw_ref, *rest):
    if rope:
        cos_ref, sina_ref, sinb_ref, q_ref, k_ref, v_ref, yc_ref = rest
    else:
        q_ref, k_ref, v_ref, yc_ref = rest
    i = pl.program_id(1)
    tm = x_ref.shape[0]
    gain = g_ref[...] * (1.0 + scale_ref[...])
    shift = shift_ref[...]

    def modulate(x):
        return (_rms(x) * gain + shift).astype(jnp.bfloat16)

    h = modulate(x_ref[...])
    h_halo = modulate(jnp.concatenate([xp_ref[...], xn_ref[...]], axis=0))

    def proj(lhs, lo, hi):
        return jnp.dot(lhs, w_ref[:, lo:hi], preferred_element_type=jnp.float32)

    def rot(t):
        return (t * cos_ref[...] + pltpu.roll(t, LANES - 16, axis=1) * sina_ref[...]
                + pltpu.roll(t, 16, axis=1) * sinb_ref[...])

    c0 = 3 * ATTN_W
    h_ext = jnp.concatenate([h, h_halo], axis=0)
    cu_all = proj(h_ext, c0 + CONV_W, c0 + 2 * CONV_W) * proj(h_ext, c0 + 2 * CONV_W, c0 + 3 * CONV_W)
    gb = proj(h, c0, c0 + CONV_W)
    cu = cu_all[0:tm]
    prev_row = jnp.where(i > 0, cu_all[tm + 7:tm + 8], 0.0)
    next_row = jnp.where(i < n_tiles - 1, cu_all[tm + 8:tm + 9], 0.0)
    row = lax.broadcasted_iota(jnp.int32, cu.shape, 0)
    prev = jnp.where(row == 0, prev_row, pltpu.roll(cu, 1, axis=0))
    nxt = jnp.where(row == tm - 1, next_row, pltpu.roll(cu, tm - 1, axis=0))
    conv = cw_ref[0:1, :] * prev + cw_ref[1:2, :] * cu + cw_ref[2:3, :] * nxt
    yc_ref[...] = (gb * conv).astype(yc_ref.dtype)

    zq = proj(h, 0, ATTN_W)
    for hd in range(N_HEADS):
        q = zq[:, hd * D_V:(hd + 1) * D_V]
        q_ref[hd] = ((rot(q) if rope else q) * QK_SCALE).astype(q_ref.dtype)
    zk = proj(h, ATTN_W, 2 * ATTN_W)
    for hd in range(N_HEADS):
        k = zk[:, hd * D_V:(hd + 1) * D_V]
        k_ref[hd] = (rot(k) if rope else k).astype(k_ref.dtype)
    zv = proj(h, 2 * ATTN_W, 3 * ATTN_W)
    for hd in range(N_HEADS):
        v_ref[hd] = zv[:, hd * D_V:(hd + 1) * D_V].astype(v_ref.dtype)


def _rope_tables(seq):
    n_rows = seq // GRID_W
    row = np.repeat(np.arange(n_rows), GRID_W).astype(np.float64)
    col = np.tile(np.arange(GRID_W), n_rows).astype(np.float64)
    nf = D_QK // 4
    inv = ROPE_BASE ** (-np.arange(nf, dtype=np.float64) / nf)
    ar = row[:, None] * inv
    ac = col[:, None] * inv
    ang = np.concatenate([ar, ar, ac, ac], axis=-1)
    ang = np.concatenate([ang, ang], axis=-1)
    first_half = (np.arange(LANES) % 32) < 16
    cos = np.cos(ang)
    sin = np.sin(ang)
    sina = np.where(first_half, -sin, 0.0)
    sinb = np.where(first_half, 0.0, sin)
    return tuple(jnp.asarray(t, dtype=jnp.float32) for t in (cos, sina, sinb))


def _in_proj_call(x, mod, mod_per_batch, g_pre1, w_in, conv_w, rope, kv_dtype):
    bsz, seq, _ = x.shape
    tm = min(TOKEN_TILE, seq)
    n_tiles = seq // tm
    halo = tm // F32_SUBLANES
    n_halo = seq // F32_SUBLANES

    def mod_spec(chunk):
        if mod_per_batch:
            return pl.BlockSpec((None, 1, D_MODEL), lambda b, i: (b, 0, chunk))
        return pl.BlockSpec((None, 1, D_MODEL), lambda b, i: (0, 0, chunk))

    in_specs = [
        pl.BlockSpec((None, tm, D_MODEL), lambda b, i: (b, i, 0)),
        pl.BlockSpec((None, F32_SUBLANES, D_MODEL), lambda b, i: (b, jnp.maximum(i * halo - 1, 0), 0)),
        pl.BlockSpec((None, F32_SUBLANES, D_MODEL), lambda b, i: (b, jnp.minimum((i + 1) * halo, n_halo - 1), 0)),
        mod_spec(0), mod_spec(1),
        pl.BlockSpec((1, D_MODEL), lambda b, i: (0, 0)),
        pl.BlockSpec((D_MODEL, IN_W), lambda b, i: (0, 0)),
        pl.BlockSpec((3, CONV_W), lambda b, i: (0, 0)),
    ]
    args = [x, x, x, mod, mod, g_pre1, w_in, conv_w]
    if rope:
        in_specs += [pl.BlockSpec((tm, LANES), lambda b, i: (i, 0))] * 3
        args += list(_rope_tables(seq))
    head_spec = pl.BlockSpec((None, N_HEADS, tm, D_V), lambda b, i: (b, 0, i, 0))
    return pl.pallas_call(
        functools.partial(_in_proj_kernel, rope, n_tiles),
        grid=(bsz, n_tiles),
        in_specs=in_specs,
        out_specs=[head_spec, head_spec, head_spec, pl.BlockSpec((None, tm, CONV_W), lambda b, i: (b, i, 0))],
        out_shape=[
            jax.ShapeDtypeStruct((bsz, N_HEADS, seq, D_V), jnp.bfloat16),
            jax.ShapeDtypeStruct((bsz, N_HEADS, seq, D_V), kv_dtype),
            jax.ShapeDtypeStruct((bsz, N_HEADS, seq, D_V), kv_dtype),
            jax.ShapeDtypeStruct((bsz, seq, CONV_W), jnp.bfloat16),
        ],
        name="in_proj_rope" if rope else "in_proj",
    )(*args)


def _attn_kernel(n_kv, lam_ref, g_ref, q_ref, *rest):
    kv_refs = rest[:2 * n_kv]
    o_ref = rest[2 * n_kv]
    heads, tq, _ = q_ref.shape
    lane = lax.broadcasted_iota(jnp.int32, (tq, 2 * D_QK), 1)
    for hh in range(heads):
        q = q_ref[hh]
        zero = jnp.zeros_like(q)
        halves = (jnp.where(lane < D_QK, q, zero), jnp.where(lane >= D_QK, q, zero))
        acc = [jnp.zeros((tq, 2 * D_V), jnp.float32) for _ in range(2)]
        m = [jnp.full((tq, 1), -1e30, jnp.float32) for _ in range(2)]
        for j in range(n_kv):
            k_ref, v_ref = kv_refs[2 * j], kv_refs[2 * j + 1]
            n_keys = k_ref.shape[1]
            ck = min(KEY_CHUNK, n_keys)
            ones = jnp.ones((ck, D_V), jnp.bfloat16)
            for c in range(n_keys // ck):
                k = k_ref[hh, c * ck:(c + 1) * ck, :].astype(jnp.bfloat16)
                v1 = jnp.concatenate([v_ref[hh, c * ck:(c + 1) * ck, :].astype(jnp.bfloat16), ones], axis=1)
                for x, qh in enumerate(halves):
                    s = lax.dot_general(qh, k, NT_DIMS, preferred_element_type=jnp.float32)
                    m_new = jnp.maximum(m[x], jnp.max(s, axis=-1, keepdims=True))
                    p = jnp.exp2(s - m_new).astype(jnp.bfloat16)
                    acc[x] = (jnp.exp2(m[x] - m_new) * acc[x]
                              + jnp.dot(p, v1, preferred_element_type=jnp.float32))
                    m[x] = m_new
        o = (acc[0][:, 0:D_V] / acc[0][:, D_V:2 * D_V]
             - lam_ref[0:1, 0:1] * (acc[1][:, 0:D_V] / acc[1][:, D_V:2 * D_V]))
        o_ref[:, hh * D_V:(hh + 1) * D_V] = (_rms(o) * (g_ref[...] * (1.0 - LAM_INIT))).astype(o_ref.dtype)


def _attn_call(lam, subln_g, q, kvs, heads_per_step):
    bsz, _, seq, _ = q.shape
    tq = min(Q_TILE, seq)
    hp = heads_per_step
    in_specs = [
        pl.BlockSpec((1, LANES), lambda b, h, i: (0, 0)),
        pl.BlockSpec((1, D_V), lambda b, h, i: (0, 0)),
        pl.BlockSpec((None, hp, tq, D_V), lambda b, h, i: (b, h, i, 0)),
    ]
    args = [lam, subln_g, q]
    for k, v, spec_fn in kvs:
        in_specs += [spec_fn(hp), spec_fn(hp)]
        args += [k, v]
    return pl.pallas_call(
        functools.partial(_attn_kernel, len(kvs)),
        grid=(bsz, N_HEADS // hp, seq // tq),
        in_specs=in_specs,
        out_specs=pl.BlockSpec((None, tq, hp * D_V), lambda b, h, i: (b, i, h)),
        out_shape=jax.ShapeDtypeStruct((bsz, seq, ATTN_W), jnp.bfloat16),
        name="attn%d" % len(kvs),
    )(*args)


def _route(logits_t):
    lg = [logits_t[g:g + 1, :] for g in range(N_GROUPS)]
    mg = functools.reduce(jnp.maximum, lg)
    p_sel = 1.0 / functools.reduce(jnp.add, [jnp.exp(t - mg) for t in lg])
    g_sel = jnp.full(mg.shape, N_GROUPS - 1, jnp.int32)
    for g in range(N_GROUPS - 2, -1, -1):
        g_sel = jnp.where(lg[g] == mg, g, g_sel)
    le = []
    for j in range(EXP_PER_GROUP):
        t = jnp.zeros_like(mg)
        for g in range(N_GROUPS):
            r = 8 + g * EXP_PER_GROUP + j
            t = jnp.where(g_sel == g, logits_t[r:r + 1, :], t)
        le.append(t)
    m1 = functools.reduce(jnp.maximum, le)
    i1 = jnp.full(mg.shape, EXP_PER_GROUP - 1, jnp.int32)
    for j in range(EXP_PER_GROUP - 2, -1, -1):
        i1 = jnp.where(le[j] == m1, j, i1)
    neg = jnp.float32(-jnp.inf)
    rest = [jnp.where(i1 == j, neg, le[j]) for j in range(EXP_PER_GROUP)]
    m2 = functools.reduce(jnp.maximum, rest)
    i2 = jnp.full(mg.shape, EXP_PER_GROUP - 1, jnp.int32)
    for j in range(EXP_PER_GROUP - 2, -1, -1):
        i2 = jnp.where(rest[j] == m2, j, i2)
    e2 = jnp.exp(m2 - m1)
    w1 = p_sel / (1.0 + e2)
    w2 = p_sel * e2 / (1.0 + e2)
    base = g_sel * EXP_PER_GROUP
    return (base + i1).astype(jnp.float32), (base + i2).astype(jnp.float32), w1, w2


def _out_proj_kernel(o_ref, yc_ref, wo_ref, x_ref, gate1_ref, shift2_ref, scale2_ref, gpost_ref, gpre_ref,
                     wr_ref, *rest):
    x1_ref, h2_ref, slots_ref, slots_t_ref, counts_ref = rest[-5:]
    tm = o_ref.shape[0]
    sub = min(OUT_SUB_ROWS, tm)
    gain1 = gate1_ref[...] * gpost_ref[...]
    gain2 = gpre_ref[...] * (1.0 + scale2_ref[...])
    routes = []
    for r in range(tm // sub):
        rows = slice(r * sub, (r + 1) * sub)
        out = (jnp.dot(o_ref[rows, :], wo_ref[0:ATTN_W, :], preferred_element_type=jnp.float32)
               + jnp.dot(yc_ref[rows, :], wo_ref[ATTN_W:D_MODEL, :], preferred_element_type=jnp.float32))
        x1 = x_ref[rows, :] + _rms(out) * gain1
        x1_ref[rows, :] = x1
        h2 = (_rms(x1) * gain2 + shift2_ref[...]).astype(jnp.bfloat16)
        h2_ref[rows, :] = h2
        logits = jnp.dot(h2, wr_ref[...], preferred_element_type=jnp.float32)
        routes.append(_route(logits.T))
    per_tile = DISPATCH_TILE // sub
    for d in range(tm // DISPATCH_TILE):
        e1, e2, w1, w2 = (jnp.concatenate(parts, axis=1)
                          for parts in zip(*routes[d * per_tile:(d + 1) * per_tile]))
        _plan(e1, e2, w1, w2, slots_ref.at[d], slots_t_ref.at[d], counts_ref.at[d])


def _out_proj_call(o, yc, w_o, x, mod, mod_per_batch, g_post1, g_pre2, w_router,
                   total_tokens, token_offset, carried):
    bsz, seq, _ = x.shape
    tm = OUT_TILES * DISPATCH_TILE
    assert seq % tm == 0 and token_offset % tm == 0 and total_tokens % tm == 0
    n_tiles = seq // tm
    tile0 = token_offset // tm
    n_all = total_tokens // DISPATCH_TILE

    def mod_spec(chunk):
        if mod_per_batch:
            return pl.BlockSpec((None, 1, D_MODEL), lambda b, i: (b, 0, chunk))
        return pl.BlockSpec((None, 1, D_MODEL), lambda b, i: (0, 0, chunk))

    def vec_spec():
        return pl.BlockSpec((1, D_MODEL), lambda b, i: (0, 0))

    tok = lambda w: pl.BlockSpec((None, tm, w), lambda b, i: (b, i, 0))
    flat = lambda w: pl.BlockSpec((tm, w), lambda b, i: (tile0 + b * n_tiles + i, 0))
    per_tile = lambda r, c: pl.BlockSpec((OUT_TILES, r, c), lambda b, i: (tile0 + b * n_tiles + i, 0, 0))
    in_specs = [
        tok(ATTN_W), tok(CONV_W),
        pl.BlockSpec((D_MODEL, D_MODEL), lambda b, i: (0, 0)),
        tok(D_MODEL),
        mod_spec(2), mod_spec(3), mod_spec(4),
        vec_spec(), vec_spec(),
        pl.BlockSpec((D_MODEL, LANES), lambda b, i: (0, 0)),
    ]
    args = [o, yc, w_o, x, mod, mod, mod, g_post1, g_pre2, w_router]
    aliases = {}
    if carried is not None:
        for j, arr in enumerate(carried):
            aliases[len(args)] = j
            in_specs.append(pl.BlockSpec(memory_space=pl.ANY))
            args.append(arr)
    return pl.pallas_call(
        _out_proj_kernel,
        grid=(bsz, n_tiles),
        in_specs=in_specs,
        out_specs=[flat(D_MODEL), flat(D_MODEL), per_tile(8, DISPATCH_TILE), per_tile(DISPATCH_TILE, LANES),
                   per_tile(N_EXPERTS, LANES)],
        out_shape=[
            jax.ShapeDtypeStruct((total_tokens, D_MODEL), jnp.float32),
            jax.ShapeDtypeStruct((total_tokens, D_MODEL), jnp.bfloat16),
            jax.ShapeDtypeStruct((n_all, 8, DISPATCH_TILE), jnp.float32),
            jax.ShapeDtypeStruct((n_all, DISPATCH_TILE, LANES), jnp.float32),
            jax.ShapeDtypeStruct((n_all, N_EXPERTS, LANES), jnp.float32),
        ],
        input_output_aliases=aliases,
        name="out_proj",
    )(*args)


def _plan(e1, e2, w1, w2, slots_ref, slots_t_ref, counts_ref):
    t = e1.shape[1]
    e1 = e1.astype(jnp.int32)
    e2 = e2.astype(jnp.int32)
    eid = lax.broadcasted_iota(jnp.int32, (N_EXPERTS, t), 0)
    hot1 = jnp.where(eid == e1, 1.0, 0.0)
    hot2 = jnp.where(eid == e2, 1.0, 0.0)
    hot = jnp.concatenate([hot1, hot2], axis=0).astype(jnp.bfloat16)
    before = (lax.broadcasted_iota(jnp.int32, (t, t), 0) < lax.broadcasted_iota(jnp.int32, (t, t), 1))
    before = jnp.where(before, 1.0, 0.0).astype(jnp.bfloat16)
    rank = jnp.dot(hot, before, preferred_element_type=jnp.float32)
    n1 = jnp.sum(hot1, axis=1, keepdims=True)
    n2 = jnp.sum(hot2, axis=1, keepdims=True)
    ones = jnp.ones((8, t), jnp.bfloat16)
    cnt_row = lax.dot_general(ones, (hot1 + hot2).astype(jnp.bfloat16), NT_DIMS,
                              preferred_element_type=jnp.float32)[0:1, :]
    pad_row = jnp.floor((cnt_row + (CHUNK - 1)) * (1.0 / CHUNK)) * CHUNK
    lower = (lax.broadcasted_iota(jnp.int32, (N_EXPERTS, N_EXPERTS), 1)
             < lax.broadcasted_iota(jnp.int32, (N_EXPERTS, N_EXPERTS), 0))
    base = jnp.sum(jnp.where(lower, pad_row, 0.0), axis=1, keepdims=True)
    slot1 = jnp.sum(hot1 * (base + rank[:N_EXPERTS]), axis=0, keepdims=True)
    slot2 = jnp.sum(hot2 * (base + n1 + rank[N_EXPERTS:]), axis=0, keepdims=True)
    slots_ref[...] = jnp.concatenate([slot1, slot2, w1, w2, jnp.zeros((4, t), jnp.float32)], axis=0)
    wide = jnp.concatenate([slot1, slot2, jnp.zeros((LANES - 2, t), jnp.float32)], axis=0)
    slots_t_ref[...] = wide.T
    counts_ref[...] = jnp.broadcast_to(n1 + n2, counts_ref.shape)


def _layout_tables(counts, n_blocks):
    cnt = counts[:, :, 0].astype(jnp.int32)
    n_tiles = cnt.shape[0]
    nch = (cnt + (CHUNK - 1)) // CHUNK
    csum = jnp.cumsum(nch, axis=0)
    rows = CHUNK * csum[-1]
    blocks = (rows + (EXPERT_BLOCK - 1)) // EXPERT_BLOCK
    blk_end = jnp.cumsum(blocks)
    blk_start = blk_end - blocks
    off = blk_start[None, :] * EXPERT_BLOCK + CHUNK * (csum - nch)
    first = jnp.cumsum(nch, axis=1) - nch
    used = jnp.sum(nch, axis=1)
    c = jnp.arange(N_CHUNKS, dtype=jnp.int32)
    e_idx = jnp.sum((c[None, :, None] >= first[:, None, :]).astype(jnp.int32), axis=2) - 1
    hot = e_idx[:, :, None] == jnp.arange(N_EXPERTS, dtype=jnp.int32)[None, None, :]
    pick = lambda a: jnp.sum(jnp.where(hot, a[:, None, :], 0), axis=2)
    row = pick(off) + CHUNK * (c[None, :] - pick(first))
    valid = c[None, :] < used[:, None]
    spare = (n_blocks * EXPERT_BLOCK + (jnp.arange(n_tiles, dtype=jnp.int32) % SPARE_SETS)[:, None] * TILE_SLOTS
             + CHUNK * c[None, :])
    scatter_rows = jnp.where(valid, row, spare)
    gather_rows = jnp.where(valid, row, 0)
    n_used = blk_end[-1]
    bc = jnp.minimum(jnp.arange(n_blocks, dtype=jnp.int32), n_used - 1)
    block_expert = jnp.sum((bc[:, None] >= blk_end[None, :]).astype(jnp.int32), axis=1)
    tail_start = blk_start * EXPERT_BLOCK + rows
    tail_chunks = (blocks * EXPERT_BLOCK - rows) // CHUNK
    flat = lambda a: a.reshape(-1).astype(jnp.int32)
    return (flat(scatter_rows // CHUNK), flat(gather_rows // CHUNK), flat(tail_start // CHUNK), flat(tail_chunks),
            flat(block_expert), flat(n_used))


def _chunk_copies(tile, chunk_ref, make_copy, action):
    for c in range(N_CHUNKS):
        action(make_copy(c, chunk_ref[tile * N_CHUNKS + c]))


def _wait_chunk_copies(make_copy):
    for _ in range(N_CHUNKS):
        make_copy(0, 0).wait()


def _dispatch_kernel(rows_ref, tail_start_ref, tail_chunks_ref, h_ref, slots_ref, xs_ref, buf, zbuf, sem, zsem):
    i = pl.program_id(0)
    n = pl.num_programs(0)
    slot = i % 2
    t = DISPATCH_TILE
    lane = lax.broadcasted_iota(jnp.int32, (SLOT_GROUP, LANES), 1)

    def copies(sl, d):
        def make(src_piece, dst_piece):
            return pltpu.make_async_copy(buf.at[sl, d, src_piece], xs_ref.at[dst_piece], sem.at[sl])
        return make

    for d in range(MOE_TILES):
        s = slots_ref[d]
        s1 = s[0:1, :].astype(jnp.int32)
        s2 = s[1:2, :].astype(jnp.int32)
        h = h_ref[d * t:(d + 1) * t, :]
        for g in range(TILE_SLOTS // SLOT_GROUP):
            rid = lax.broadcasted_iota(jnp.int32, (SLOT_GROUP, t), 0) + g * SLOT_GROUP
            m1 = rid == s1
            m2 = rid == s2
            perm = jnp.where(m1, 1.0, jnp.where(m2, 1.0, 0.0)).astype(jnp.bfloat16)
            rows = jnp.dot(perm, h, preferred_element_type=jnp.float32)
            w = jnp.sum(jnp.where(m1, s[2:3, :], 0.0) + jnp.where(m2, s[3:4, :], 0.0), axis=1, keepdims=True)
            hi = w.astype(jnp.bfloat16).astype(jnp.float32)
            aux = jnp.where(lane == 0, hi, jnp.where(lane == 1, w - hi, 0.0))
            pieces = slice(g * GROUP_CHUNKS, (g + 1) * GROUP_CHUNKS)
            buf[slot, d, pieces, :, 0:D_MODEL] = rows.astype(jnp.bfloat16).reshape(GROUP_CHUNKS, CHUNK, D_MODEL)
            buf[slot, d, pieces, :, D_MODEL:XS_W] = aux.astype(jnp.bfloat16).reshape(GROUP_CHUNKS, CHUNK, LANES)
        _chunk_copies(i * MOE_TILES + d, rows_ref, copies(slot, d), lambda c: c.start())

    @pl.when(i > 0)
    def _():
        for d in range(MOE_TILES):
            _wait_chunk_copies(copies(1 - slot, d))

    @pl.when(i == n - 1)
    def _():
        zbuf[...] = jnp.zeros_like(zbuf)

        def tails(action):
            for e in range(N_EXPERTS):
                start = tail_start_ref[e]

                def body(m, carry, start=start):
                    action(pltpu.make_async_copy(zbuf, xs_ref.at[start + m], zsem))
                    return carry

                lax.fori_loop(0, tail_chunks_ref[e], body, 0)

        tails(lambda c: c.start())
        tails(lambda c: c.wait())
        for d in range(MOE_TILES):
            _wait_chunk_copies(copies(slot, d))


def _dispatch_call(tables, h2, slots, n_blocks):
    scatter_rows, tail_start, tail_chunks = tables
    tokens = h2.shape[0]
    t = MOE_TILES * DISPATCH_TILE
    return pl.pallas_call(
        _dispatch_kernel,
        grid_spec=pltpu.PrefetchScalarGridSpec(
            num_scalar_prefetch=3,
            grid=(tokens // t,),
            in_specs=[
                pl.BlockSpec((t, D_MODEL), lambda i, *_: (i, 0)),
                pl.BlockSpec((MOE_TILES, 8, DISPATCH_TILE), lambda i, *_: (i, 0, 0)),
            ],
            out_specs=pl.BlockSpec(memory_space=pl.ANY),
            scratch_shapes=[
                pltpu.VMEM((2, MOE_TILES, N_CHUNKS, CHUNK, XS_W), jnp.bfloat16),
                pltpu.VMEM((CHUNK, XS_W), jnp.bfloat16),
                pltpu.SemaphoreType.DMA((2,)),
                pltpu.SemaphoreType.DMA(()),
            ],
        ),
        out_shape=jax.ShapeDtypeStruct(((n_blocks * EXPERT_BLOCK + SPARE_SETS * TILE_SLOTS) // CHUNK, CHUNK, XS_W),
                                       jnp.bfloat16),
        name="moe_dispatch",
    )(scatter_rows, tail_start, tail_chunks, h2, slots)


def _expert_kernel(bexp_ref, nused_ref, xs_ref, wg_ref, wu_ref, wd_ref, ys_ref, wgu_b, wd_b):
    b = pl.program_id(0)
    e = bexp_ref[b]
    prev = bexp_ref[jnp.maximum(b - 1, 0)]

    @pl.when((b == 0) | (e != prev))
    def _():
        wgu_b[:, 0:D_EXPERT] = wg_ref[...].astype(jnp.bfloat16)
        wgu_b[:, D_EXPERT:2 * D_EXPERT] = wu_ref[...].astype(jnp.bfloat16)
        wd_b[...] = wd_ref[...].astype(jnp.bfloat16)

    @pl.when(b < nused_ref[0])
    def _():
        x = xs_ref[:, 0:D_MODEL]
        y = None
        half = D_EXPERT // 2
        for c in range(2):
            cols = slice(c * half, (c + 1) * half)
            g = jnp.dot(x, wgu_b[:, c * half:(c + 1) * half], preferred_element_type=jnp.float32)
            u = jnp.dot(x, wgu_b[:, D_EXPERT + c * half:D_EXPERT + (c + 1) * half],
                        preferred_element_type=jnp.float32)
            act = (_silu(g) * u).astype(jnp.bfloat16)
            t = jnp.dot(act, wd_b[cols, :], preferred_element_type=jnp.float32)
            y = t if y is None else y + t
        w = (xs_ref[:, D_MODEL:D_MODEL + 1].astype(jnp.float32)
             + xs_ref[:, D_MODEL + 1:D_MODEL + 2].astype(jnp.float32))
        ys_ref[...] = (w * y).astype(jnp.bfloat16)


def _expert_call(block_expert, n_used, xs, w_eg, w_eu, w_ed, n_blocks):
    row_blk = lambda w: pl.BlockSpec((EXPERT_BLOCK, w), lambda b, be, nu: (jnp.minimum(b, nu[0] - 1), 0))
    wt_blk = lambda r, c: pl.BlockSpec((None, r, c), lambda b, be, nu: (be[b], 0, 0))
    return pl.pallas_call(
        _expert_kernel,
        grid_spec=pltpu.PrefetchScalarGridSpec(
            num_scalar_prefetch=2,
            grid=(n_blocks,),
            in_specs=[
                row_blk(XS_W),
                wt_blk(D_MODEL, D_EXPERT), wt_blk(D_MODEL, D_EXPERT), wt_blk(D_EXPERT, D_MODEL),
            ],
            out_specs=row_blk(D_MODEL),
            scratch_shapes=[
                pltpu.VMEM((D_MODEL, 2 * D_EXPERT), jnp.bfloat16),
                pltpu.VMEM((D_EXPERT, D_MODEL), jnp.bfloat16),
            ],
        ),
        out_shape=jax.ShapeDtypeStruct((n_blocks * EXPERT_BLOCK, D_MODEL), jnp.bfloat16),
        name="moe_experts",
    )(block_expert, n_used, xs, w_eg, w_eu, w_ed)


def _combine_kernel(tile0, rows_ref, ys_ref, slots_t_ref, x1_ref, gate2_ref, gpost_ref, out_ref, buf, sem):
    j = pl.program_id(0)
    n = pl.num_programs(0)
    slot = j % 2
    tile = tile0 + j * MOE_TILES

    def copies(sl, d):
        def make(dst_piece, src_piece):
            return pltpu.make_async_copy(ys_ref.at[src_piece], buf.at[sl, d, dst_piece], sem.at[sl])
        return make

    def fetch(first_tile, sl):
        for d in range(MOE_TILES):
            _chunk_copies(first_tile + d, rows_ref, copies(sl, d), lambda c: c.start())

    @pl.when(j == 0)
    def _():
        fetch(tile, slot)

    @pl.when(j + 1 < n)
    def _():
        fetch(tile + MOE_TILES, 1 - slot)

    for d in range(MOE_TILES):
        _wait_chunk_copies(copies(slot, d))
    gain = gate2_ref[...] * gpost_ref[...]
    lane = lax.broadcasted_iota(jnp.int32, (COMBINE_SUB_ROWS, TILE_SLOTS), 1)
    per_tile = DISPATCH_TILE // COMBINE_SUB_ROWS
    for d in range(MOE_TILES):
        sorted_rows = buf[slot, d].reshape(TILE_SLOTS, D_MODEL)
        for r in range(per_tile):
            rows = slice((d * per_tile + r) * COMBINE_SUB_ROWS, (d * per_tile + r + 1) * COMBINE_SUB_ROWS)
            local = slice(r * COMBINE_SUB_ROWS, (r + 1) * COMBINE_SUB_ROWS)
            s1 = slots_t_ref[d, local, 0:1].astype(jnp.int32)
            s2 = slots_t_ref[d, local, 1:2].astype(jnp.int32)
            unperm = jnp.where(lane == s1, 1.0, jnp.where(lane == s2, 1.0, 0.0)).astype(jnp.bfloat16)
            y = jnp.dot(unperm, sorted_rows, preferred_element_type=jnp.float32)
            out_ref[rows, :] = x1_ref[rows, :] + _rms(y) * gain


def _combine_call(gather_rows, ys, slots_t, x1, mod, mod_per_batch, g_post2, token_offset, bsz, seq):
    t = MOE_TILES * DISPATCH_TILE
    tile0 = token_offset // DISPATCH_TILE
    step0 = token_offset // t
    tokens = bsz * seq
    assert tokens % t == 0 and token_offset % t == 0 and (seq % t == 0 or not mod_per_batch)
    if mod_per_batch:
        per = seq // t
        gate2_spec = pl.BlockSpec((None, 1, D_MODEL), lambda j, *_: (j // per, 0, 5))
    else:
        gate2_spec = pl.BlockSpec((None, 1, D_MODEL), lambda j, *_: (0, 0, 5))
    y = pl.pallas_call(
        functools.partial(_combine_kernel, tile0),
        grid_spec=pltpu.PrefetchScalarGridSpec(
            num_scalar_prefetch=1,
            grid=(tokens // t,),
            in_specs=[
                pl.BlockSpec(memory_space=pl.ANY),
                pl.BlockSpec((MOE_TILES, DISPATCH_TILE, LANES), lambda j, *_: (step0 + j, 0, 0)),
                pl.BlockSpec((t, D_MODEL), lambda j, *_: (step0 + j, 0)),
                gate2_spec,
                pl.BlockSpec((1, D_MODEL), lambda j, *_: (0, 0)),
            ],
            out_specs=pl.BlockSpec((t, D_MODEL), lambda j, *_: (j, 0)),
            scratch_shapes=[
                pltpu.VMEM((2, MOE_TILES, N_CHUNKS, CHUNK, D_MODEL), jnp.bfloat16),
                pltpu.SemaphoreType.DMA((2,)),
            ],
        ),
        out_shape=jax.ShapeDtypeStruct((tokens, D_MODEL), jnp.float32),
        name="moe_combine",
    )(gather_rows, ys.reshape(-1, CHUNK, D_MODEL), slots_t, x1, mod, g_post2)
    return y.reshape(bsz, seq, D_MODEL)


def kernel(x_prompt, x_sample, cache_k, cache_v, c, c_ctx, w_mod, b_mod, g_pre1, g_post1, g_pre2, g_post2,
           w_in, conv_w, lambda_q1, lambda_k1, lambda_q2, lambda_k2, subln_g, w_o, w_router_group,
           w_router_expert, w_exp_gate, w_exp_up, w_exp_down):
    n_lat = c.shape[0]
    cond = jnp.concatenate(
        [c, c_ctx[None, :], jnp.zeros((MOD_ROWS - n_lat - 1, D_MODEL), jnp.float32)], axis=0)
    mod, lam = _mod_call(cond, w_mod[0], b_mod, lambda_q1, lambda_k1, lambda_q2, lambda_k2)
    mod = mod.reshape(MOD_ROWS, 1, 6 * D_MODEL)
    mod_lat, mod_ctx = mod[:n_lat], mod[n_lat:n_lat + 1]

    w_in_b = w_in[0].astype(jnp.bfloat16)
    w_o_b = w_o[0].astype(jnp.bfloat16)
    w_router = jnp.concatenate(
        [w_router_group[0], jnp.zeros((D_MODEL, 8 - N_GROUPS), jnp.float32), w_router_expert[0],
         jnp.zeros((D_MODEL, LANES - 8 - N_EXPERTS), jnp.float32)], axis=1).astype(jnp.bfloat16)

    n_prompt = x_prompt.shape[0] * x_prompt.shape[1]
    n_sample = x_sample.shape[0] * x_sample.shape[1]
    total = n_prompt + n_sample

    def mixer(x, mod_x, per_batch, rope, ctx_kv, token_offset, carried):
        kv_dtype = jnp.bfloat16 if rope else jnp.float32
        q, k, v, yc = _in_proj_call(x, mod_x, per_batch, g_pre1, w_in_b, conv_w[0], rope, kv_dtype)
        kvs = []
        if ctx_kv is not None:
            ck, cv = ctx_kv
            n_ctx = ck.shape[3]
            kvs.append((ck, cv, lambda hp: pl.BlockSpec((None, None, hp, n_ctx, D_V),
                                                        lambda b, h, i: (b, 0, h, 0, 0))))
        seq = x.shape[1]
        kvs.append((k, v, lambda hp: pl.BlockSpec((None, hp, seq, D_V), lambda b, h, i: (b, h, 0, 0))))
        o = _attn_call(lam, subln_g, q, kvs, N_HEADS if seq <= MERGE_HEADS_MAX_SEQ else 1)
        if not per_batch:
            o, yc, x = (a.reshape(1, -1, a.shape[-1]) for a in (o, yc, x))
        shared = _out_proj_call(o, yc, w_o_b, x, mod_x, per_batch, g_post1, g_pre2, w_router,
                                total, token_offset, carried)
        return shared, k, v

    shared, kp, vp = mixer(x_prompt, mod_ctx, False, False, None, 0, None)
    (x1, h2, slots, slots_t, counts), _, _ = mixer(x_sample, mod_lat, True, True, (cache_k, cache_v),
                                                   n_prompt, shared)

    n_tiles = total // DISPATCH_TILE
    max_rows = 2 * total + n_tiles * N_EXPERTS * (CHUNK - 1) + N_EXPERTS * (EXPERT_BLOCK - CHUNK)
    n_blocks = -(-max_rows // EXPERT_BLOCK)
    scatter_rows, gather_rows, tail_start, tail_chunks, block_expert, n_used = _layout_tables(counts, n_blocks)
    xs = _dispatch_call((scatter_rows, tail_start, tail_chunks), h2, slots, n_blocks)
    ys = _expert_call(block_expert, n_used, xs.reshape(-1, XS_W), w_exp_gate[0], w_exp_up[0], w_exp_down[0],
                      n_blocks)
    yp = _combine_call(gather_rows, ys, slots_t, x1, mod_ctx, False, g_post2, 0,
                       x_prompt.shape[0], x_prompt.shape[1])
    ysamp = _combine_call(gather_rows, ys, slots_t, x1, mod_lat, True, g_post2, n_prompt,
                          x_sample.shape[0], x_sample.shape[1])
    return yp, ysamp, kp[:, None], vp[:, None]
```

```python
import functools
import math

import numpy as np
import jax
import jax.numpy as jnp
from jax import lax
from jax.experimental import pallas as pl
from jax.experimental.pallas import tpu as pltpu

D_MODEL = 1024
GRID_W = 64
N_HEADS = 4
D_QK = 64
D_V = 128
ATTN_W = N_HEADS * D_V
CONV_W = D_MODEL - ATTN_W
IN_W = 3 * ATTN_W + 3 * CONV_W
N_GROUPS = 4
EXP_PER_GROUP = 4
N_EXPERTS = N_GROUPS * EXP_PER_GROUP
D_EXPERT = 512
ROPE_BASE = 10000.0
EPS = 1e-6
LAM_INIT = 0.8 - 0.6 * math.exp(-0.3 * 0)

LANES = 128
F32_SUBLANES = 8
BF16_SUBLANES = 16
MOD_ROWS = 16
TOKEN_TILE = 1024
OUT_SUB_ROWS = 256
OUT_TILES = 2
Q_TILE = 2048
KEY_CHUNK = 256
MERGE_HEADS_MAX_SEQ = 512
SHORT_SEQS_PER_STEP = 4
DISPATCH_TILE = 512
CHUNK = BF16_SUBLANES
TILE_SLOTS = 1280
N_CHUNKS = TILE_SLOTS // CHUNK
SLOT_GROUP = 256
GROUP_CHUNKS = SLOT_GROUP // CHUNK
MOE_TILES = 2
SPARE_SETS = 2 * MOE_TILES
COMBINE_SUB_ROWS = 256
EXPERT_BLOCK = 1024
XS_W = D_MODEL + LANES
QK_SCALE = (1.0 / math.sqrt(D_QK)) * math.log2(math.e)

NT_DIMS = (((1,), (1,)), ((), ()))


def _rms(x):
    return x * lax.rsqrt(jnp.mean(x * x, axis=-1, keepdims=True) + EPS)


def _silu(x):
    return x * (1.0 / (1.0 + jnp.exp(-x)))


def _mod_kernel(cond_ref, w_ref, b_ref, lq1_ref, lk1_ref, lq2_ref, lk2_ref, mod_ref, lam_ref):
    s = _silu(cond_ref[...])
    s_hi = s.astype(jnp.bfloat16)
    s_lo = (s - s_hi.astype(jnp.float32)).astype(jnp.bfloat16)
    w = w_ref[...]
    w_hi = w.astype(jnp.bfloat16)
    w_lo = (w - w_hi.astype(jnp.float32)).astype(jnp.bfloat16)
    both = jnp.dot(jnp.concatenate([s_hi, s_lo], axis=0), w_hi, preferred_element_type=jnp.float32)
    m = both[0:MOD_ROWS] + both[MOD_ROWS:2 * MOD_ROWS] + jnp.dot(s_hi, w_lo, preferred_element_type=jnp.float32)
    mod_ref[...] = m + b_ref[...]
    a = jnp.sum(lq1_ref[...] * lk1_ref[...], axis=-1, keepdims=True)
    b = jnp.sum(lq2_ref[...] * lk2_ref[...], axis=-1, keepdims=True)
    lam_ref[...] = jnp.broadcast_to(jnp.exp(a) - jnp.exp(b) + LAM_INIT, lam_ref.shape)


def _mod_call(cond, w_mod, b_mod, lq1, lk1, lq2, lk2):
    n_col = 6 * D_MODEL
    col_tile = 1536
    small = pl.BlockSpec((1, D_QK), lambda j: (0, 0))
    return pl.pallas_call(
        _mod_kernel,
        grid=(n_col // col_tile,),
        in_specs=[
            pl.BlockSpec((MOD_ROWS, D_MODEL), lambda j: (0, 0)),
            pl.BlockSpec((D_MODEL, col_tile), lambda j: (0, j)),
            pl.BlockSpec((1, col_tile), lambda j: (0, j)),
            small, small, small, small,
        ],
        out_specs=[
            pl.BlockSpec((MOD_ROWS, col_tile), lambda j: (0, j)),
            pl.BlockSpec((1, LANES), lambda j: (0, 0)),
        ],
        out_shape=[
            jax.ShapeDtypeStruct((MOD_ROWS, n_col), jnp.float32),
            jax.ShapeDtypeStruct((1, LANES), jnp.float32),
        ],
        name="mod",
    )(cond, w_mod, b_mod, lq1, lk1, lq2, lk2)


def _in_proj_kernel(rope, n_tiles, seq_len, x_ref, xp_ref, xn_ref, shift_ref, scale_ref, g_ref, w_ref, cw_ref,
                    *rest):
    if rope:
        cos_ref, sina_ref, sinb_ref, q_ref, k_ref, v_ref, yc_ref = rest
    else:
        q_ref, k_ref, v_ref, yc_ref = rest
    i = pl.program_id(1)
    tm = x_ref.shape[0]
    spt = q_ref.shape[0]
    per_seq = tm // spt
    gain = g_ref[...] * (1.0 + scale_ref[...])
    shift = shift_ref[...]

    def modulate(x):
        return (_rms(x) * gain + shift).astype(jnp.bfloat16)

    h = modulate(x_ref[...])
    h_halo = modulate(jnp.concatenate([xp_ref[...], xn_ref[...]], axis=0))

    def proj(lhs, lo, hi):
        return jnp.dot(lhs, w_ref[:, lo:hi], preferred_element_type=jnp.float32)

    def rot(t):
        return (t * cos_ref[...] + pltpu.roll(t, LANES - 16, axis=1) * sina_ref[...]
                + pltpu.roll(t, 16, axis=1) * sinb_ref[...])

    c0 = 3 * ATTN_W
    h_ext = jnp.concatenate([h, h_halo], axis=0)
    cu_all = proj(h_ext, c0 + CONV_W, c0 + 2 * CONV_W) * proj(h_ext, c0 + 2 * CONV_W, c0 + 3 * CONV_W)
    gb = proj(h, c0, c0 + CONV_W)
    cu = cu_all[0:tm]
    prev_row = jnp.where(i > 0, cu_all[tm + 7:tm + 8], 0.0)
    next_row = jnp.where(i < n_tiles - 1, cu_all[tm + 8:tm + 9], 0.0)
    row = lax.broadcasted_iota(jnp.int32, cu.shape, 0)
    pos = (i * tm + row) & (seq_len - 1)
    prev = jnp.where(row == 0, prev_row, pltpu.roll(cu, 1, axis=0))
    nxt = jnp.where(row == tm - 1, next_row, pltpu.roll(cu, tm - 1, axis=0))
    prev = jnp.where(pos == 0, 0.0, prev)
    nxt = jnp.where(pos == seq_len - 1, 0.0, nxt)
    conv = cw_ref[0:1, :] * prev + cw_ref[1:2, :] * cu + cw_ref[2:3, :] * nxt
    yc_ref[...] = (gb * conv).astype(yc_ref.dtype)

    def per_head(z, ref, finish):
        for hd in range(N_HEADS):
            t = finish(z[:, hd * D_V:(hd + 1) * D_V]).astype(ref.dtype)
            for sq in range(spt):
                ref[sq, hd] = t[sq * per_seq:(sq + 1) * per_seq]

    per_head(proj(h, 0, ATTN_W), q_ref, lambda t: (rot(t) if rope else t) * QK_SCALE)
    per_head(proj(h, ATTN_W, 2 * ATTN_W), k_ref, lambda t: rot(t) if rope else t)
    per_head(proj(h, 2 * ATTN_W, 3 * ATTN_W), v_ref, lambda t: t)


def _rope_tables(seq):
    n_rows = seq // GRID_W
    row = np.repeat(np.arange(n_rows), GRID_W).astype(np.float64)
    col = np.tile(np.arange(GRID_W), n_rows).astype(np.float64)
    nf = D_QK // 4
    inv = ROPE_BASE ** (-np.arange(nf, dtype=np.float64) / nf)
    ar = row[:, None] * inv
    ac = col[:, None] * inv
    ang = np.concatenate([ar, ar, ac, ac], axis=-1)
    ang = np.concatenate([ang, ang], axis=-1)
    first_half = (np.arange(LANES) % 32) < 16
    cos = np.cos(ang)
    sin = np.sin(ang)
    sina = np.where(first_half, -sin, 0.0)
    sinb = np.where(first_half, 0.0, sin)
    return tuple(jnp.asarray(t, dtype=jnp.float32) for t in (cos, sina, sinb))


def _in_proj_call(x, mod, mod_per_batch, g_pre1, w_in, conv_w, rope, kv_dtype):
    n_seq, seq_len, _ = x.shape
    assert seq_len & (seq_len - 1) == 0
    spt = 1 if (mod_per_batch or seq_len >= TOKEN_TILE) else TOKEN_TILE // seq_len
    x = x.reshape(n_seq // spt, spt * seq_len, D_MODEL)
    bsz, seq, _ = x.shape
    tm = min(TOKEN_TILE, seq)
    n_tiles = seq // tm
    halo = tm // F32_SUBLANES
    n_halo = seq // F32_SUBLANES

    def mod_spec(chunk):
        if mod_per_batch:
            return pl.BlockSpec((None, 1, D_MODEL), lambda b, i: (b, 0, chunk))
        return pl.BlockSpec((None, 1, D_MODEL), lambda b, i: (0, 0, chunk))

    in_specs = [
        pl.BlockSpec((None, tm, D_MODEL), lambda b, i: (b, i, 0)),
        pl.BlockSpec((None, F32_SUBLANES, D_MODEL), lambda b, i: (b, jnp.maximum(i * halo - 1, 0), 0)),
        pl.BlockSpec((None, F32_SUBLANES, D_MODEL), lambda b, i: (b, jnp.minimum((i + 1) * halo, n_halo - 1), 0)),
        mod_spec(0), mod_spec(1),
        pl.BlockSpec((1, D_MODEL), lambda b, i: (0, 0)),
        pl.BlockSpec((D_MODEL, IN_W), lambda b, i: (0, 0)),
        pl.BlockSpec((3, CONV_W), lambda b, i: (0, 0)),
    ]
    args = [x, x, x, mod, mod, g_pre1, w_in, conv_w]
    if rope:
        in_specs += [pl.BlockSpec((tm, LANES), lambda b, i: (i, 0))] * 3
        args += list(_rope_tables(seq))
    head_spec = pl.BlockSpec((spt, N_HEADS, tm // spt, D_V), lambda b, i: (b, 0, i, 0))
    head_shape = lambda dt: jax.ShapeDtypeStruct((n_seq, N_HEADS, seq_len, D_V), dt)
    q, k, v, yc = pl.pallas_call(
        functools.partial(_in_proj_kernel, rope, n_tiles, seq_len),
        grid=(bsz, n_tiles),
        in_specs=in_specs,
        out_specs=[head_spec, head_spec, head_spec, pl.BlockSpec((None, tm, CONV_W), lambda b, i: (b, i, 0))],
        out_shape=[head_shape(jnp.bfloat16), head_shape(kv_dtype), head_shape(kv_dtype),
                   jax.ShapeDtypeStruct((bsz, seq, CONV_W), jnp.bfloat16)],
        name="in_proj_rope" if rope else "in_proj",
    )(*args)
    return q, k, v, yc.reshape(n_seq, seq_len, CONV_W)


def _attn_kernel(n_kv, lam_ref, g_ref, q_ref, *rest):
    kv_refs = rest[:2 * n_kv]
    o_ref = rest[2 * n_kv]
    seqs, heads, tq, _ = q_ref.shape
    for sq in range(seqs):
        _attend(sq, heads, tq, lam_ref, g_ref, q_ref, kv_refs, o_ref)


def _attend(sq, heads, tq, lam_ref, g_ref, q_ref, kv_refs, o_ref):
    n_kv = len(kv_refs) // 2
    lane = lax.broadcasted_iota(jnp.int32, (tq, 2 * D_QK), 1)
    for hh in range(heads):
        q = q_ref[sq, hh]
        zero = jnp.zeros_like(q)
        halves = (jnp.where(lane < D_QK, q, zero), jnp.where(lane >= D_QK, q, zero))
        acc = [jnp.zeros((tq, 2 * D_V), jnp.float32) for _ in range(2)]
        m = [jnp.full((tq, 1), -1e30, jnp.float32) for _ in range(2)]
        for j in range(n_kv):
            k_ref, v_ref = kv_refs[2 * j], kv_refs[2 * j + 1]
            n_keys = k_ref.shape[2]
            ck = min(KEY_CHUNK, n_keys)
            ones = jnp.ones((ck, D_V), jnp.bfloat16)
            for c in range(n_keys // ck):
                k = k_ref[sq, hh, c * ck:(c + 1) * ck, :].astype(jnp.bfloat16)
                v1 = jnp.concatenate([v_ref[sq, hh, c * ck:(c + 1) * ck, :].astype(jnp.bfloat16), ones], axis=1)
                for x, qh in enumerate(halves):
                    s = lax.dot_general(qh, k, NT_DIMS, preferred_element_type=jnp.float32)
                    m_new = jnp.maximum(m[x], jnp.max(s, axis=-1, keepdims=True))
                    p = jnp.exp2(s - m_new).astype(jnp.bfloat16)
                    acc[x] = (jnp.exp2(m[x] - m_new) * acc[x]
                              + jnp.dot(p, v1, preferred_element_type=jnp.float32))
                    m[x] = m_new
        o = (acc[0][:, 0:D_V] / acc[0][:, D_V:2 * D_V]
             - lam_ref[0:1, 0:1] * (acc[1][:, 0:D_V] / acc[1][:, D_V:2 * D_V]))
        o_ref[sq, :, hh * D_V:(hh + 1) * D_V] = (_rms(o) * (g_ref[...] * (1.0 - LAM_INIT))).astype(o_ref.dtype)


def _attn_call(lam, subln_g, q, kvs, seqs_per_step, heads_per_step):
    bsz, _, seq, _ = q.shape
    tq = min(Q_TILE, seq)
    sp, hp = seqs_per_step, heads_per_step
    in_specs = [
        pl.BlockSpec((1, LANES), lambda b, h, i: (0, 0)),
        pl.BlockSpec((1, D_V), lambda b, h, i: (0, 0)),
        pl.BlockSpec((sp, hp, tq, D_V), lambda b, h, i: (b, h, i, 0)),
    ]
    args = [lam, subln_g, q]
    for k, v, spec_fn in kvs:
        in_specs += [spec_fn(sp, hp), spec_fn(sp, hp)]
        args += [k, v]
    return pl.pallas_call(
        functools.partial(_attn_kernel, len(kvs)),
        grid=(bsz // sp, N_HEADS // hp, seq // tq),
        in_specs=in_specs,
        out_specs=pl.BlockSpec((sp, tq, hp * D_V), lambda b, h, i: (b, i, h)),
        out_shape=jax.ShapeDtypeStruct((bsz, seq, ATTN_W), jnp.bfloat16),
        name="attn%d" % len(kvs),
    )(*args)


def _route(logits_t):
    lg = [logits_t[g:g + 1, :] for g in range(N_GROUPS)]
    mg = functools.reduce(jnp.maximum, lg)
    p_sel = 1.0 / functools.reduce(jnp.add, [jnp.exp(t - mg) for t in lg])
    g_sel = jnp.full(mg.shape, N_GROUPS - 1, jnp.int32)
    for g in range(N_GROUPS - 2, -1, -1):
        g_sel = jnp.where(lg[g] == mg, g, g_sel)
    le = []
    for j in range(EXP_PER_GROUP):
        t = jnp.zeros_like(mg)
        for g in range(N_GROUPS):
            r = 8 + g * EXP_PER_GROUP + j
            t = jnp.where(g_sel == g, logits_t[r:r + 1, :], t)
        le.append(t)
    m1 = functools.reduce(jnp.maximum, le)
    i1 = jnp.full(mg.shape, EXP_PER_GROUP - 1, jnp.int32)
    for j in range(EXP_PER_GROUP - 2, -1, -1):
        i1 = jnp.where(le[j] == m1, j, i1)
    neg = jnp.float32(-jnp.inf)
    rest = [jnp.where(i1 == j, neg, le[j]) for j in range(EXP_PER_GROUP)]
    m2 = functools.reduce(jnp.maximum, rest)
    i2 = jnp.full(mg.shape, EXP_PER_GROUP - 1, jnp.int32)
    for j in range(EXP_PER_GROUP - 2, -1, -1):
        i2 = jnp.where(rest[j] == m2, j, i2)
    e2 = jnp.exp(m2 - m1)
    w1 = p_sel / (1.0 + e2)
    w2 = p_sel * e2 / (1.0 + e2)
    base = g_sel * EXP_PER_GROUP
    return (base + i1).astype(jnp.float32), (base + i2).astype(jnp.float32), w1, w2


def _out_proj_kernel(o_ref, yc_ref, wo_ref, x_ref, gate1_ref, shift2_ref, scale2_ref, gpost_ref, gpre_ref,
                     wr_ref, *rest):
    x1_ref, h2_ref, slots_ref, slots_t_ref, counts_ref = rest[-5:]
    tm = o_ref.shape[0]
    sub = min(OUT_SUB_ROWS, tm)
    gain1 = gate1_ref[...] * gpost_ref[...]
    gain2 = gpre_ref[...] * (1.0 + scale2_ref[...])
    routes = []
    for r in range(tm // sub):
        rows = slice(r * sub, (r + 1) * sub)
        out = (jnp.dot(o_ref[rows, :], wo_ref[0:ATTN_W, :], preferred_element_type=jnp.float32)
               + jnp.dot(yc_ref[rows, :], wo_ref[ATTN_W:D_MODEL, :], preferred_element_type=jnp.float32))
        x1 = x_ref[rows, :] + _rms(out) * gain1
        x1_ref[rows, :] = x1
        h2 = (_rms(x1) * gain2 + shift2_ref[...]).astype(jnp.bfloat16)
        h2_ref[rows, :] = h2
        logits = jnp.dot(h2, wr_ref[...], preferred_element_type=jnp.float32)
        routes.append(_route(logits.T))
    per_tile = DISPATCH_TILE // sub
    for d in range(tm // DISPATCH_TILE):
        e1, e2, w1, w2 = (jnp.concatenate(parts, axis=1)
                          for parts in zip(*routes[d * per_tile:(d + 1) * per_tile]))
        _plan(e1, e2, w1, w2, slots_ref.at[d], slots_t_ref.at[d], counts_ref.at[d])


def _out_proj_call(o, yc, w_o, x, mod, mod_per_batch, g_post1, g_pre2, w_router,
                   total_tokens, token_offset, carried):
    bsz, seq, _ = x.shape
    tm = OUT_TILES * DISPATCH_TILE
    assert seq % tm == 0 and token_offset % tm == 0 and total_tokens % tm == 0
    n_tiles = seq // tm
    tile0 = token_offset // tm
    n_all = total_tokens // DISPATCH_TILE

    def mod_spec(chunk):
        if mod_per_batch:
            return pl.BlockSpec((None, 1, D_MODEL), lambda b, i: (b, 0, chunk))
        return pl.BlockSpec((None, 1, D_MODEL), lambda b, i: (0, 0, chunk))

    def vec_spec():
        return pl.BlockSpec((1, D_MODEL), lambda b, i: (0, 0))

    tok = lambda w: pl.BlockSpec((None, tm, w), lambda b, i: (b, i, 0))
    flat = lambda w: pl.BlockSpec((tm, w), lambda b, i: (tile0 + b * n_tiles + i, 0))
    per_tile = lambda r, c: pl.BlockSpec((OUT_TILES, r, c), lambda b, i: (tile0 + b * n_tiles + i, 0, 0))
    in_specs = [
        tok(ATTN_W), tok(CONV_W),
        pl.BlockSpec((D_MODEL, D_MODEL), lambda b, i: (0, 0)),
        tok(D_MODEL),
        mod_spec(2), mod_spec(3), mod_spec(4),
        vec_spec(), vec_spec(),
        pl.BlockSpec((D_MODEL, LANES), lambda b, i: (0, 0)),
    ]
    args = [o, yc, w_o, x, mod, mod, mod, g_post1, g_pre2, w_router]
    aliases = {}
    if carried is not None:
        for j, arr in enumerate(carried):
            aliases[len(args)] = j
            in_specs.append(pl.BlockSpec(memory_space=pl.ANY))
            args.append(arr)
    return pl.pallas_call(
        _out_proj_kernel,
        grid=(bsz, n_tiles),
        in_specs=in_specs,
        out_specs=[flat(D_MODEL), flat(D_MODEL), per_tile(8, DISPATCH_TILE), per_tile(DISPATCH_TILE, LANES),
                   per_tile(N_EXPERTS, LANES)],
        out_shape=[
            jax.ShapeDtypeStruct((total_tokens, D_MODEL), jnp.float32),
            jax.ShapeDtypeStruct((total_tokens, D_MODEL), jnp.bfloat16),
            jax.ShapeDtypeStruct((n_all, 8, DISPATCH_TILE), jnp.float32),
            jax.ShapeDtypeStruct((n_all, DISPATCH_TILE, LANES), jnp.float32),
            jax.ShapeDtypeStruct((n_all, N_EXPERTS, LANES), jnp.float32),
        ],
        input_output_aliases=aliases,
        name="out_proj",
    )(*args)


def _plan(e1, e2, w1, w2, slots_ref, slots_t_ref, counts_ref):
    t = e1.shape[1]
    e1 = e1.astype(jnp.int32)
    e2 = e2.astype(jnp.int32)
    eid = lax.broadcasted_iota(jnp.int32, (N_EXPERTS, t), 0)
    hot1 = jnp.where(eid == e1, 1.0, 0.0)
    hot2 = jnp.where(eid == e2, 1.0, 0.0)
    hot = jnp.concatenate([hot1, hot2], axis=0).astype(jnp.bfloat16)
    before = (lax.broadcasted_iota(jnp.int32, (t, t), 0) < lax.broadcasted_iota(jnp.int32, (t, t), 1))
    before = jnp.where(before, 1.0, 0.0).astype(jnp.bfloat16)
    rank = jnp.dot(hot, before, preferred_element_type=jnp.float32)
    n1 = jnp.sum(hot1, axis=1, keepdims=True)
    n2 = jnp.sum(hot2, axis=1, keepdims=True)
    ones = jnp.ones((8, t), jnp.bfloat16)
    cnt_row = lax.dot_general(ones, (hot1 + hot2).astype(jnp.bfloat16), NT_DIMS,
                              preferred_element_type=jnp.float32)[0:1, :]
    pad_row = jnp.floor((cnt_row + (CHUNK - 1)) * (1.0 / CHUNK)) * CHUNK
    lower = (lax.broadcasted_iota(jnp.int32, (N_EXPERTS, N_EXPERTS), 1)
             < lax.broadcasted_iota(jnp.int32, (N_EXPERTS, N_EXPERTS), 0))
    base = jnp.sum(jnp.where(lower, pad_row, 0.0), axis=1, keepdims=True)
    slot1 = jnp.sum(hot1 * (base + rank[:N_EXPERTS]), axis=0, keepdims=True)
    slot2 = jnp.sum(hot2 * (base + n1 + rank[N_EXPERTS:]), axis=0, keepdims=True)
    slots_ref[...] = jnp.concatenate([slot1, slot2, w1, w2, jnp.zeros((4, t), jnp.float32)], axis=0)
    wide = jnp.concatenate([slot1, slot2, jnp.zeros((LANES - 2, t), jnp.float32)], axis=0)
    slots_t_ref[...] = wide.T
    counts_ref[...] = jnp.broadcast_to(n1 + n2, counts_ref.shape)


def _layout_tables(counts, n_blocks):
    cnt = counts[:, :, 0].astype(jnp.int32)
    n_tiles = cnt.shape[0]
    nch = (cnt + (CHUNK - 1)) // CHUNK
    csum = jnp.cumsum(nch, axis=0)
    rows = CHUNK * csum[-1]
    blocks = (rows + (EXPERT_BLOCK - 1)) // EXPERT_BLOCK
    blk_end = jnp.cumsum(blocks)
    blk_start = blk_end - blocks
    off = blk_start[None, :] * EXPERT_BLOCK + CHUNK * (csum - nch)
    first = jnp.cumsum(nch, axis=1) - nch
    used = jnp.sum(nch, axis=1)
    c = jnp.arange(N_CHUNKS, dtype=jnp.int32)
    e_idx = jnp.sum((c[None, :, None] >= first[:, None, :]).astype(jnp.int32), axis=2) - 1
    hot = e_idx[:, :, None] == jnp.arange(N_EXPERTS, dtype=jnp.int32)[None, None, :]
    pick = lambda a: jnp.sum(jnp.where(hot, a[:, None, :], 0), axis=2)
    row = pick(off) + CHUNK * (c[None, :] - pick(first))
    valid = c[None, :] < used[:, None]
    spare = (n_blocks * EXPERT_BLOCK + (jnp.arange(n_tiles, dtype=jnp.int32) % SPARE_SETS)[:, None] * TILE_SLOTS
             + CHUNK * c[None, :])
    scatter_rows = jnp.where(valid, row, spare)
    gather_rows = jnp.where(valid, row, 0)
    n_used = blk_end[-1]
    bc = jnp.minimum(jnp.arange(n_blocks, dtype=jnp.int32), n_used - 1)
    block_expert = jnp.sum((bc[:, None] >= blk_end[None, :]).astype(jnp.int32), axis=1)
    tail_start = blk_start * EXPERT_BLOCK + rows
    tail_chunks = (blocks * EXPERT_BLOCK - rows) // CHUNK
    flat = lambda a: a.reshape(-1).astype(jnp.int32)
    return (flat(scatter_rows // CHUNK), flat(gather_rows // CHUNK), flat(tail_start // CHUNK), flat(tail_chunks),
            flat(block_expert), flat(n_used))


def _chunk_copies(tile, chunk_ref, make_copy, action):
    for c in range(N_CHUNKS):
        action(make_copy(c, chunk_ref[tile * N_CHUNKS + c]))


def _wait_chunk_copies(make_copy):
    for _ in range(N_CHUNKS):
        make_copy(0, 0).wait()


def _dispatch_kernel(rows_ref, tail_start_ref, tail_chunks_ref, h_ref, slots_ref, xs_ref, buf, zbuf, sem, zsem):
    i = pl.program_id(0)
    n = pl.num_programs(0)
    slot = i % 2
    t = DISPATCH_TILE
    lane = lax.broadcasted_iota(jnp.int32, (SLOT_GROUP, LANES), 1)

    def copies(sl, d):
        def make(src_piece, dst_piece):
            return pltpu.make_async_copy(buf.at[sl, d, src_piece], xs_ref.at[dst_piece], sem.at[sl])
        return make

    for d in range(MOE_TILES):
        s = slots_ref[d]
        s1 = s[0:1, :].astype(jnp.int32)
        s2 = s[1:2, :].astype(jnp.int32)
        h = h_ref[d * t:(d + 1) * t, :]
        for g in range(TILE_SLOTS // SLOT_GROUP):
            rid = lax.broadcasted_iota(jnp.int32, (SLOT_GROUP, t), 0) + g * SLOT_GROUP
            m1 = rid == s1
            m2 = rid == s2
            perm = jnp.where(m1, 1.0, jnp.where(m2, 1.0, 0.0)).astype(jnp.bfloat16)
            rows = jnp.dot(perm, h, preferred_element_type=jnp.float32)
            w = jnp.sum(jnp.where(m1, s[2:3, :], 0.0) + jnp.where(m2, s[3:4, :], 0.0), axis=1, keepdims=True)
            hi = w.astype(jnp.bfloat16).astype(jnp.float32)
            aux = jnp.where(lane == 0, hi, jnp.where(lane == 1, w - hi, 0.0))
            pieces = slice(g * GROUP_CHUNKS, (g + 1) * GROUP_CHUNKS)
            buf[slot, d, pieces, :, 0:D_MODEL] = rows.astype(jnp.bfloat16).reshape(GROUP_CHUNKS, CHUNK, D_MODEL)
            buf[slot, d, pieces, :, D_MODEL:XS_W] = aux.astype(jnp.bfloat16).reshape(GROUP_CHUNKS, CHUNK, LANES)
        _chunk_copies(i * MOE_TILES + d, rows_ref, copies(slot, d), lambda c: c.start())

    @pl.when(i > 0)
    def _():
        for d in range(MOE_TILES):
            _wait_chunk_copies(copies(1 - slot, d))

    @pl.when(i == n - 1)
    def _():
        zbuf[...] = jnp.zeros_like(zbuf)

        def tails(action):
            for e in range(N_EXPERTS):
                start = tail_start_ref[e]

                def body(m, carry, start=start):
                    action(pltpu.make_async_copy(zbuf, xs_ref.at[start + m], zsem))
                    return carry

                lax.fori_loop(0, tail_chunks_ref[e], body, 0)

        tails(lambda c: c.start())
        tails(lambda c: c.wait())
        for d in range(MOE_TILES):
            _wait_chunk_copies(copies(slot, d))


def _dispatch_call(tables, h2, slots, n_blocks):
    scatter_rows, tail_start, tail_chunks = tables
    tokens = h2.shape[0]
    t = MOE_TILES * DISPATCH_TILE
    return pl.pallas_call(
        _dispatch_kernel,
        grid_spec=pltpu.PrefetchScalarGridSpec(
            num_scalar_prefetch=3,
            grid=(tokens // t,),
            in_specs=[
                pl.BlockSpec((t, D_MODEL), lambda i, *_: (i, 0)),
                pl.BlockSpec((MOE_TILES, 8, DISPATCH_TILE), lambda i, *_: (i, 0, 0)),
            ],
            out_specs=pl.BlockSpec(memory_space=pl.ANY),
            scratch_shapes=[
                pltpu.VMEM((2, MOE_TILES, N_CHUNKS, CHUNK, XS_W), jnp.bfloat16),
                pltpu.VMEM((CHUNK, XS_W), jnp.bfloat16),
                pltpu.SemaphoreType.DMA((2,)),
                pltpu.SemaphoreType.DMA(()),
            ],
        ),
        out_shape=jax.ShapeDtypeStruct(((n_blocks * EXPERT_BLOCK + SPARE_SETS * TILE_SLOTS) // CHUNK, CHUNK, XS_W),
                                       jnp.bfloat16),
        name="moe_dispatch",
    )(scatter_rows, tail_start, tail_chunks, h2, slots)


def _expert_kernel(bexp_ref, nused_ref, xs_ref, wg_ref, wu_ref, wd_ref, ys_ref, wgu_b, wd_b):
    b = pl.program_id(0)
    e = bexp_ref[b]
    prev = bexp_ref[jnp.maximum(b - 1, 0)]

    @pl.when((b == 0) | (e != prev))
    def _():
        wgu_b[:, 0:D_EXPERT] = wg_ref[...].astype(jnp.bfloat16)
        wgu_b[:, D_EXPERT:2 * D_EXPERT] = wu_ref[...].astype(jnp.bfloat16)
        wd_b[...] = wd_ref[...].astype(jnp.bfloat16)

    @pl.when(b < nused_ref[0])
    def _():
        x = xs_ref[:, 0:D_MODEL]
        y = None
        half = D_EXPERT // 2
        for c in range(2):
            cols = slice(c * half, (c + 1) * half)
            g = jnp.dot(x, wgu_b[:, c * half:(c + 1) * half], preferred_element_type=jnp.float32)
            u = jnp.dot(x, wgu_b[:, D_EXPERT + c * half:D_EXPERT + (c + 1) * half],
                        preferred_element_type=jnp.float32)
            act = (_silu(g) * u).astype(jnp.bfloat16)
            t = jnp.dot(act, wd_b[cols, :], preferred_element_type=jnp.float32)
            y = t if y is None else y + t
        w = (xs_ref[:, D_MODEL:D_MODEL + 1].astype(jnp.float32)
             + xs_ref[:, D_MODEL + 1:D_MODEL + 2].astype(jnp.float32))
        ys_ref[...] = (w * y).astype(jnp.bfloat16)


def _expert_call(block_expert, n_used, xs, w_eg, w_eu, w_ed, n_blocks):
    row_blk = lambda w: pl.BlockSpec((EXPERT_BLOCK, w), lambda b, be, nu: (jnp.minimum(b, nu[0] - 1), 0))
    wt_blk = lambda r, c: pl.BlockSpec((None, r, c), lambda b, be, nu: (be[b], 0, 0))
    return pl.pallas_call(
        _expert_kernel,
        grid_spec=pltpu.PrefetchScalarGridSpec(
            num_scalar_prefetch=2,
            grid=(n_blocks,),
            in_specs=[
                row_blk(XS_W),
                wt_blk(D_MODEL, D_EXPERT), wt_blk(D_MODEL, D_EXPERT), wt_blk(D_EXPERT, D_MODEL),
            ],
            out_specs=row_blk(D_MODEL),
            scratch_shapes=[
                pltpu.VMEM((D_MODEL, 2 * D_EXPERT), jnp.bfloat16),
                pltpu.VMEM((D_EXPERT, D_MODEL), jnp.bfloat16),
            ],
        ),
        out_shape=jax.ShapeDtypeStruct((n_blocks * EXPERT_BLOCK, D_MODEL), jnp.bfloat16),
        name="moe_experts",
    )(block_expert, n_used, xs, w_eg, w_eu, w_ed)


def _combine_kernel(tile0, rows_ref, ys_ref, slots_t_ref, x1_ref, gate2_ref, gpost_ref, out_ref, buf, sem):
    j = pl.program_id(0)
    n = pl.num_programs(0)
    slot = j % 2
    tile = tile0 + j * MOE_TILES

    def copies(sl, d):
        def make(dst_piece, src_piece):
            return pltpu.make_async_copy(ys_ref.at[src_piece], buf.at[sl, d, dst_piece], sem.at[sl])
        return make

    def fetch(first_tile, sl):
        for d in range(MOE_TILES):
            _chunk_copies(first_tile + d, rows_ref, copies(sl, d), lambda c: c.start())

    @pl.when(j == 0)
    def _():
        fetch(tile, slot)

    @pl.when(j + 1 < n)
    def _():
        fetch(tile + MOE_TILES, 1 - slot)

    for d in range(MOE_TILES):
        _wait_chunk_copies(copies(slot, d))
    gain = gate2_ref[...] * gpost_ref[...]
    lane = lax.broadcasted_iota(jnp.int32, (COMBINE_SUB_ROWS, TILE_SLOTS), 1)
    per_tile = DISPATCH_TILE // COMBINE_SUB_ROWS
    for d in range(MOE_TILES):
        sorted_rows = buf[slot, d].reshape(TILE_SLOTS, D_MODEL)
        for r in range(per_tile):
            rows = slice((d * per_tile + r) * COMBINE_SUB_ROWS, (d * per_tile + r + 1) * COMBINE_SUB_ROWS)
            local = slice(r * COMBINE_SUB_ROWS, (r + 1) * COMBINE_SUB_ROWS)
            s1 = slots_t_ref[d, local, 0:1].astype(jnp.int32)
            s2 = slots_t_ref[d, local, 1:2].astype(jnp.int32)
            unperm = jnp.where(lane == s1, 1.0, jnp.where(lane == s2, 1.0, 0.0)).astype(jnp.bfloat16)
            y = jnp.dot(unperm, sorted_rows, preferred_element_type=jnp.float32)
            out_ref[rows, :] = x1_ref[rows, :] + _rms(y) * gain


def _combine_call(gather_rows, ys, slots_t, x1, mod, mod_per_batch, g_post2, token_offset, bsz, seq):
    t = MOE_TILES * DISPATCH_TILE
    tile0 = token_offset // DISPATCH_TILE
    step0 = token_offset // t
    tokens = bsz * seq
    assert tokens % t == 0 and token_offset % t == 0 and (seq % t == 0 or not mod_per_batch)
    if mod_per_batch:
        per = seq // t
        gate2_spec = pl.BlockSpec((None, 1, D_MODEL), lambda j, *_: (j // per, 0, 5))
    else:
        gate2_spec = pl.BlockSpec((None, 1, D_MODEL), lambda j, *_: (0, 0, 5))
    y = pl.pallas_call(
        functools.partial(_combine_kernel, tile0),
        grid_spec=pltpu.PrefetchScalarGridSpec(
            num_scalar_prefetch=1,
            grid=(tokens // t,),
            in_specs=[
                pl.BlockSpec(memory_space=pl.ANY),
                pl.BlockSpec((MOE_TILES, DISPATCH_TILE, LANES), lambda j, *_: (step0 + j, 0, 0)),
                pl.BlockSpec((t, D_MODEL), lambda j, *_: (step0 + j, 0)),
                gate2_spec,
                pl.BlockSpec((1, D_MODEL), lambda j, *_: (0, 0)),
            ],
            out_specs=pl.BlockSpec((t, D_MODEL), lambda j, *_: (j, 0)),
            scratch_shapes=[
                pltpu.VMEM((2, MOE_TILES, N_CHUNKS, CHUNK, D_MODEL), jnp.bfloat16),
                pltpu.SemaphoreType.DMA((2,)),
            ],
        ),
        out_shape=jax.ShapeDtypeStruct((tokens, D_MODEL), jnp.float32),
        name="moe_combine",
    )(gather_rows, ys.reshape(-1, CHUNK, D_MODEL), slots_t, x1, mod, g_post2)
    return y.reshape(bsz, seq, D_MODEL)


def kernel(x_prompt, x_sample, cache_k, cache_v, c, c_ctx, w_mod, b_mod, g_pre1, g_post1, g_pre2, g_post2,
           w_in, conv_w, lambda_q1, lambda_k1, lambda_q2, lambda_k2, subln_g, w_o, w_router_group,
           w_router_expert, w_exp_gate, w_exp_up, w_exp_down):
    n_lat = c.shape[0]
    cond = jnp.concatenate(
        [c, c_ctx[None, :], jnp.zeros((MOD_ROWS - n_lat - 1, D_MODEL), jnp.float32)], axis=0)
    mod, lam = _mod_call(cond, w_mod[0], b_mod, lambda_q1, lambda_k1, lambda_q2, lambda_k2)
    mod = mod.reshape(MOD_ROWS, 1, 6 * D_MODEL)
    mod_lat, mod_ctx = mod[:n_lat], mod[n_lat:n_lat + 1]

    w_in_b = w_in[0].astype(jnp.bfloat16)
    w_o_b = w_o[0].astype(jnp.bfloat16)
    w_router = jnp.concatenate(
        [w_router_group[0], jnp.zeros((D_MODEL, 8 - N_GROUPS), jnp.float32), w_router_expert[0],
         jnp.zeros((D_MODEL, LANES - 8 - N_EXPERTS), jnp.float32)], axis=1).astype(jnp.bfloat16)

    n_prompt = x_prompt.shape[0] * x_prompt.shape[1]
    n_sample = x_sample.shape[0] * x_sample.shape[1]
    total = n_prompt + n_sample

    def mixer(x, mod_x, per_batch, rope, ctx_kv, token_offset, carried):
        kv_dtype = jnp.bfloat16 if rope else jnp.float32
        q, k, v, yc = _in_proj_call(x, mod_x, per_batch, g_pre1, w_in_b, conv_w[0], rope, kv_dtype)
        kvs = []
        if ctx_kv is not None:
            ck, cv = ctx_kv
            n_ctx = ck.shape[3]
            kvs.append((ck, cv, lambda sp, hp: pl.BlockSpec((sp, None, hp, n_ctx, D_V),
                                                            lambda b, h, i: (b, 0, h, 0, 0))))
        seq = x.shape[1]
        kvs.append((k, v, lambda sp, hp: pl.BlockSpec((sp, hp, seq, D_V), lambda b, h, i: (b, h, 0, 0))))
        short = seq <= MERGE_HEADS_MAX_SEQ
        o = _attn_call(lam, subln_g, q, kvs, SHORT_SEQS_PER_STEP if short else 1, N_HEADS if short else 1)
        if not per_batch:
            o, yc, x = (a.reshape(1, -1, a.shape[-1]) for a in (o, yc, x))
        shared = _out_proj_call(o, yc, w_o_b, x, mod_x, per_batch, g_post1, g_pre2, w_router,
                                total, token_offset, carried)
        return shared, k, v

    shared, kp, vp = mixer(x_prompt, mod_ctx, False, False, None, 0, None)
    (x1, h2, slots, slots_t, counts), _, _ = mixer(x_sample, mod_lat, True, True, (cache_k, cache_v),
                                                   n_prompt, shared)

    n_tiles = total // DISPATCH_TILE
    max_rows = 2 * total + n_tiles * N_EXPERTS * (CHUNK - 1) + N_EXPERTS * (EXPERT_BLOCK - CHUNK)
    n_blocks = -(-max_rows // EXPERT_BLOCK)
    scatter_rows, gather_rows, tail_start, tail_chunks, block_expert, n_used = _layout_tables(counts, n_blocks)
    xs = _dispatch_call((scatter_rows, tail_start, tail_chunks), h2, slots, n_blocks)
    ys = _expert_call(block_expert, n_used, xs.reshape(-1, XS_W), w_exp_gate[0], w_exp_up[0], w_exp_down[0],
                      n_blocks)
    yp = _combine_call(gather_rows, ys, slots_t, x1, mod_ctx, False, g_post2, 0,
                       x_prompt.shape[0], x_prompt.shape[1])
    ysamp = _combine_call(gather_rows, ys, slots_t, x1, mod_lat, True, g_post2, n_prompt,
                          x_sample.shape[0], x_sample.shape[1])
    return yp, ysamp, kp[:, None], vp[:, None]
```

```python
import functools
import math

import numpy as np
import jax
import jax.numpy as jnp
from jax import lax
from jax.experimental import pallas as pl
from jax.experimental.pallas import tpu as pltpu

D_MODEL = 1024
GRID_W = 64
N_HEADS = 4
D_QK = 64
D_V = 128
ATTN_W = N_HEADS * D_V
CONV_W = D_MODEL - ATTN_W
IN_W = 3 * ATTN_W + 3 * CONV_W
N_GROUPS = 4
EXP_PER_GROUP = 4
N_EXPERTS = N_GROUPS * EXP_PER_GROUP
D_EXPERT = 512
ROPE_BASE = 10000.0
EPS = 1e-6
LAM_INIT = 0.8 - 0.6 * math.exp(-0.3 * 0)

LANES = 128
F32_SUBLANES = 8
BF16_SUBLANES = 16
MOD_ROWS = 16
TOKEN_TILE = 1024
OUT_SUB_ROWS = 256
OUT_TILES = 2
Q_TILE = 2048
KEY_CHUNK = 256
LONG_SEQ_HEADS_PER_STEP = 2
MERGE_HEADS_MAX_SEQ = 512
SHORT_SEQS_PER_STEP = 4
DISPATCH_TILE = 512
CHUNK = BF16_SUBLANES
TILE_SLOTS = 1280
N_CHUNKS = TILE_SLOTS // CHUNK
SLOT_GROUP = 256
GROUP_CHUNKS = SLOT_GROUP // CHUNK
MOE_TILES = 2
SPARE_SETS = 2 * MOE_TILES
COMBINE_SUB_ROWS = 256
EXPERT_BLOCK = 1024
XS_W = D_MODEL + LANES
QK_SCALE = (1.0 / math.sqrt(D_QK)) * math.log2(math.e)

NT_DIMS = (((1,), (1,)), ((), ()))


def _rms(x):
    return x * lax.rsqrt(jnp.mean(x * x, axis=-1, keepdims=True) + EPS)


def _silu(x):
    return x * (1.0 / (1.0 + jnp.exp(-x)))


def _mod_kernel(cond_ref, w_ref, b_ref, lq1_ref, lk1_ref, lq2_ref, lk2_ref, mod_ref, lam_ref):
    s = _silu(cond_ref[...])
    s_hi = s.astype(jnp.bfloat16)
    s_lo = (s - s_hi.astype(jnp.float32)).astype(jnp.bfloat16)
    w = w_ref[...]
    w_hi = w.astype(jnp.bfloat16)
    w_lo = (w - w_hi.astype(jnp.float32)).astype(jnp.bfloat16)
    both = jnp.dot(jnp.concatenate([s_hi, s_lo], axis=0), w_hi, preferred_element_type=jnp.float32)
    m = both[0:MOD_ROWS] + both[MOD_ROWS:2 * MOD_ROWS] + jnp.dot(s_hi, w_lo, preferred_element_type=jnp.float32)
    mod_ref[...] = m + b_ref[...]
    a = jnp.sum(lq1_ref[...] * lk1_ref[...], axis=-1, keepdims=True)
    b = jnp.sum(lq2_ref[...] * lk2_ref[...], axis=-1, keepdims=True)
    lam_ref[...] = jnp.broadcast_to(jnp.exp(a) - jnp.exp(b) + LAM_INIT, lam_ref.shape)


def _mod_call(cond, w_mod, b_mod, lq1, lk1, lq2, lk2):
    n_col = 6 * D_MODEL
    col_tile = 1536
    small = pl.BlockSpec((1, D_QK), lambda j: (0, 0))
    return pl.pallas_call(
        _mod_kernel,
        grid=(n_col // col_tile,),
        in_specs=[
            pl.BlockSpec((MOD_ROWS, D_MODEL), lambda j: (0, 0)),
            pl.BlockSpec((D_MODEL, col_tile), lambda j: (0, j)),
            pl.BlockSpec((1, col_tile), lambda j: (0, j)),
            small, small, small, small,
        ],
        out_specs=[
            pl.BlockSpec((MOD_ROWS, col_tile), lambda j: (0, j)),
            pl.BlockSpec((1, LANES), lambda j: (0, 0)),
        ],
        out_shape=[
            jax.ShapeDtypeStruct((MOD_ROWS, n_col), jnp.float32),
            jax.ShapeDtypeStruct((1, LANES), jnp.float32),
        ],
        name="mod",
    )(cond, w_mod, b_mod, lq1, lk1, lq2, lk2)


def _in_proj_kernel(rope, n_tiles, seq_len, x_ref, xp_ref, xn_ref, shift_ref, scale_ref, g_ref, w_ref, cw_ref,
                    *rest):
    if rope:
        cos_ref, sina_ref, sinb_ref, q_ref, k_ref, v_ref, yc_ref = rest
    else:
        q_ref, k_ref, v_ref, yc_ref = rest
    i = pl.program_id(1)
    tm = x_ref.shape[0]
    spt = q_ref.shape[0]
    per_seq = tm // spt
    gain = g_ref[...] * (1.0 + scale_ref[...])
    shift = shift_ref[...]

    def modulate(x):
        return (_rms(x) * gain + shift).astype(jnp.bfloat16)

    h = modulate(x_ref[...])
    h_halo = modulate(jnp.concatenate([xp_ref[...], xn_ref[...]], axis=0))

    def proj(lhs, lo, hi):
        return jnp.dot(lhs, w_ref[:, lo:hi], preferred_element_type=jnp.float32)

    def rot(t):
        return (t * cos_ref[...] + pltpu.roll(t, LANES - 16, axis=1) * sina_ref[...]
                + pltpu.roll(t, 16, axis=1) * sinb_ref[...])

    c0 = 3 * ATTN_W
    h_ext = jnp.concatenate([h, h_halo], axis=0)
    cu_all = proj(h_ext, c0 + CONV_W, c0 + 2 * CONV_W) * proj(h_ext, c0 + 2 * CONV_W, c0 + 3 * CONV_W)
    gb = proj(h, c0, c0 + CONV_W)
    cu = cu_all[0:tm]
    prev_row = jnp.where(i > 0, cu_all[tm + 7:tm + 8], 0.0)
    next_row = jnp.where(i < n_tiles - 1, cu_all[tm + 8:tm + 9], 0.0)
    row = lax.broadcasted_iota(jnp.int32, cu.shape, 0)
    pos = (i * tm + row) & (seq_len - 1)
    prev = jnp.where(row == 0, prev_row, pltpu.roll(cu, 1, axis=0))
    nxt = jnp.where(row == tm - 1, next_row, pltpu.roll(cu, tm - 1, axis=0))
    prev = jnp.where(pos == 0, 0.0, prev)
    nxt = jnp.where(pos == seq_len - 1, 0.0, nxt)
    conv = cw_ref[0:1, :] * prev + cw_ref[1:2, :] * cu + cw_ref[2:3, :] * nxt
    yc_ref[...] = (gb * conv).astype(yc_ref.dtype)

    def per_head(z, ref, finish):
        for hd in range(N_HEADS):
            t = finish(z[:, hd * D_V:(hd + 1) * D_V]).astype(ref.dtype)
            for sq in range(spt):
                ref[sq, hd] = t[sq * per_seq:(sq + 1) * per_seq]

    per_head(proj(h, 0, ATTN_W), q_ref, lambda t: (rot(t) if rope else t) * QK_SCALE)
    per_head(proj(h, ATTN_W, 2 * ATTN_W), k_ref, lambda t: rot(t) if rope else t)
    per_head(proj(h, 2 * ATTN_W, 3 * ATTN_W), v_ref, lambda t: t)


def _rope_tables(seq):
    n_rows = seq // GRID_W
    row = np.repeat(np.arange(n_rows), GRID_W).astype(np.float64)
    col = np.tile(np.arange(GRID_W), n_rows).astype(np.float64)
    nf = D_QK // 4
    inv = ROPE_BASE ** (-np.arange(nf, dtype=np.float64) / nf)
    ar = row[:, None] * inv
    ac = col[:, None] * inv
    ang = np.concatenate([ar, ar, ac, ac], axis=-1)
    ang = np.concatenate([ang, ang], axis=-1)
    first_half = (np.arange(LANES) % 32) < 16
    cos = np.cos(ang)
    sin = np.sin(ang)
    sina = np.where(first_half, -sin, 0.0)
    sinb = np.where(first_half, 0.0, sin)
    return tuple(jnp.asarray(t, dtype=jnp.float32) for t in (cos, sina, sinb))


def _in_proj_call(x, mod, mod_per_batch, g_pre1, w_in, conv_w, rope, kv_dtype):
    n_seq, seq_len, _ = x.shape
    assert seq_len & (seq_len - 1) == 0
    spt = 1 if (mod_per_batch or seq_len >= TOKEN_TILE) else TOKEN_TILE // seq_len
    x = x.reshape(n_seq // spt, spt * seq_len, D_MODEL)
    bsz, seq, _ = x.shape
    tm = min(TOKEN_TILE, seq)
    n_tiles = seq // tm
    halo = tm // F32_SUBLANES
    n_halo = seq // F32_SUBLANES

    def mod_spec(chunk):
        if mod_per_batch:
            return pl.BlockSpec((None, 1, D_MODEL), lambda b, i: (b, 0, chunk))
        return pl.BlockSpec((None, 1, D_MODEL), lambda b, i: (0, 0, chunk))

    in_specs = [
        pl.BlockSpec((None, tm, D_MODEL), lambda b, i: (b, i, 0)),
        pl.BlockSpec((None, F32_SUBLANES, D_MODEL), lambda b, i: (b, jnp.maximum(i * halo - 1, 0), 0)),
        pl.BlockSpec((None, F32_SUBLANES, D_MODEL), lambda b, i: (b, jnp.minimum((i + 1) * halo, n_halo - 1), 0)),
        mod_spec(0), mod_spec(1),
        pl.BlockSpec((1, D_MODEL), lambda b, i: (0, 0)),
        pl.BlockSpec((D_MODEL, IN_W), lambda b, i: (0, 0)),
        pl.BlockSpec((3, CONV_W), lambda b, i: (0, 0)),
    ]
    args = [x, x, x, mod, mod, g_pre1, w_in, conv_w]
    if rope:
        in_specs += [pl.BlockSpec((tm, LANES), lambda b, i: (i, 0))] * 3
        args += list(_rope_tables(seq))
    head_spec = pl.BlockSpec((spt, N_HEADS, tm // spt, D_V), lambda b, i: (b, 0, i, 0))
    head_shape = lambda dt: jax.ShapeDtypeStruct((n_seq, N_HEADS, seq_len, D_V), dt)
    q, k, v, yc = pl.pallas_call(
        functools.partial(_in_proj_kernel, rope, n_tiles, seq_len),
        grid=(bsz, n_tiles),
        in_specs=in_specs,
        out_specs=[head_spec, head_spec, head_spec, pl.BlockSpec((None, tm, CONV_W), lambda b, i: (b, i, 0))],
        out_shape=[head_shape(jnp.bfloat16), head_shape(kv_dtype), head_shape(kv_dtype),
                   jax.ShapeDtypeStruct((bsz, seq, CONV_W), jnp.bfloat16)],
        name="in_proj_rope" if rope else "in_proj",
    )(*args)
    return q, k, v, yc.reshape(n_seq, seq_len, CONV_W)


def _attn_kernel(n_kv, lam_ref, g_ref, q_ref, *rest):
    kv_refs = rest[:2 * n_kv]
    o_ref = rest[2 * n_kv]
    seqs, heads, tq, _ = q_ref.shape
    for sq in range(seqs):
        _attend(sq, heads, tq, lam_ref, g_ref, q_ref, kv_refs, o_ref)


def _attend(sq, heads, tq, lam_ref, g_ref, q_ref, kv_refs, o_ref):
    n_kv = len(kv_refs) // 2
    lane = lax.broadcasted_iota(jnp.int32, (tq, 2 * D_QK), 1)
    for hh in range(heads):
        q = q_ref[sq, hh]
        zero = jnp.zeros_like(q)
        halves = (jnp.where(lane < D_QK, q, zero), jnp.where(lane >= D_QK, q, zero))
        acc = [jnp.zeros((tq, 2 * D_V), jnp.float32) for _ in range(2)]
        m = [jnp.full((tq, 1), -1e30, jnp.float32) for _ in range(2)]
        for j in range(n_kv):
            k_ref, v_ref = kv_refs[2 * j], kv_refs[2 * j + 1]
            n_keys = k_ref.shape[2]
            ck = min(KEY_CHUNK, n_keys)
            ones = jnp.ones((ck, D_V), jnp.bfloat16)
            for c in range(n_keys // ck):
                k = k_ref[sq, hh, c * ck:(c + 1) * ck, :].astype(jnp.bfloat16)
                v1 = jnp.concatenate([v_ref[sq, hh, c * ck:(c + 1) * ck, :].astype(jnp.bfloat16), ones], axis=1)
                for x, qh in enumerate(halves):
                    s = lax.dot_general(qh, k, NT_DIMS, preferred_element_type=jnp.float32)
                    m_new = jnp.maximum(m[x], jnp.max(s, axis=-1, keepdims=True))
                    p = jnp.exp2(s - m_new).astype(jnp.bfloat16)
                    acc[x] = (jnp.exp2(m[x] - m_new) * acc[x]
                              + jnp.dot(p, v1, preferred_element_type=jnp.float32))
                    m[x] = m_new
        o = (acc[0][:, 0:D_V] / acc[0][:, D_V:2 * D_V]
             - lam_ref[0:1, 0:1] * (acc[1][:, 0:D_V] / acc[1][:, D_V:2 * D_V]))
        o_ref[sq, :, hh * D_V:(hh + 1) * D_V] = (_rms(o) * (g_ref[...] * (1.0 - LAM_INIT))).astype(o_ref.dtype)


def _attn_call(lam, subln_g, q, kvs, seqs_per_step, heads_per_step):
    bsz, _, seq, _ = q.shape
    tq = min(Q_TILE, seq)
    sp, hp = seqs_per_step, heads_per_step
    in_specs = [
        pl.BlockSpec((1, LANES), lambda b, h, i: (0, 0)),
        pl.BlockSpec((1, D_V), lambda b, h, i: (0, 0)),
        pl.BlockSpec((sp, hp, tq, D_V), lambda b, h, i: (b, h, i, 0)),
    ]
    args = [lam, subln_g, q]
    for k, v, spec_fn in kvs:
        in_specs += [spec_fn(sp, hp), spec_fn(sp, hp)]
        args += [k, v]
    return pl.pallas_call(
        functools.partial(_attn_kernel, len(kvs)),
        grid=(bsz // sp, N_HEADS // hp, seq // tq),
        in_specs=in_specs,
        out_specs=pl.BlockSpec((sp, tq, hp * D_V), lambda b, h, i: (b, i, h)),
        out_shape=jax.ShapeDtypeStruct((bsz, seq, ATTN_W), jnp.bfloat16),
        name="attn%d" % len(kvs),
    )(*args)


def _route(logits_t):
    lg = [logits_t[g:g + 1, :] for g in range(N_GROUPS)]
    mg = functools.reduce(jnp.maximum, lg)
    p_sel = 1.0 / functools.reduce(jnp.add, [jnp.exp(t - mg) for t in lg])
    g_sel = jnp.full(mg.shape, N_GROUPS - 1, jnp.int32)
    for g in range(N_GROUPS - 2, -1, -1):
        g_sel = jnp.where(lg[g] == mg, g, g_sel)
    le = []
    for j in range(EXP_PER_GROUP):
        t = jnp.zeros_like(mg)
        for g in range(N_GROUPS):
            r = 8 + g * EXP_PER_GROUP + j
            t = jnp.where(g_sel == g, logits_t[r:r + 1, :], t)
        le.append(t)
    m1 = functools.reduce(jnp.maximum, le)
    i1 = jnp.full(mg.shape, EXP_PER_GROUP - 1, jnp.int32)
    for j in range(EXP_PER_GROUP - 2, -1, -1):
        i1 = jnp.where(le[j] == m1, j, i1)
    neg = jnp.float32(-jnp.inf)
    rest = [jnp.where(i1 == j, neg, le[j]) for j in range(EXP_PER_GROUP)]
    m2 = functools.reduce(jnp.maximum, rest)
    i2 = jnp.full(mg.shape, EXP_PER_GROUP - 1, jnp.int32)
    for j in range(EXP_PER_GROUP - 2, -1, -1):
        i2 = jnp.where(rest[j] == m2, j, i2)
    e2 = jnp.exp(m2 - m1)
    w1 = p_sel / (1.0 + e2)
    w2 = p_sel * e2 / (1.0 + e2)
    base = g_sel * EXP_PER_GROUP
    return (base + i1).astype(jnp.float32), (base + i2).astype(jnp.float32), w1, w2


def _out_proj_kernel(o_ref, yc_ref, wo_ref, x_ref, gate1_ref, shift2_ref, scale2_ref, gpost_ref, gpre_ref,
                     wr_ref, *rest):
    x1_ref, h2_ref, slots_ref, slots_t_ref, counts_ref = rest[-5:]
    tm = o_ref.shape[0]
    sub = min(OUT_SUB_ROWS, tm)
    gain1 = gate1_ref[...] * gpost_ref[...]
    gain2 = gpre_ref[...] * (1.0 + scale2_ref[...])
    routes = []
    for r in range(tm // sub):
        rows = slice(r * sub, (r + 1) * sub)
        out = (jnp.dot(o_ref[rows, :], wo_ref[0:ATTN_W, :], preferred_element_type=jnp.float32)
               + jnp.dot(yc_ref[rows, :], wo_ref[ATTN_W:D_MODEL, :], preferred_element_type=jnp.float32))
        x1 = x_ref[rows, :] + _rms(out) * gain1
        x1_ref[rows, :] = x1
        h2 = (_rms(x1) * gain2 + shift2_ref[...]).astype(jnp.bfloat16)
        h2_ref[rows, :] = h2
        logits = jnp.dot(h2, wr_ref[...], preferred_element_type=jnp.float32)
        routes.append(_route(logits.T))
    per_tile = DISPATCH_TILE // sub
    for d in range(tm // DISPATCH_TILE):
        e1, e2, w1, w2 = (jnp.concatenate(parts, axis=1)
                          for parts in zip(*routes[d * per_tile:(d + 1) * per_tile]))
        _plan(e1, e2, w1, w2, slots_ref.at[d], slots_t_ref.at[d], counts_ref.at[d])


def _out_proj_call(o, yc, w_o, x, mod, mod_per_batch, g_post1, g_pre2, w_router,
                   total_tokens, token_offset, carried):
    bsz, seq, _ = x.shape
    tm = OUT_TILES * DISPATCH_TILE
    assert seq % tm == 0 and token_offset % tm == 0 and total_tokens % tm == 0
    n_tiles = seq // tm
    tile0 = token_offset // tm
    n_all = total_tokens // DISPATCH_TILE

    def mod_spec(chunk):
        if mod_per_batch:
            return pl.BlockSpec((None, 1, D_MODEL), lambda b, i: (b, 0, chunk))
        return pl.BlockSpec((None, 1, D_MODEL), lambda b, i: (0, 0, chunk))

    def vec_spec():
        return pl.BlockSpec((1, D_MODEL), lambda b, i: (0, 0))

    tok = lambda w: pl.BlockSpec((None, tm, w), lambda b, i: (b, i, 0))
    flat = lambda w: pl.BlockSpec((tm, w), lambda b, i: (tile0 + b * n_tiles + i, 0))
    per_tile = lambda r, c: pl.BlockSpec((OUT_TILES, r, c), lambda b, i: (tile0 + b * n_tiles + i, 0, 0))
    in_specs = [
        tok(ATTN_W), tok(CONV_W),
        pl.BlockSpec((D_MODEL, D_MODEL), lambda b, i: (0, 0)),
        tok(D_MODEL),
        mod_spec(2), mod_spec(3), mod_spec(4),
        vec_spec(), vec_spec(),
        pl.BlockSpec((D_MODEL, LANES), lambda b, i: (0, 0)),
    ]
    args = [o, yc, w_o, x, mod, mod, mod, g_post1, g_pre2, w_router]
    aliases = {}
    if carried is not None:
        for j, arr in enumerate(carried):
            aliases[len(args)] = j
            in_specs.append(pl.BlockSpec(memory_space=pl.ANY))
            args.append(arr)
    return pl.pallas_call(
        _out_proj_kernel,
        grid=(bsz, n_tiles),
        in_specs=in_specs,
        out_specs=[flat(D_MODEL), flat(D_MODEL), per_tile(8, DISPATCH_TILE), per_tile(DISPATCH_TILE, LANES),
                   per_tile(N_EXPERTS, LANES)],
        out_shape=[
            jax.ShapeDtypeStruct((total_tokens, D_MODEL), jnp.float32),
            jax.ShapeDtypeStruct((total_tokens, D_MODEL), jnp.bfloat16),
            jax.ShapeDtypeStruct((n_all, 8, DISPATCH_TILE), jnp.float32),
            jax.ShapeDtypeStruct((n_all, DISPATCH_TILE, LANES), jnp.float32),
            jax.ShapeDtypeStruct((n_all, N_EXPERTS, LANES), jnp.float32),
        ],
        input_output_aliases=aliases,
        name="out_proj",
    )(*args)


def _plan(e1, e2, w1, w2, slots_ref, slots_t_ref, counts_ref):
    t = e1.shape[1]
    e1 = e1.astype(jnp.int32)
    e2 = e2.astype(jnp.int32)
    eid = lax.broadcasted_iota(jnp.int32, (N_EXPERTS, t), 0)
    hot1 = jnp.where(eid == e1, 1.0, 0.0)
    hot2 = jnp.where(eid == e2, 1.0, 0.0)
    hot = jnp.concatenate([hot1, hot2], axis=0).astype(jnp.bfloat16)
    before = (lax.broadcasted_iota(jnp.int32, (t, t), 0) < lax.broadcasted_iota(jnp.int32, (t, t), 1))
    before = jnp.where(before, 1.0, 0.0).astype(jnp.bfloat16)
    rank = jnp.dot(hot, before, preferred_element_type=jnp.float32)
    n1 = jnp.sum(hot1, axis=1, keepdims=True)
    n2 = jnp.sum(hot2, axis=1, keepdims=True)
    ones = jnp.ones((8, t), jnp.bfloat16)
    cnt_row = lax.dot_general(ones, (hot1 + hot2).astype(jnp.bfloat16), NT_DIMS,
                              preferred_element_type=jnp.float32)[0:1, :]
    pad_row = jnp.floor((cnt_row + (CHUNK - 1)) * (1.0 / CHUNK)) * CHUNK
    lower = (lax.broadcasted_iota(jnp.int32, (N_EXPERTS, N_EXPERTS), 1)
             < lax.broadcasted_iota(jnp.int32, (N_EXPERTS, N_EXPERTS), 0))
    base = jnp.sum(jnp.where(lower, pad_row, 0.0), axis=1, keepdims=True)
    slot1 = jnp.sum(hot1 * (base + rank[:N_EXPERTS]), axis=0, keepdims=True)
    slot2 = jnp.sum(hot2 * (base + n1 + rank[N_EXPERTS:]), axis=0, keepdims=True)
    slots_ref[...] = jnp.concatenate([slot1, slot2, w1, w2, jnp.zeros((4, t), jnp.float32)], axis=0)
    wide = jnp.concatenate([slot1, slot2, jnp.zeros((LANES - 2, t), jnp.float32)], axis=0)
    slots_t_ref[...] = wide.T
    counts_ref[...] = jnp.broadcast_to(n1 + n2, counts_ref.shape)


def _layout_tables(counts, n_blocks):
    cnt = counts[:, :, 0].astype(jnp.int32)
    n_tiles = cnt.shape[0]
    nch = (cnt + (CHUNK - 1)) // CHUNK
    csum = jnp.cumsum(nch, axis=0)
    rows = CHUNK * csum[-1]
    blocks = (rows + (EXPERT_BLOCK - 1)) // EXPERT_BLOCK
    blk_end = jnp.cumsum(blocks)
    blk_start = blk_end - blocks
    off = blk_start[None, :] * EXPERT_BLOCK + CHUNK * (csum - nch)
    first = jnp.cumsum(nch, axis=1) - nch
    used = jnp.sum(nch, axis=1)
    c = jnp.arange(N_CHUNKS, dtype=jnp.int32)
    e_idx = jnp.sum((c[None, :, None] >= first[:, None, :]).astype(jnp.int32), axis=2) - 1
    hot = e_idx[:, :, None] == jnp.arange(N_EXPERTS, dtype=jnp.int32)[None, None, :]
    pick = lambda a: jnp.sum(jnp.where(hot, a[:, None, :], 0), axis=2)
    row = pick(off) + CHUNK * (c[None, :] - pick(first))
    valid = c[None, :] < used[:, None]
    spare = (n_blocks * EXPERT_BLOCK + (jnp.arange(n_tiles, dtype=jnp.int32) % SPARE_SETS)[:, None] * TILE_SLOTS
             + CHUNK * c[None, :])
    scatter_rows = jnp.where(valid, row, spare)
    gather_rows = jnp.where(valid, row, 0)
    n_used = blk_end[-1]
    bc = jnp.minimum(jnp.arange(n_blocks, dtype=jnp.int32), n_used - 1)
    block_expert = jnp.sum((bc[:, None] >= blk_end[None, :]).astype(jnp.int32), axis=1)
    tail_start = blk_start * EXPERT_BLOCK + rows
    tail_chunks = (blocks * EXPERT_BLOCK - rows) // CHUNK
    flat = lambda a: a.reshape(-1).astype(jnp.int32)
    return (flat(scatter_rows // CHUNK), flat(gather_rows // CHUNK), flat(tail_start // CHUNK), flat(tail_chunks),
            flat(block_expert), flat(n_used))


def _chunk_copies(tile, chunk_ref, make_copy, action):
    for c in range(N_CHUNKS):
        action(make_copy(c, chunk_ref[tile * N_CHUNKS + c]))


def _wait_chunk_copies(make_copy):
    for _ in range(N_CHUNKS):
        make_copy(0, 0).wait()


def _dispatch_kernel(rows_ref, tail_start_ref, tail_chunks_ref, h_ref, slots_ref, xs_ref, buf, zbuf, sem, zsem):
    i = pl.program_id(0)
    n = pl.num_programs(0)
    slot = i % 2
    t = DISPATCH_TILE
    lane = lax.broadcasted_iota(jnp.int32, (SLOT_GROUP, LANES), 1)

    def copies(sl, d):
        def make(src_piece, dst_piece):
            return pltpu.make_async_copy(buf.at[sl, d, src_piece], xs_ref.at[dst_piece], sem.at[sl])
        return make

    for d in range(MOE_TILES):
        s = slots_ref[d]
        s1 = s[0:1, :].astype(jnp.int32)
        s2 = s[1:2, :].astype(jnp.int32)
        h = h_ref[d * t:(d + 1) * t, :]
        for g in range(TILE_SLOTS // SLOT_GROUP):
            rid = lax.broadcasted_iota(jnp.int32, (SLOT_GROUP, t), 0) + g * SLOT_GROUP
            m1 = rid == s1
            m2 = rid == s2
            perm = jnp.where(m1, 1.0, jnp.where(m2, 1.0, 0.0)).astype(jnp.bfloat16)
            rows = jnp.dot(perm, h, preferred_element_type=jnp.float32)
            w = jnp.sum(jnp.where(m1, s[2:3, :], 0.0) + jnp.where(m2, s[3:4, :], 0.0), axis=1, keepdims=True)
            hi = w.astype(jnp.bfloat16).astype(jnp.float32)
            aux = jnp.where(lane == 0, hi, jnp.where(lane == 1, w - hi, 0.0))
            pieces = slice(g * GROUP_CHUNKS, (g + 1) * GROUP_CHUNKS)
            buf[slot, d, pieces, :, 0:D_MODEL] = rows.astype(jnp.bfloat16).reshape(GROUP_CHUNKS, CHUNK, D_MODEL)
            buf[slot, d, pieces, :, D_MODEL:XS_W] = aux.astype(jnp.bfloat16).reshape(GROUP_CHUNKS, CHUNK, LANES)
        _chunk_copies(i * MOE_TILES + d, rows_ref, copies(slot, d), lambda c: c.start())

    @pl.when(i > 0)
    def _():
        for d in range(MOE_TILES):
            _wait_chunk_copies(copies(1 - slot, d))

    @pl.when(i == n - 1)
    def _():
        zbuf[...] = jnp.zeros_like(zbuf)

        def tails(action):
            for e in range(N_EXPERTS):
                start = tail_start_ref[e]

                def body(m, carry, start=start):
                    action(pltpu.make_async_copy(zbuf, xs_ref.at[start + m], zsem))
                    return carry

                lax.fori_loop(0, tail_chunks_ref[e], body, 0)

        tails(lambda c: c.start())
        tails(lambda c: c.wait())
        for d in range(MOE_TILES):
            _wait_chunk_copies(copies(slot, d))


def _dispatch_call(tables, h2, slots, n_blocks):
    scatter_rows, tail_start, tail_chunks = tables
    tokens = h2.shape[0]
    t = MOE_TILES * DISPATCH_TILE
    return pl.pallas_call(
        _dispatch_kernel,
        grid_spec=pltpu.PrefetchScalarGridSpec(
            num_scalar_prefetch=3,
            grid=(tokens // t,),
            in_specs=[
                pl.BlockSpec((t, D_MODEL), lambda i, *_: (i, 0)),
                pl.BlockSpec((MOE_TILES, 8, DISPATCH_TILE), lambda i, *_: (i, 0, 0)),
            ],
            out_specs=pl.BlockSpec(memory_space=pl.ANY),
            scratch_shapes=[
                pltpu.VMEM((2, MOE_TILES, N_CHUNKS, CHUNK, XS_W), jnp.bfloat16),
                pltpu.VMEM((CHUNK, XS_W), jnp.bfloat16),
                pltpu.SemaphoreType.DMA((2,)),
                pltpu.SemaphoreType.DMA(()),
            ],
        ),
        out_shape=jax.ShapeDtypeStruct(((n_blocks * EXPERT_BLOCK + SPARE_SETS * TILE_SLOTS) // CHUNK, CHUNK, XS_W),
                                       jnp.bfloat16),
        name="moe_dispatch",
    )(scatter_rows, tail_start, tail_chunks, h2, slots)


def _expert_kernel(bexp_ref, nused_ref, xs_ref, wg_ref, wu_ref, wd_ref, ys_ref, wgu_b, wd_b):
    b = pl.program_id(0)
    e = bexp_ref[b]
    prev = bexp_ref[jnp.maximum(b - 1, 0)]

    @pl.when((b == 0) | (e != prev))
    def _():
        wgu_b[:, 0:D_EXPERT] = wg_ref[...].astype(jnp.bfloat16)
        wgu_b[:, D_EXPERT:2 * D_EXPERT] = wu_ref[...].astype(jnp.bfloat16)
        wd_b[...] = wd_ref[...].astype(jnp.bfloat16)

    @pl.when(b < nused_ref[0])
    def _():
        x = xs_ref[:, 0:D_MODEL]
        y = None
        half = D_EXPERT // 2
        for c in range(2):
            cols = slice(c * half, (c + 1) * half)
            g = jnp.dot(x, wgu_b[:, c * half:(c + 1) * half], preferred_element_type=jnp.float32)
            u = jnp.dot(x, wgu_b[:, D_EXPERT + c * half:D_EXPERT + (c + 1) * half],
                        preferred_element_type=jnp.float32)
            act = (_silu(g) * u).astype(jnp.bfloat16)
            t = jnp.dot(act, wd_b[cols, :], preferred_element_type=jnp.float32)
            y = t if y is None else y + t
        w = (xs_ref[:, D_MODEL:D_MODEL + 1].astype(jnp.float32)
             + xs_ref[:, D_MODEL + 1:D_MODEL + 2].astype(jnp.float32))
        ys_ref[...] = (w * y).astype(jnp.bfloat16)


def _expert_call(block_expert, n_used, xs, w_eg, w_eu, w_ed, n_blocks):
    row_blk = lambda w: pl.BlockSpec((EXPERT_BLOCK, w), lambda b, be, nu: (jnp.minimum(b, nu[0] - 1), 0))
    wt_blk = lambda r, c: pl.BlockSpec((None, r, c), lambda b, be, nu: (be[b], 0, 0))
    return pl.pallas_call(
        _expert_kernel,
        grid_spec=pltpu.PrefetchScalarGridSpec(
            num_scalar_prefetch=2,
            grid=(n_blocks,),
            in_specs=[
                row_blk(XS_W),
                wt_blk(D_MODEL, D_EXPERT), wt_blk(D_MODEL, D_EXPERT), wt_blk(D_EXPERT, D_MODEL),
            ],
            out_specs=row_blk(D_MODEL),
            scratch_shapes=[
                pltpu.VMEM((D_MODEL, 2 * D_EXPERT), jnp.bfloat16),
                pltpu.VMEM((D_EXPERT, D_MODEL), jnp.bfloat16),
            ],
        ),
        out_shape=jax.ShapeDtypeStruct((n_blocks * EXPERT_BLOCK, D_MODEL), jnp.bfloat16),
        name="moe_experts",
    )(block_expert, n_used, xs, w_eg, w_eu, w_ed)


def _combine_kernel(tile0, rows_ref, ys_ref, slots_t_ref, x1_ref, gate2_ref, gpost_ref, out_ref, buf, sem):
    j = pl.program_id(0)
    n = pl.num_programs(0)
    slot = j % 2
    tile = tile0 + j * MOE_TILES

    def copies(sl, d):
        def make(dst_piece, src_piece):
            return pltpu.make_async_copy(ys_ref.at[src_piece], buf.at[sl, d, dst_piece], sem.at[sl])
        return make

    def fetch(first_tile, sl):
        for d in range(MOE_TILES):
            _chunk_copies(first_tile + d, rows_ref, copies(sl, d), lambda c: c.start())

    @pl.when(j == 0)
    def _():
        fetch(tile, slot)

    @pl.when(j + 1 < n)
    def _():
        fetch(tile + MOE_TILES, 1 - slot)

    for d in range(MOE_TILES):
        _wait_chunk_copies(copies(slot, d))
    gain = gate2_ref[...] * gpost_ref[...]
    lane = lax.broadcasted_iota(jnp.int32, (COMBINE_SUB_ROWS, TILE_SLOTS), 1)
    per_tile = DISPATCH_TILE // COMBINE_SUB_ROWS
    for d in range(MOE_TILES):
        sorted_rows = buf[slot, d].reshape(TILE_SLOTS, D_MODEL)
        for r in range(per_tile):
            rows = slice((d * per_tile + r) * COMBINE_SUB_ROWS, (d * per_tile + r + 1) * COMBINE_SUB_ROWS)
            local = slice(r * COMBINE_SUB_ROWS, (r + 1) * COMBINE_SUB_ROWS)
            s1 = slots_t_ref[d, local, 0:1].astype(jnp.int32)
            s2 = slots_t_ref[d, local, 1:2].astype(jnp.int32)
            unperm = jnp.where(lane == s1, 1.0, jnp.where(lane == s2, 1.0, 0.0)).astype(jnp.bfloat16)
            y = jnp.dot(unperm, sorted_rows, preferred_element_type=jnp.float32)
            out_ref[rows, :] = x1_ref[rows, :] + _rms(y) * gain


def _combine_call(gather_rows, ys, slots_t, x1, mod, mod_per_batch, g_post2, token_offset, bsz, seq):
    t = MOE_TILES * DISPATCH_TILE
    tile0 = token_offset // DISPATCH_TILE
    step0 = token_offset // t
    tokens = bsz * seq
    assert tokens % t == 0 and token_offset % t == 0 and (seq % t == 0 or not mod_per_batch)
    if mod_per_batch:
        per = seq // t
        gate2_spec = pl.BlockSpec((None, 1, D_MODEL), lambda j, *_: (j // per, 0, 5))
    else:
        gate2_spec = pl.BlockSpec((None, 1, D_MODEL), lambda j, *_: (0, 0, 5))
    y = pl.pallas_call(
        functools.partial(_combine_kernel, tile0),
        grid_spec=pltpu.PrefetchScalarGridSpec(
            num_scalar_prefetch=1,
            grid=(tokens // t,),
            in_specs=[
                pl.BlockSpec(memory_space=pl.ANY),
                pl.BlockSpec((MOE_TILES, DISPATCH_TILE, LANES), lambda j, *_: (step0 + j, 0, 0)),
                pl.BlockSpec((t, D_MODEL), lambda j, *_: (step0 + j, 0)),
                gate2_spec,
                pl.BlockSpec((1, D_MODEL), lambda j, *_: (0, 0)),
            ],
            out_specs=pl.BlockSpec((t, D_MODEL), lambda j, *_: (j, 0)),
            scratch_shapes=[
                pltpu.VMEM((2, MOE_TILES, N_CHUNKS, CHUNK, D_MODEL), jnp.bfloat16),
                pltpu.SemaphoreType.DMA((2,)),
            ],
        ),
        out_shape=jax.ShapeDtypeStruct((tokens, D_MODEL), jnp.float32),
        name="moe_combine",
    )(gather_rows, ys.reshape(-1, CHUNK, D_MODEL), slots_t, x1, mod, g_post2)
    return y.reshape(bsz, seq, D_MODEL)


def kernel(x_prompt, x_sample, cache_k, cache_v, c, c_ctx, w_mod, b_mod, g_pre1, g_post1, g_pre2, g_post2,
           w_in, conv_w, lambda_q1, lambda_k1, lambda_q2, lambda_k2, subln_g, w_o, w_router_group,
           w_router_expert, w_exp_gate, w_exp_up, w_exp_down):
    n_lat = c.shape[0]
    cond = jnp.concatenate(
        [c, c_ctx[None, :], jnp.zeros((MOD_ROWS - n_lat - 1, D_MODEL), jnp.float32)], axis=0)
    mod, lam = _mod_call(cond, w_mod[0], b_mod, lambda_q1, lambda_k1, lambda_q2, lambda_k2)
    mod = mod.reshape(MOD_ROWS, 1, 6 * D_MODEL)
    mod_lat, mod_ctx = mod[:n_lat], mod[n_lat:n_lat + 1]

    w_in_b = w_in[0].astype(jnp.bfloat16)
    w_o_b = w_o[0].astype(jnp.bfloat16)
    w_router = jnp.concatenate(
        [w_router_group[0], jnp.zeros((D_MODEL, 8 - N_GROUPS), jnp.float32), w_router_expert[0],
         jnp.zeros((D_MODEL, LANES - 8 - N_EXPERTS), jnp.float32)], axis=1).astype(jnp.bfloat16)

    n_prompt = x_prompt.shape[0] * x_prompt.shape[1]
    n_sample = x_sample.shape[0] * x_sample.shape[1]
    total = n_prompt + n_sample

    def mixer(x, mod_x, per_batch, rope, ctx_kv, token_offset, carried):
        kv_dtype = jnp.bfloat16 if rope else jnp.float32
        q, k, v, yc = _in_proj_call(x, mod_x, per_batch, g_pre1, w_in_b, conv_w[0], rope, kv_dtype)
        kvs = []
        if ctx_kv is not None:
            ck, cv = ctx_kv
            n_ctx = ck.shape[3]
            kvs.append((ck, cv, lambda sp, hp: pl.BlockSpec((sp, None, hp, n_ctx, D_V),
                                                            lambda b, h, i: (b, 0, h, 0, 0))))
        seq = x.shape[1]
        kvs.append((k, v, lambda sp, hp: pl.BlockSpec((sp, hp, seq, D_V), lambda b, h, i: (b, h, 0, 0))))
        short = seq <= MERGE_HEADS_MAX_SEQ
        o = _attn_call(lam, subln_g, q, kvs, SHORT_SEQS_PER_STEP if short else 1,
                       N_HEADS if short else LONG_SEQ_HEADS_PER_STEP)
        if not per_batch:
            o, yc, x = (a.reshape(1, -1, a.shape[-1]) for a in (o, yc, x))
        shared = _out_proj_call(o, yc, w_o_b, x, mod_x, per_batch, g_post1, g_pre2, w_router,
                                total, token_offset, carried)
        return shared, k, v

    shared, kp, vp = mixer(x_prompt, mod_ctx, False, False, None, 0, None)
    (x1, h2, slots, slots_t, counts), _, _ = mixer(x_sample, mod_lat, True, True, (cache_k, cache_v),
                                                   n_prompt, shared)

    n_tiles = total // DISPATCH_TILE
    max_rows = 2 * total + n_tiles * N_EXPERTS * (CHUNK - 1) + N_EXPERTS * (EXPERT_BLOCK - CHUNK)
    n_blocks = -(-max_rows // EXPERT_BLOCK)
    scatter_rows, gather_rows, tail_start, tail_chunks, block_expert, n_used = _layout_tables(counts, n_blocks)
    xs = _dispatch_call((scatter_rows, tail_start, tail_chunks), h2, slots, n_blocks)
    ys = _expert_call(block_expert, n_used, xs.reshape(-1, XS_W), w_exp_gate[0], w_exp_up[0], w_exp_down[0],
                      n_blocks)
    yp = _combine_call(gather_rows, ys, slots_t, x1, mod_ctx, False, g_post2, 0,
                       x_prompt.shape[0], x_prompt.shape[1])
    ysamp = _combine_call(gather_rows, ys, slots_t, x1, mod_lat, True, g_post2, n_prompt,
                          x_sample.shape[0], x_sample.shape[1])
    return yp, ysamp, kp[:, None], vp[:, None]
```

```python
import functools
import math

import numpy as np
import jax
import jax.numpy as jnp
from jax import lax
from jax.experimental import pallas as pl
from jax.experimental.pallas import tpu as pltpu

D_MODEL = 1024
GRID_W = 64
N_HEADS = 4
D_QK = 64
D_V = 128
ATTN_W = N_HEADS * D_V
CONV_W = D_MODEL - ATTN_W
IN_W = 3 * ATTN_W + 3 * CONV_W
N_GROUPS = 4
EXP_PER_GROUP = 4
N_EXPERTS = N_GROUPS * EXP_PER_GROUP
D_EXPERT = 512
ROPE_BASE = 10000.0
EPS = 1e-6
LAM_INIT = 0.8 - 0.6 * math.exp(-0.3 * 0)

LANES = 128
F32_SUBLANES = 8
BF16_SUBLANES = 16
MOD_ROWS = 16
TOKEN_TILE = 1024
OUT_SUB_ROWS = 256
OUT_TILES = 2
Q_TILE = 2048
KEY_CHUNK = 256
LONG_SEQ_HEADS_PER_STEP = 2
MERGE_HEADS_MAX_SEQ = 512
SHORT_SEQS_PER_STEP = 4
DISPATCH_TILE = 512
CHUNK = BF16_SUBLANES
TILE_SLOTS = 1280
N_CHUNKS = TILE_SLOTS // CHUNK
SLOT_GROUP = 256
GROUP_CHUNKS = SLOT_GROUP // CHUNK
MOE_TILES = 2
SPARE_SETS = 2 * MOE_TILES
COMBINE_SUB_ROWS = 256
EXPERT_BLOCK = 1024
QK_SCALE = (1.0 / math.sqrt(D_QK)) * math.log2(math.e)

NT_DIMS = (((1,), (1,)), ((), ()))


def _rms(x):
    return x * lax.rsqrt(jnp.mean(x * x, axis=-1, keepdims=True) + EPS)


def _silu(x):
    return x * (1.0 / (1.0 + jnp.exp(-x)))


def _mod_kernel(cond_ref, w_ref, b_ref, lq1_ref, lk1_ref, lq2_ref, lk2_ref, mod_ref, lam_ref):
    s = _silu(cond_ref[...])
    s_hi = s.astype(jnp.bfloat16)
    s_lo = (s - s_hi.astype(jnp.float32)).astype(jnp.bfloat16)
    w = w_ref[...]
    w_hi = w.astype(jnp.bfloat16)
    w_lo = (w - w_hi.astype(jnp.float32)).astype(jnp.bfloat16)
    both = jnp.dot(jnp.concatenate([s_hi, s_lo], axis=0), w_hi, preferred_element_type=jnp.float32)
    m = both[0:MOD_ROWS] + both[MOD_ROWS:2 * MOD_ROWS] + jnp.dot(s_hi, w_lo, preferred_element_type=jnp.float32)
    mod_ref[...] = m + b_ref[...]
    a = jnp.sum(lq1_ref[...] * lk1_ref[...], axis=-1, keepdims=True)
    b = jnp.sum(lq2_ref[...] * lk2_ref[...], axis=-1, keepdims=True)
    lam_ref[...] = jnp.broadcast_to(jnp.exp(a) - jnp.exp(b) + LAM_INIT, lam_ref.shape)


def _mod_call(cond, w_mod, b_mod, lq1, lk1, lq2, lk2):
    n_col = 6 * D_MODEL
    col_tile = 1536
    small = pl.BlockSpec((1, D_QK), lambda j: (0, 0))
    return pl.pallas_call(
        _mod_kernel,
        grid=(n_col // col_tile,),
        in_specs=[
            pl.BlockSpec((MOD_ROWS, D_MODEL), lambda j: (0, 0)),
            pl.BlockSpec((D_MODEL, col_tile), lambda j: (0, j)),
            pl.BlockSpec((1, col_tile), lambda j: (0, j)),
            small, small, small, small,
        ],
        out_specs=[
            pl.BlockSpec((MOD_ROWS, col_tile), lambda j: (0, j)),
            pl.BlockSpec((1, LANES), lambda j: (0, 0)),
        ],
        out_shape=[
            jax.ShapeDtypeStruct((MOD_ROWS, n_col), jnp.float32),
            jax.ShapeDtypeStruct((1, LANES), jnp.float32),
        ],
        name="mod",
    )(cond, w_mod, b_mod, lq1, lk1, lq2, lk2)


def _in_proj_kernel(rope, n_tiles, seq_len, x_ref, xp_ref, xn_ref, shift_ref, scale_ref, g_ref, w_ref, cw_ref,
                    *rest):
    if rope:
        cos_ref, sina_ref, sinb_ref, q_ref, k_ref, v_ref, yc_ref = rest
    else:
        q_ref, k_ref, v_ref, yc_ref = rest
    i = pl.program_id(1)
    tm = x_ref.shape[0]
    spt = q_ref.shape[0]
    per_seq = tm // spt
    gain = g_ref[...] * (1.0 + scale_ref[...])
    shift = shift_ref[...]

    def modulate(x):
        return (_rms(x) * gain + shift).astype(jnp.bfloat16)

    h = modulate(x_ref[...])
    h_halo = modulate(jnp.concatenate([xp_ref[...], xn_ref[...]], axis=0))

    def proj(lhs, lo, hi):
        return jnp.dot(lhs, w_ref[:, lo:hi], preferred_element_type=jnp.float32)

    def rot(t):
        return (t * cos_ref[...] + pltpu.roll(t, LANES - 16, axis=1) * sina_ref[...]
                + pltpu.roll(t, 16, axis=1) * sinb_ref[...])

    c0 = 3 * ATTN_W
    h_ext = jnp.concatenate([h, h_halo], axis=0)
    cu_all = proj(h_ext, c0 + CONV_W, c0 + 2 * CONV_W) * proj(h_ext, c0 + 2 * CONV_W, c0 + 3 * CONV_W)
    gb = proj(h, c0, c0 + CONV_W)
    cu = cu_all[0:tm]
    prev_row = jnp.where(i > 0, cu_all[tm + 7:tm + 8], 0.0)
    next_row = jnp.where(i < n_tiles - 1, cu_all[tm + 8:tm + 9], 0.0)
    row = lax.broadcasted_iota(jnp.int32, cu.shape, 0)
    pos = (i * tm + row) & (seq_len - 1)
    prev = jnp.where(row == 0, prev_row, pltpu.roll(cu, 1, axis=0))
    nxt = jnp.where(row == tm - 1, next_row, pltpu.roll(cu, tm - 1, axis=0))
    prev = jnp.where(pos == 0, 0.0, prev)
    nxt = jnp.where(pos == seq_len - 1, 0.0, nxt)
    conv = cw_ref[0:1, :] * prev + cw_ref[1:2, :] * cu + cw_ref[2:3, :] * nxt
    yc_ref[...] = (gb * conv).astype(yc_ref.dtype)

    def per_head(z, ref, finish):
        for hd in range(N_HEADS):
            t = finish(z[:, hd * D_V:(hd + 1) * D_V]).astype(ref.dtype)
            for sq in range(spt):
                ref[sq, hd] = t[sq * per_seq:(sq + 1) * per_seq]

    per_head(proj(h, 0, ATTN_W), q_ref, lambda t: (rot(t) if rope else t) * QK_SCALE)
    per_head(proj(h, ATTN_W, 2 * ATTN_W), k_ref, lambda t: rot(t) if rope else t)
    per_head(proj(h, 2 * ATTN_W, 3 * ATTN_W), v_ref, lambda t: t)


def _rope_tables(seq):
    n_rows = seq // GRID_W
    row = np.repeat(np.arange(n_rows), GRID_W).astype(np.float64)
    col = np.tile(np.arange(GRID_W), n_rows).astype(np.float64)
    nf = D_QK // 4
    inv = ROPE_BASE ** (-np.arange(nf, dtype=np.float64) / nf)
    ar = row[:, None] * inv
    ac = col[:, None] * inv
    ang = np.concatenate([ar, ar, ac, ac], axis=-1)
    ang = np.concatenate([ang, ang], axis=-1)
    first_half = (np.arange(LANES) % 32) < 16
    cos = np.cos(ang)
    sin = np.sin(ang)
    sina = np.where(first_half, -sin, 0.0)
    sinb = np.where(first_half, 0.0, sin)
    return tuple(jnp.asarray(t, dtype=jnp.float32) for t in (cos, sina, sinb))


def _in_proj_call(x, mod, mod_per_batch, g_pre1, w_in, conv_w, rope, kv_dtype):
    n_seq, seq_len, _ = x.shape
    assert seq_len & (seq_len - 1) == 0
    spt = 1 if (mod_per_batch or seq_len >= TOKEN_TILE) else TOKEN_TILE // seq_len
    x = x.reshape(n_seq // spt, spt * seq_len, D_MODEL)
    bsz, seq, _ = x.shape
    tm = min(TOKEN_TILE, seq)
    n_tiles = seq // tm
    halo = tm // F32_SUBLANES
    n_halo = seq // F32_SUBLANES

    def mod_spec(chunk):
        if mod_per_batch:
            return pl.BlockSpec((None, 1, D_MODEL), lambda b, i: (b, 0, chunk))
        return pl.BlockSpec((None, 1, D_MODEL), lambda b, i: (0, 0, chunk))

    in_specs = [
        pl.BlockSpec((None, tm, D_MODEL), lambda b, i: (b, i, 0)),
        pl.BlockSpec((None, F32_SUBLANES, D_MODEL), lambda b, i: (b, jnp.maximum(i * halo - 1, 0), 0)),
        pl.BlockSpec((None, F32_SUBLANES, D_MODEL), lambda b, i: (b, jnp.minimum((i + 1) * halo, n_halo - 1), 0)),
        mod_spec(0), mod_spec(1),
        pl.BlockSpec((1, D_MODEL), lambda b, i: (0, 0)),
        pl.BlockSpec((D_MODEL, IN_W), lambda b, i: (0, 0)),
        pl.BlockSpec((3, CONV_W), lambda b, i: (0, 0)),
    ]
    args = [x, x, x, mod, mod, g_pre1, w_in, conv_w]
    if rope:
        in_specs += [pl.BlockSpec((tm, LANES), lambda b, i: (i, 0))] * 3
        args += list(_rope_tables(seq))
    head_spec = pl.BlockSpec((spt, N_HEADS, tm // spt, D_V), lambda b, i: (b, 0, i, 0))
    head_shape = lambda dt: jax.ShapeDtypeStruct((n_seq, N_HEADS, seq_len, D_V), dt)
    q, k, v, yc = pl.pallas_call(
        functools.partial(_in_proj_kernel, rope, n_tiles, seq_len),
        grid=(bsz, n_tiles),
        in_specs=in_specs,
        out_specs=[head_spec, head_spec, head_spec, pl.BlockSpec((None, tm, CONV_W), lambda b, i: (b, i, 0))],
        out_shape=[head_shape(jnp.bfloat16), head_shape(kv_dtype), head_shape(kv_dtype),
                   jax.ShapeDtypeStruct((bsz, seq, CONV_W), jnp.bfloat16)],
        name="in_proj_rope" if rope else "in_proj",
    )(*args)
    return q, k, v, yc.reshape(n_seq, seq_len, CONV_W)


def _attn_kernel(n_kv, lam_ref, g_ref, q_ref, *rest):
    kv_refs = rest[:2 * n_kv]
    o_ref = rest[2 * n_kv]
    seqs, heads, tq, _ = q_ref.shape
    for sq in range(seqs):
        _attend(sq, heads, tq, lam_ref, g_ref, q_ref, kv_refs, o_ref)


def _attend(sq, heads, tq, lam_ref, g_ref, q_ref, kv_refs, o_ref):
    n_kv = len(kv_refs) // 2
    lane = lax.broadcasted_iota(jnp.int32, (tq, 2 * D_QK), 1)
    for hh in range(heads):
        q = q_ref[sq, hh]
        zero = jnp.zeros_like(q)
        halves = (jnp.where(lane < D_QK, q, zero), jnp.where(lane >= D_QK, q, zero))
        acc = [jnp.zeros((tq, 2 * D_V), jnp.float32) for _ in range(2)]
        m = [jnp.full((tq, 1), -1e30, jnp.float32) for _ in range(2)]
        for j in range(n_kv):
            k_ref, v_ref = kv_refs[2 * j], kv_refs[2 * j + 1]
            n_keys = k_ref.shape[2]
            ck = min(KEY_CHUNK, n_keys)
            ones = jnp.ones((ck, D_V), jnp.bfloat16)
            for c in range(n_keys // ck):
                k = k_ref[sq, hh, c * ck:(c + 1) * ck, :].astype(jnp.bfloat16)
                v1 = jnp.concatenate([v_ref[sq, hh, c * ck:(c + 1) * ck, :].astype(jnp.bfloat16), ones], axis=1)
                for x, qh in enumerate(halves):
                    s = lax.dot_general(qh, k, NT_DIMS, preferred_element_type=jnp.float32)
                    m_new = jnp.maximum(m[x], jnp.max(s, axis=-1, keepdims=True))
                    p = jnp.exp2(s - m_new).astype(jnp.bfloat16)
                    acc[x] = (jnp.exp2(m[x] - m_new) * acc[x]
                              + jnp.dot(p, v1, preferred_element_type=jnp.float32))
                    m[x] = m_new
        o = (acc[0][:, 0:D_V] / acc[0][:, D_V:2 * D_V]
             - lam_ref[0:1, 0:1] * (acc[1][:, 0:D_V] / acc[1][:, D_V:2 * D_V]))
        o_ref[sq, :, hh * D_V:(hh + 1) * D_V] = (_rms(o) * (g_ref[...] * (1.0 - LAM_INIT))).astype(o_ref.dtype)


def _attn_call(lam, subln_g, q, kvs, seqs_per_step, heads_per_step):
    bsz, _, seq, _ = q.shape
    tq = min(Q_TILE, seq)
    sp, hp = seqs_per_step, heads_per_step
    in_specs = [
        pl.BlockSpec((1, LANES), lambda b, h, i: (0, 0)),
        pl.BlockSpec((1, D_V), lambda b, h, i: (0, 0)),
        pl.BlockSpec((sp, hp, tq, D_V), lambda b, h, i: (b, h, i, 0)),
    ]
    args = [lam, subln_g, q]
    for k, v, spec_fn in kvs:
        in_specs += [spec_fn(sp, hp), spec_fn(sp, hp)]
        args += [k, v]
    return pl.pallas_call(
        functools.partial(_attn_kernel, len(kvs)),
        grid=(bsz // sp, N_HEADS // hp, seq // tq),
        in_specs=in_specs,
        out_specs=pl.BlockSpec((sp, tq, hp * D_V), lambda b, h, i: (b, i, h)),
        out_shape=jax.ShapeDtypeStruct((bsz, seq, ATTN_W), jnp.bfloat16),
        name="attn%d" % len(kvs),
    )(*args)


def _route(logits_t):
    lg = [logits_t[g:g + 1, :] for g in range(N_GROUPS)]
    mg = functools.reduce(jnp.maximum, lg)
    p_sel = 1.0 / functools.reduce(jnp.add, [jnp.exp(t - mg) for t in lg])
    g_sel = jnp.full(mg.shape, N_GROUPS - 1, jnp.int32)
    for g in range(N_GROUPS - 2, -1, -1):
        g_sel = jnp.where(lg[g] == mg, g, g_sel)
    le = []
    for j in range(EXP_PER_GROUP):
        t = jnp.zeros_like(mg)
        for g in range(N_GROUPS):
            r = 8 + g * EXP_PER_GROUP + j
            t = jnp.where(g_sel == g, logits_t[r:r + 1, :], t)
        le.append(t)
    m1 = functools.reduce(jnp.maximum, le)
    i1 = jnp.full(mg.shape, EXP_PER_GROUP - 1, jnp.int32)
    for j in range(EXP_PER_GROUP - 2, -1, -1):
        i1 = jnp.where(le[j] == m1, j, i1)
    neg = jnp.float32(-jnp.inf)
    rest = [jnp.where(i1 == j, neg, le[j]) for j in range(EXP_PER_GROUP)]
    m2 = functools.reduce(jnp.maximum, rest)
    i2 = jnp.full(mg.shape, EXP_PER_GROUP - 1, jnp.int32)
    for j in range(EXP_PER_GROUP - 2, -1, -1):
        i2 = jnp.where(rest[j] == m2, j, i2)
    e2 = jnp.exp(m2 - m1)
    w1 = p_sel / (1.0 + e2)
    w2 = p_sel * e2 / (1.0 + e2)
    base = g_sel * EXP_PER_GROUP
    return (base + i1).astype(jnp.float32), (base + i2).astype(jnp.float32), w1, w2


def _out_proj_kernel(o_ref, yc_ref, wo_ref, x_ref, gate1_ref, shift2_ref, scale2_ref, gpost_ref, gpre_ref,
                     wr_ref, *rest):
    x1_ref, h2_ref, slots_ref, slots_t_ref, counts_ref = rest[-5:]
    tm = o_ref.shape[0]
    sub = min(OUT_SUB_ROWS, tm)
    gain1 = gate1_ref[...] * gpost_ref[...]
    gain2 = gpre_ref[...] * (1.0 + scale2_ref[...])
    routes = []
    for r in range(tm // sub):
        rows = slice(r * sub, (r + 1) * sub)
        out = (jnp.dot(o_ref[rows, :], wo_ref[0:ATTN_W, :], preferred_element_type=jnp.float32)
               + jnp.dot(yc_ref[rows, :], wo_ref[ATTN_W:D_MODEL, :], preferred_element_type=jnp.float32))
        x1 = x_ref[rows, :] + _rms(out) * gain1
        x1_ref[rows, :] = x1
        h2 = (_rms(x1) * gain2 + shift2_ref[...]).astype(jnp.bfloat16)
        h2_ref[rows, :] = h2
        logits = jnp.dot(h2, wr_ref[...], preferred_element_type=jnp.float32)
        routes.append(_route(logits.T))
    per_tile = DISPATCH_TILE // sub
    for d in range(tm // DISPATCH_TILE):
        e1, e2, w1, w2 = (jnp.concatenate(parts, axis=1)
                          for parts in zip(*routes[d * per_tile:(d + 1) * per_tile]))
        _plan(e1, e2, w1, w2, slots_ref.at[d], slots_t_ref.at[d], counts_ref.at[d])


def _out_proj_call(o, yc, w_o, x, mod, mod_per_batch, g_post1, g_pre2, w_router,
                   total_tokens, token_offset, carried):
    bsz, seq, _ = x.shape
    tm = OUT_TILES * DISPATCH_TILE
    assert seq % tm == 0 and token_offset % tm == 0 and total_tokens % tm == 0
    n_tiles = seq // tm
    tile0 = token_offset // tm
    n_all = total_tokens // DISPATCH_TILE

    def mod_spec(chunk):
        if mod_per_batch:
            return pl.BlockSpec((None, 1, D_MODEL), lambda b, i: (b, 0, chunk))
        return pl.BlockSpec((None, 1, D_MODEL), lambda b, i: (0, 0, chunk))

    def vec_spec():
        return pl.BlockSpec((1, D_MODEL), lambda b, i: (0, 0))

    tok = lambda w: pl.BlockSpec((None, tm, w), lambda b, i: (b, i, 0))
    flat = lambda w: pl.BlockSpec((tm, w), lambda b, i: (tile0 + b * n_tiles + i, 0))
    per_tile = lambda r, c: pl.BlockSpec((OUT_TILES, r, c), lambda b, i: (tile0 + b * n_tiles + i, 0, 0))
    in_specs = [
        tok(ATTN_W), tok(CONV_W),
        pl.BlockSpec((D_MODEL, D_MODEL), lambda b, i: (0, 0)),
        tok(D_MODEL),
        mod_spec(2), mod_spec(3), mod_spec(4),
        vec_spec(), vec_spec(),
        pl.BlockSpec((D_MODEL, LANES), lambda b, i: (0, 0)),
    ]
    args = [o, yc, w_o, x, mod, mod, mod, g_post1, g_pre2, w_router]
    aliases = {}
    if carried is not None:
        for j, arr in enumerate(carried):
            aliases[len(args)] = j
            in_specs.append(pl.BlockSpec(memory_space=pl.ANY))
            args.append(arr)
    return pl.pallas_call(
        _out_proj_kernel,
        grid=(bsz, n_tiles),
        in_specs=in_specs,
        out_specs=[flat(D_MODEL), flat(D_MODEL), per_tile(8, DISPATCH_TILE), per_tile(DISPATCH_TILE, LANES),
                   per_tile(N_EXPERTS, LANES)],
        out_shape=[
            jax.ShapeDtypeStruct((total_tokens, D_MODEL), jnp.float32),
            jax.ShapeDtypeStruct((total_tokens, D_MODEL), jnp.bfloat16),
            jax.ShapeDtypeStruct((n_all, 8, DISPATCH_TILE), jnp.float32),
            jax.ShapeDtypeStruct((n_all, DISPATCH_TILE, LANES), jnp.float32),
            jax.ShapeDtypeStruct((n_all, N_EXPERTS, LANES), jnp.float32),
        ],
        input_output_aliases=aliases,
        name="out_proj",
    )(*args)


def _plan(e1, e2, w1, w2, slots_ref, slots_t_ref, counts_ref):
    t = e1.shape[1]
    e1 = e1.astype(jnp.int32)
    e2 = e2.astype(jnp.int32)
    eid = lax.broadcasted_iota(jnp.int32, (N_EXPERTS, t), 0)
    hot1 = jnp.where(eid == e1, 1.0, 0.0)
    hot2 = jnp.where(eid == e2, 1.0, 0.0)
    hot = jnp.concatenate([hot1, hot2], axis=0).astype(jnp.bfloat16)
    before = (lax.broadcasted_iota(jnp.int32, (t, t), 0) < lax.broadcasted_iota(jnp.int32, (t, t), 1))
    before = jnp.where(before, 1.0, 0.0).astype(jnp.bfloat16)
    rank = jnp.dot(hot, before, preferred_element_type=jnp.float32)
    n1 = jnp.sum(hot1, axis=1, keepdims=True)
    n2 = jnp.sum(hot2, axis=1, keepdims=True)
    ones = jnp.ones((8, t), jnp.bfloat16)
    cnt_row = lax.dot_general(ones, (hot1 + hot2).astype(jnp.bfloat16), NT_DIMS,
                              preferred_element_type=jnp.float32)[0:1, :]
    pad_row = jnp.floor((cnt_row + (CHUNK - 1)) * (1.0 / CHUNK)) * CHUNK
    lower = (lax.broadcasted_iota(jnp.int32, (N_EXPERTS, N_EXPERTS), 1)
             < lax.broadcasted_iota(jnp.int32, (N_EXPERTS, N_EXPERTS), 0))
    base = jnp.sum(jnp.where(lower, pad_row, 0.0), axis=1, keepdims=True)
    slot1 = jnp.sum(hot1 * (base + rank[:N_EXPERTS]), axis=0, keepdims=True)
    slot2 = jnp.sum(hot2 * (base + n1 + rank[N_EXPERTS:]), axis=0, keepdims=True)
    slots_ref[...] = jnp.concatenate([slot1, slot2, jnp.zeros((6, t), jnp.float32)], axis=0)
    wide = jnp.concatenate([slot1, slot2, w1, w2, jnp.zeros((LANES - 4, t), jnp.float32)], axis=0)
    slots_t_ref[...] = wide.T
    counts_ref[...] = jnp.broadcast_to(n1 + n2, counts_ref.shape)


def _layout_tables(counts, n_blocks):
    cnt = counts[:, :, 0].astype(jnp.int32)
    n_tiles = cnt.shape[0]
    nch = (cnt + (CHUNK - 1)) // CHUNK
    csum = jnp.cumsum(nch, axis=0)
    rows = CHUNK * csum[-1]
    blocks = (rows + (EXPERT_BLOCK - 1)) // EXPERT_BLOCK
    blk_end = jnp.cumsum(blocks)
    blk_start = blk_end - blocks
    off = blk_start[None, :] * EXPERT_BLOCK + CHUNK * (csum - nch)
    first = jnp.cumsum(nch, axis=1) - nch
    used = jnp.sum(nch, axis=1)
    c = jnp.arange(N_CHUNKS, dtype=jnp.int32)
    e_idx = jnp.sum((c[None, :, None] >= first[:, None, :]).astype(jnp.int32), axis=2) - 1
    hot = e_idx[:, :, None] == jnp.arange(N_EXPERTS, dtype=jnp.int32)[None, None, :]
    pick = lambda a: jnp.sum(jnp.where(hot, a[:, None, :], 0), axis=2)
    row = pick(off) + CHUNK * (c[None, :] - pick(first))
    valid = c[None, :] < used[:, None]
    spare = (n_blocks * EXPERT_BLOCK + (jnp.arange(n_tiles, dtype=jnp.int32) % SPARE_SETS)[:, None] * TILE_SLOTS
             + CHUNK * c[None, :])
    scatter_rows = jnp.where(valid, row, spare)
    gather_rows = jnp.where(valid, row, 0)
    n_used = blk_end[-1]
    bc = jnp.minimum(jnp.arange(n_blocks, dtype=jnp.int32), n_used - 1)
    block_expert = jnp.sum((bc[:, None] >= blk_end[None, :]).astype(jnp.int32), axis=1)
    tail_start = blk_start * EXPERT_BLOCK + rows
    tail_chunks = (blocks * EXPERT_BLOCK - rows) // CHUNK
    flat = lambda a: a.reshape(-1).astype(jnp.int32)
    return (flat(scatter_rows // CHUNK), flat(gather_rows // CHUNK), flat(tail_start // CHUNK), flat(tail_chunks),
            flat(block_expert), flat(n_used))


def _chunk_copies(tile, chunk_ref, make_copy, action):
    for c in range(N_CHUNKS):
        action(make_copy(c, chunk_ref[tile * N_CHUNKS + c]))


def _wait_chunk_copies(make_copy):
    for _ in range(N_CHUNKS):
        make_copy(0, 0).wait()


def _dispatch_kernel(rows_ref, tail_start_ref, tail_chunks_ref, h_ref, slots_ref, xs_ref, buf, zbuf, sem, zsem):
    i = pl.program_id(0)
    n = pl.num_programs(0)
    slot = i % 2
    t = DISPATCH_TILE

    def copies(sl, d):
        def make(src_piece, dst_piece):
            return pltpu.make_async_copy(buf.at[sl, d, src_piece], xs_ref.at[dst_piece], sem.at[sl])
        return make

    for d in range(MOE_TILES):
        s = slots_ref[d]
        s1 = s[0:1, :].astype(jnp.int32)
        s2 = s[1:2, :].astype(jnp.int32)
        h = h_ref[d * t:(d + 1) * t, :]
        for g in range(TILE_SLOTS // SLOT_GROUP):
            rid = lax.broadcasted_iota(jnp.int32, (SLOT_GROUP, t), 0) + g * SLOT_GROUP
            perm = jnp.where(rid == s1, 1.0, jnp.where(rid == s2, 1.0, 0.0)).astype(jnp.bfloat16)
            rows = jnp.dot(perm, h, preferred_element_type=jnp.float32)
            pieces = slice(g * GROUP_CHUNKS, (g + 1) * GROUP_CHUNKS)
            buf[slot, d, pieces] = rows.astype(jnp.bfloat16).reshape(GROUP_CHUNKS, CHUNK, D_MODEL)
        _chunk_copies(i * MOE_TILES + d, rows_ref, copies(slot, d), lambda c: c.start())

    @pl.when(i > 0)
    def _():
        for d in range(MOE_TILES):
            _wait_chunk_copies(copies(1 - slot, d))

    @pl.when(i == n - 1)
    def _():
        zbuf[...] = jnp.zeros_like(zbuf)

        def tails(action):
            for e in range(N_EXPERTS):
                start = tail_start_ref[e]

                def body(m, carry, start=start):
                    action(pltpu.make_async_copy(zbuf, xs_ref.at[start + m], zsem))
                    return carry

                lax.fori_loop(0, tail_chunks_ref[e], body, 0)

        tails(lambda c: c.start())
        tails(lambda c: c.wait())
        for d in range(MOE_TILES):
            _wait_chunk_copies(copies(slot, d))


def _dispatch_call(tables, h2, slots, n_blocks):
    scatter_rows, tail_start, tail_chunks = tables
    tokens = h2.shape[0]
    t = MOE_TILES * DISPATCH_TILE
    return pl.pallas_call(
        _dispatch_kernel,
        grid_spec=pltpu.PrefetchScalarGridSpec(
            num_scalar_prefetch=3,
            grid=(tokens // t,),
            in_specs=[
                pl.BlockSpec((t, D_MODEL), lambda i, *_: (i, 0)),
                pl.BlockSpec((MOE_TILES, 8, DISPATCH_TILE), lambda i, *_: (i, 0, 0)),
            ],
            out_specs=pl.BlockSpec(memory_space=pl.ANY),
            scratch_shapes=[
                pltpu.VMEM((2, MOE_TILES, N_CHUNKS, CHUNK, D_MODEL), jnp.bfloat16),
                pltpu.VMEM((CHUNK, D_MODEL), jnp.bfloat16),
                pltpu.SemaphoreType.DMA((2,)),
                pltpu.SemaphoreType.DMA(()),
            ],
        ),
        out_shape=jax.ShapeDtypeStruct(((n_blocks * EXPERT_BLOCK + SPARE_SETS * TILE_SLOTS) // CHUNK, CHUNK, D_MODEL),
                                       jnp.bfloat16),
        name="moe_dispatch",
    )(scatter_rows, tail_start, tail_chunks, h2, slots)


def _expert_kernel(bexp_ref, nused_ref, xs_ref, wg_ref, wu_ref, wd_ref, ys_ref, wgu_b, wd_b):
    b = pl.program_id(0)
    e = bexp_ref[b]
    prev = bexp_ref[jnp.maximum(b - 1, 0)]

    @pl.when((b == 0) | (e != prev))
    def _():
        wgu_b[:, 0:D_EXPERT] = wg_ref[...].astype(jnp.bfloat16)
        wgu_b[:, D_EXPERT:2 * D_EXPERT] = wu_ref[...].astype(jnp.bfloat16)
        wd_b[...] = wd_ref[...].astype(jnp.bfloat16)

    @pl.when(b < nused_ref[0])
    def _():
        x = xs_ref[...]
        y = None
        half = D_EXPERT // 2
        for c in range(2):
            cols = slice(c * half, (c + 1) * half)
            g = jnp.dot(x, wgu_b[:, c * half:(c + 1) * half], preferred_element_type=jnp.float32)
            u = jnp.dot(x, wgu_b[:, D_EXPERT + c * half:D_EXPERT + (c + 1) * half],
                        preferred_element_type=jnp.float32)
            act = (_silu(g) * u).astype(jnp.bfloat16)
            t = jnp.dot(act, wd_b[cols, :], preferred_element_type=jnp.float32)
            y = t if y is None else y + t
        ys_ref[...] = y.astype(jnp.bfloat16)


def _expert_call(block_expert, n_used, xs, w_eg, w_eu, w_ed, n_blocks):
    row_blk = lambda w: pl.BlockSpec((EXPERT_BLOCK, w), lambda b, be, nu: (jnp.minimum(b, nu[0] - 1), 0))
    wt_blk = lambda r, c: pl.BlockSpec((None, r, c), lambda b, be, nu: (be[b], 0, 0))
    return pl.pallas_call(
        _expert_kernel,
        grid_spec=pltpu.PrefetchScalarGridSpec(
            num_scalar_prefetch=2,
            grid=(n_blocks,),
            in_specs=[
                row_blk(D_MODEL),
                wt_blk(D_MODEL, D_EXPERT), wt_blk(D_MODEL, D_EXPERT), wt_blk(D_EXPERT, D_MODEL),
            ],
            out_specs=row_blk(D_MODEL),
            scratch_shapes=[
                pltpu.VMEM((D_MODEL, 2 * D_EXPERT), jnp.bfloat16),
                pltpu.VMEM((D_EXPERT, D_MODEL), jnp.bfloat16),
            ],
        ),
        out_shape=jax.ShapeDtypeStruct((n_blocks * EXPERT_BLOCK, D_MODEL), jnp.bfloat16),
        name="moe_experts",
    )(block_expert, n_used, xs, w_eg, w_eu, w_ed)


def _combine_kernel(tile0, rows_ref, ys_ref, slots_t_ref, x1_ref, gate2_ref, gpost_ref, out_ref, buf, sem):
    j = pl.program_id(0)
    n = pl.num_programs(0)
    slot = j % 2
    tile = tile0 + j * MOE_TILES

    def copies(sl, d):
        def make(dst_piece, src_piece):
            return pltpu.make_async_copy(ys_ref.at[src_piece], buf.at[sl, d, dst_piece], sem.at[sl])
        return make

    def fetch(first_tile, sl):
        for d in range(MOE_TILES):
            _chunk_copies(first_tile + d, rows_ref, copies(sl, d), lambda c: c.start())

    @pl.when(j == 0)
    def _():
        fetch(tile, slot)

    @pl.when(j + 1 < n)
    def _():
        fetch(tile + MOE_TILES, 1 - slot)

    for d in range(MOE_TILES):
        _wait_chunk_copies(copies(slot, d))
    gain = gate2_ref[...] * gpost_ref[...]
    lane = lax.broadcasted_iota(jnp.int32, (COMBINE_SUB_ROWS, TILE_SLOTS), 1)
    per_tile = DISPATCH_TILE // COMBINE_SUB_ROWS
    for d in range(MOE_TILES):
        sorted_rows = buf[slot, d].reshape(TILE_SLOTS, D_MODEL)
        for r in range(per_tile):
            rows = slice((d * per_tile + r) * COMBINE_SUB_ROWS, (d * per_tile + r + 1) * COMBINE_SUB_ROWS)
            local = slice(r * COMBINE_SUB_ROWS, (r + 1) * COMBINE_SUB_ROWS)
            s1 = slots_t_ref[d, local, 0:1].astype(jnp.int32)
            s2 = slots_t_ref[d, local, 1:2].astype(jnp.int32)
            unperm = jnp.where(lane == s1, slots_t_ref[d, local, 2:3],
                               jnp.where(lane == s2, slots_t_ref[d, local, 3:4], 0.0)).astype(jnp.bfloat16)
            y = jnp.dot(unperm, sorted_rows, preferred_element_type=jnp.float32)
            out_ref[rows, :] = x1_ref[rows, :] + _rms(y) * gain


def _combine_call(gather_rows, ys, slots_t, x1, mod, mod_per_batch, g_post2, token_offset, bsz, seq):
    t = MOE_TILES * DISPATCH_TILE
    tile0 = token_offset // DISPATCH_TILE
    step0 = token_offset // t
    tokens = bsz * seq
    assert tokens % t == 0 and token_offset % t == 0 and (seq % t == 0 or not mod_per_batch)
    if mod_per_batch:
        per = seq // t
        gate2_spec = pl.BlockSpec((None, 1, D_MODEL), lambda j, *_: (j // per, 0, 5))
    else:
        gate2_spec = pl.BlockSpec((None, 1, D_MODEL), lambda j, *_: (0, 0, 5))
    y = pl.pallas_call(
        functools.partial(_combine_kernel, tile0),
        grid_spec=pltpu.PrefetchScalarGridSpec(
            num_scalar_prefetch=1,
            grid=(tokens // t,),
            in_specs=[
                pl.BlockSpec(memory_space=pl.ANY),
                pl.BlockSpec((MOE_TILES, DISPATCH_TILE, LANES), lambda j, *_: (step0 + j, 0, 0)),
                pl.BlockSpec((t, D_MODEL), lambda j, *_: (step0 + j, 0)),
                gate2_spec,
                pl.BlockSpec((1, D_MODEL), lambda j, *_: (0, 0)),
            ],
            out_specs=pl.BlockSpec((t, D_MODEL), lambda j, *_: (j, 0)),
            scratch_shapes=[
                pltpu.VMEM((2, MOE_TILES, N_CHUNKS, CHUNK, D_MODEL), jnp.bfloat16),
                pltpu.SemaphoreType.DMA((2,)),
            ],
        ),
        out_shape=jax.ShapeDtypeStruct((tokens, D_MODEL), jnp.float32),
        name="moe_combine",
    )(gather_rows, ys.reshape(-1, CHUNK, D_MODEL), slots_t, x1, mod, g_post2)
    return y.reshape(bsz, seq, D_MODEL)


def kernel(x_prompt, x_sample, cache_k, cache_v, c, c_ctx, w_mod, b_mod, g_pre1, g_post1, g_pre2, g_post2,
           w_in, conv_w, lambda_q1, lambda_k1, lambda_q2, lambda_k2, subln_g, w_o, w_router_group,
           w_router_expert, w_exp_gate, w_exp_up, w_exp_down):
    n_lat = c.shape[0]
    cond = jnp.concatenate(
        [c, c_ctx[None, :], jnp.zeros((MOD_ROWS - n_lat - 1, D_MODEL), jnp.float32)], axis=0)
    mod, lam = _mod_call(cond, w_mod[0], b_mod, lambda_q1, lambda_k1, lambda_q2, lambda_k2)
    mod = mod.reshape(MOD_ROWS, 1, 6 * D_MODEL)
    mod_lat, mod_ctx = mod[:n_lat], mod[n_lat:n_lat + 1]

    w_in_b = w_in[0].astype(jnp.bfloat16)
    w_o_b = w_o[0].astype(jnp.bfloat16)
    w_router = jnp.concatenate(
        [w_router_group[0], jnp.zeros((D_MODEL, 8 - N_GROUPS), jnp.float32), w_router_expert[0],
         jnp.zeros((D_MODEL, LANES - 8 - N_EXPERTS), jnp.float32)], axis=1).astype(jnp.bfloat16)

    n_prompt = x_prompt.shape[0] * x_prompt.shape[1]
    n_sample = x_sample.shape[0] * x_sample.shape[1]
    total = n_prompt + n_sample

    def mixer(x, mod_x, per_batch, rope, ctx_kv, token_offset, carried):
        kv_dtype = jnp.bfloat16 if rope else jnp.float32
        q, k, v, yc = _in_proj_call(x, mod_x, per_batch, g_pre1, w_in_b, conv_w[0], rope, kv_dtype)
        kvs = []
        if ctx_kv is not None:
            ck, cv = ctx_kv
            n_ctx = ck.shape[3]
            kvs.append((ck, cv, lambda sp, hp: pl.BlockSpec((sp, None, hp, n_ctx, D_V),
                                                            lambda b, h, i: (b, 0, h, 0, 0))))
        seq = x.shape[1]
        kvs.append((k, v, lambda sp, hp: pl.BlockSpec((sp, hp, seq, D_V), lambda b, h, i: (b, h, 0, 0))))
        short = seq <= MERGE_HEADS_MAX_SEQ
        o = _attn_call(lam, subln_g, q, kvs, SHORT_SEQS_PER_STEP if short else 1,
                       N_HEADS if short else LONG_SEQ_HEADS_PER_STEP)
        if not per_batch:
            o, yc, x = (a.reshape(1, -1, a.shape[-1]) for a in (o, yc, x))
        shared = _out_proj_call(o, yc, w_o_b, x, mod_x, per_batch, g_post1, g_pre2, w_router,
                                total, token_offset, carried)
        return shared, k, v

    shared, kp, vp = mixer(x_prompt, mod_ctx, False, False, None, 0, None)
    (x1, h2, slots, slots_t, counts), _, _ = mixer(x_sample, mod_lat, True, True, (cache_k, cache_v),
                                                   n_prompt, shared)

    n_tiles = total // DISPATCH_TILE
    max_rows = 2 * total + n_tiles * N_EXPERTS * (CHUNK - 1) + N_EXPERTS * (EXPERT_BLOCK - CHUNK)
    n_blocks = -(-max_rows // EXPERT_BLOCK)
    scatter_rows, gather_rows, tail_start, tail_chunks, block_expert, n_used = _layout_tables(counts, n_blocks)
    xs = _dispatch_call((scatter_rows, tail_start, tail_chunks), h2, slots, n_blocks)
    ys = _expert_call(block_expert, n_used, xs.reshape(-1, D_MODEL), w_exp_gate[0], w_exp_up[0], w_exp_down[0],
                      n_blocks)
    yp = _combine_call(gather_rows, ys, slots_t, x1, mod_ctx, False, g_post2, 0,
                       x_prompt.shape[0], x_prompt.shape[1])
    ysamp = _combine_call(gather_rows, ys, slots_t, x1, mod_lat, True, g_post2, n_prompt,
                          x_sample.shape[0], x_sample.shape[1])
    return yp, ysamp, kp[:, None], vp[:, None]
```

```python
import functools
import math

import numpy as np
import jax
import jax.numpy as jnp
from jax import lax
from jax.experimental import pallas as pl
from jax.experimental.pallas import tpu as pltpu

D_MODEL = 1024
GRID_W = 64
N_HEADS = 4
D_QK = 64
D_V = 128
ATTN_W = N_HEADS * D_V
CONV_W = D_MODEL - ATTN_W
IN_W = 3 * ATTN_W + 3 * CONV_W
N_GROUPS = 4
EXP_PER_GROUP = 4
N_EXPERTS = N_GROUPS * EXP_PER_GROUP
D_EXPERT = 512
ROPE_BASE = 10000.0
EPS = 1e-6
LAM_INIT = 0.8 - 0.6 * math.exp(-0.3 * 0)

LANES = 128
F32_SUBLANES = 8
BF16_SUBLANES = 16
MOD_ROWS = 16
TOKEN_TILE = 1024
OUT_SUB_ROWS = 256
OUT_TILES = 2
Q_TILE = 2048
KEY_CHUNK = 256
LONG_SEQ_HEADS_PER_STEP = 2
MERGE_HEADS_MAX_SEQ = 512
SHORT_SEQS_PER_STEP = 8
EXPERT_LOGIT_ROW0 = F32_SUBLANES
DISPATCH_TILE = 512
CHUNK = BF16_SUBLANES
TILE_SLOTS = 1280
N_CHUNKS = TILE_SLOTS // CHUNK
SLOT_GROUP = 256
GROUP_CHUNKS = SLOT_GROUP // CHUNK
MOE_TILES = 2
SPARE_SETS = 2 * MOE_TILES
COMBINE_SUB_ROWS = 256
EXPERT_BLOCK = 1024
QK_SCALE = (1.0 / math.sqrt(D_QK)) * math.log2(math.e)

NT_DIMS = (((1,), (1,)), ((), ()))


def _rms(x):
    return x * lax.rsqrt(jnp.mean(x * x, axis=-1, keepdims=True) + EPS)


def _silu(x):
    return x * (1.0 / (1.0 + jnp.exp(-x)))


def _mod_kernel(cond_ref, w_ref, b_ref, lq1_ref, lk1_ref, lq2_ref, lk2_ref, mod_ref, lam_ref):
    s = _silu(cond_ref[...])
    s_hi = s.astype(jnp.bfloat16)
    s_lo = (s - s_hi.astype(jnp.float32)).astype(jnp.bfloat16)
    w = w_ref[...]
    w_hi = w.astype(jnp.bfloat16)
    w_lo = (w - w_hi.astype(jnp.float32)).astype(jnp.bfloat16)
    both = jnp.dot(jnp.concatenate([s_hi, s_lo], axis=0), w_hi, preferred_element_type=jnp.float32)
    m = both[0:MOD_ROWS] + both[MOD_ROWS:2 * MOD_ROWS] + jnp.dot(s_hi, w_lo, preferred_element_type=jnp.float32)
    mod_ref[...] = m + b_ref[...]
    a = jnp.sum(lq1_ref[...] * lk1_ref[...], axis=-1, keepdims=True)
    b = jnp.sum(lq2_ref[...] * lk2_ref[...], axis=-1, keepdims=True)
    lam_ref[...] = jnp.broadcast_to(jnp.exp(a) - jnp.exp(b) + LAM_INIT, lam_ref.shape)


def _mod_call(cond, w_mod, b_mod, lq1, lk1, lq2, lk2):
    n_col = 6 * D_MODEL
    col_tile = 1536
    small = pl.BlockSpec((1, D_QK), lambda j: (0, 0))
    return pl.pallas_call(
        _mod_kernel,
        grid=(n_col // col_tile,),
        in_specs=[
            pl.BlockSpec((MOD_ROWS, D_MODEL), lambda j: (0, 0)),
            pl.BlockSpec((D_MODEL, col_tile), lambda j: (0, j)),
            pl.BlockSpec((1, col_tile), lambda j: (0, j)),
            small, small, small, small,
        ],
        out_specs=[
            pl.BlockSpec((MOD_ROWS, col_tile), lambda j: (0, j)),
            pl.BlockSpec((1, LANES), lambda j: (0, 0)),
        ],
        out_shape=[
            jax.ShapeDtypeStruct((MOD_ROWS, n_col), jnp.float32),
            jax.ShapeDtypeStruct((1, LANES), jnp.float32),
        ],
        name="mod",
    )(cond, w_mod, b_mod, lq1, lk1, lq2, lk2)


def _in_proj_kernel(rope, n_tiles, seq_len, x_ref, xp_ref, xn_ref, shift_ref, scale_ref, g_ref, w_ref, cw_ref,
                    *rest):
    if rope:
        cos_ref, sina_ref, sinb_ref, q_ref, k_ref, v_ref, yc_ref = rest
    else:
        q_ref, k_ref, v_ref, yc_ref = rest
    i = pl.program_id(1)
    tm = x_ref.shape[0]
    spt = q_ref.shape[0]
    per_seq = tm // spt
    gain = g_ref[...] * (1.0 + scale_ref[...])
    shift = shift_ref[...]

    def modulate(x):
        return (_rms(x) * gain + shift).astype(jnp.bfloat16)

    h = modulate(x_ref[...])
    h_halo = modulate(jnp.concatenate([xp_ref[...], xn_ref[...]], axis=0))

    def proj(lhs, lo, hi):
        return jnp.dot(lhs, w_ref[:, lo:hi], preferred_element_type=jnp.float32)

    def rot(t):
        return (t * cos_ref[...] + pltpu.roll(t, LANES - 16, axis=1) * sina_ref[...]
                + pltpu.roll(t, 16, axis=1) * sinb_ref[...])

    c0 = 3 * ATTN_W
    h_ext = jnp.concatenate([h, h_halo], axis=0)
    cu_all = proj(h_ext, c0 + CONV_W, c0 + 2 * CONV_W) * proj(h_ext, c0 + 2 * CONV_W, c0 + 3 * CONV_W)
    gb = proj(h, c0, c0 + CONV_W)
    cu = cu_all[0:tm]
    prev_row = jnp.where(i > 0, cu_all[tm + F32_SUBLANES - 1:tm + F32_SUBLANES], 0.0)
    next_row = jnp.where(i < n_tiles - 1, cu_all[tm + F32_SUBLANES:tm + F32_SUBLANES + 1], 0.0)
    row = lax.broadcasted_iota(jnp.int32, cu.shape, 0)
    pos = (i * tm + row) & (seq_len - 1)
    prev = jnp.where(row == 0, prev_row, pltpu.roll(cu, 1, axis=0))
    nxt = jnp.where(row == tm - 1, next_row, pltpu.roll(cu, tm - 1, axis=0))
    prev = jnp.where(pos == 0, 0.0, prev)
    nxt = jnp.where(pos == seq_len - 1, 0.0, nxt)
    conv = cw_ref[0:1, :] * prev + cw_ref[1:2, :] * cu + cw_ref[2:3, :] * nxt
    yc_ref[...] = (gb * conv).astype(yc_ref.dtype)

    def per_head(z, ref, finish):
        for hd in range(N_HEADS):
            t = finish(z[:, hd * D_V:(hd + 1) * D_V]).astype(ref.dtype)
            for sq in range(spt):
                ref[sq, hd] = t[sq * per_seq:(sq + 1) * per_seq]

    per_head(proj(h, 0, ATTN_W), q_ref, lambda t: (rot(t) if rope else t) * QK_SCALE)
    per_head(proj(h, ATTN_W, 2 * ATTN_W), k_ref, lambda t: rot(t) if rope else t)
    per_head(proj(h, 2 * ATTN_W, 3 * ATTN_W), v_ref, lambda t: t)


def _rope_tables(seq):
    n_rows = seq // GRID_W
    row = np.repeat(np.arange(n_rows), GRID_W).astype(np.float64)
    col = np.tile(np.arange(GRID_W), n_rows).astype(np.float64)
    nf = D_QK // 4
    inv = ROPE_BASE ** (-np.arange(nf, dtype=np.float64) / nf)
    ar = row[:, None] * inv
    ac = col[:, None] * inv
    ang = np.concatenate([ar, ar, ac, ac], axis=-1)
    ang = np.concatenate([ang, ang], axis=-1)
    first_half = (np.arange(LANES) % 32) < 16
    cos = np.cos(ang)
    sin = np.sin(ang)
    sina = np.where(first_half, -sin, 0.0)
    sinb = np.where(first_half, 0.0, sin)
    return tuple(jnp.asarray(t, dtype=jnp.float32) for t in (cos, sina, sinb))


def _in_proj_call(x, mod, mod_per_batch, g_pre1, w_in, conv_w, rope, kv_dtype):
    n_seq, seq_len, _ = x.shape
    assert seq_len & (seq_len - 1) == 0
    spt = 1 if (mod_per_batch or seq_len >= TOKEN_TILE) else TOKEN_TILE // seq_len
    x = x.reshape(n_seq // spt, spt * seq_len, D_MODEL)
    bsz, seq, _ = x.shape
    tm = min(TOKEN_TILE, seq)
    n_tiles = seq // tm
    halo = tm // F32_SUBLANES
    n_halo = seq // F32_SUBLANES

    def mod_spec(chunk):
        if mod_per_batch:
            return pl.BlockSpec((None, 1, D_MODEL), lambda b, i: (b, 0, chunk))
        return pl.BlockSpec((None, 1, D_MODEL), lambda b, i: (0, 0, chunk))

    in_specs = [
        pl.BlockSpec((None, tm, D_MODEL), lambda b, i: (b, i, 0)),
        pl.BlockSpec((None, F32_SUBLANES, D_MODEL), lambda b, i: (b, jnp.maximum(i * halo - 1, 0), 0)),
        pl.BlockSpec((None, F32_SUBLANES, D_MODEL), lambda b, i: (b, jnp.minimum((i + 1) * halo, n_halo - 1), 0)),
        mod_spec(0), mod_spec(1),
        pl.BlockSpec((1, D_MODEL), lambda b, i: (0, 0)),
        pl.BlockSpec((D_MODEL, IN_W), lambda b, i: (0, 0)),
        pl.BlockSpec((3, CONV_W), lambda b, i: (0, 0)),
    ]
    args = [x, x, x, mod, mod, g_pre1, w_in, conv_w]
    if rope:
        in_specs += [pl.BlockSpec((tm, LANES), lambda b, i: (i, 0))] * 3
        args += list(_rope_tables(seq))
    head_spec = pl.BlockSpec((spt, N_HEADS, tm // spt, D_V), lambda b, i: (b, 0, i, 0))
    head_shape = lambda dt: jax.ShapeDtypeStruct((n_seq, N_HEADS, seq_len, D_V), dt)
    q, k, v, yc = pl.pallas_call(
        functools.partial(_in_proj_kernel, rope, n_tiles, seq_len),
        grid=(bsz, n_tiles),
        in_specs=in_specs,
        out_specs=[head_spec, head_spec, head_spec, pl.BlockSpec((None, tm, CONV_W), lambda b, i: (b, i, 0))],
        out_shape=[head_shape(jnp.bfloat16), head_shape(kv_dtype), head_shape(kv_dtype),
                   jax.ShapeDtypeStruct((bsz, seq, CONV_W), jnp.bfloat16)],
        name="in_proj_rope" if rope else "in_proj",
    )(*args)
    return q, k, v, yc.reshape(n_seq, seq_len, CONV_W)


def _attn_kernel(n_kv, lam_ref, g_ref, q_ref, *rest):
    kv_refs = rest[:2 * n_kv]
    o_ref = rest[2 * n_kv]
    seqs, heads, tq, _ = q_ref.shape
    for sq in range(seqs):
        _attend(sq, heads, tq, lam_ref, g_ref, q_ref, kv_refs, o_ref)


def _attend(sq, heads, tq, lam_ref, g_ref, q_ref, kv_refs, o_ref):
    n_kv = len(kv_refs) // 2
    lane = lax.broadcasted_iota(jnp.int32, (tq, 2 * D_QK), 1)
    for hh in range(heads):
        q = q_ref[sq, hh]
        zero = jnp.zeros_like(q)
        halves = (jnp.where(lane < D_QK, q, zero), jnp.where(lane >= D_QK, q, zero))
        acc = [jnp.zeros((tq, 2 * D_V), jnp.float32) for _ in range(2)]
        m = [jnp.full((tq, 1), -1e30, jnp.float32) for _ in range(2)]
        for j in range(n_kv):
            k_ref, v_ref = kv_refs[2 * j], kv_refs[2 * j + 1]
            n_keys = k_ref.shape[2]
            ck = min(KEY_CHUNK, n_keys)
            ones = jnp.ones((ck, D_V), jnp.bfloat16)
            for c in range(n_keys // ck):
                k = k_ref[sq, hh, c * ck:(c + 1) * ck, :].astype(jnp.bfloat16)
                v1 = jnp.concatenate([v_ref[sq, hh, c * ck:(c + 1) * ck, :].astype(jnp.bfloat16), ones], axis=1)
                for x, qh in enumerate(halves):
                    s = lax.dot_general(qh, k, NT_DIMS, preferred_element_type=jnp.float32)
                    m_new = jnp.maximum(m[x], jnp.max(s, axis=-1, keepdims=True))
                    p = jnp.exp2(s - m_new).astype(jnp.bfloat16)
                    acc[x] = (jnp.exp2(m[x] - m_new) * acc[x]
                              + jnp.dot(p, v1, preferred_element_type=jnp.float32))
                    m[x] = m_new
        o = (acc[0][:, 0:D_V] / acc[0][:, D_V:2 * D_V]
             - lam_ref[0:1, 0:1] * (acc[1][:, 0:D_V] / acc[1][:, D_V:2 * D_V]))
        o_ref[sq, :, hh * D_V:(hh + 1) * D_V] = (_rms(o) * (g_ref[...] * (1.0 - LAM_INIT))).astype(o_ref.dtype)


def _attn_call(lam, subln_g, q, kvs, seqs_per_step, heads_per_step):
    bsz, _, seq, _ = q.shape
    tq = min(Q_TILE, seq)
    sp, hp = seqs_per_step, heads_per_step
    in_specs = [
        pl.BlockSpec((1, LANES), lambda b, h, i: (0, 0)),
        pl.BlockSpec((1, D_V), lambda b, h, i: (0, 0)),
        pl.BlockSpec((sp, hp, tq, D_V), lambda b, h, i: (b, h, i, 0)),
    ]
    args = [lam, subln_g, q]
    for k, v, spec_fn in kvs:
        in_specs += [spec_fn(sp, hp), spec_fn(sp, hp)]
        args += [k, v]
    return pl.pallas_call(
        functools.partial(_attn_kernel, len(kvs)),
        grid=(bsz // sp, N_HEADS // hp, seq // tq),
        in_specs=in_specs,
        out_specs=pl.BlockSpec((sp, tq, hp * D_V), lambda b, h, i: (b, i, h)),
        out_shape=jax.ShapeDtypeStruct((bsz, seq, ATTN_W), jnp.bfloat16),
        name="attn%d" % len(kvs),
    )(*args)


def _route(logits_t):
    lg = [logits_t[g:g + 1, :] for g in range(N_GROUPS)]
    mg = functools.reduce(jnp.maximum, lg)
    p_sel = 1.0 / functools.reduce(jnp.add, [jnp.exp(t - mg) for t in lg])
    g_sel = jnp.full(mg.shape, N_GROUPS - 1, jnp.int32)
    for g in range(N_GROUPS - 2, -1, -1):
        g_sel = jnp.where(lg[g] == mg, g, g_sel)
    le = []
    for j in range(EXP_PER_GROUP):
        t = jnp.zeros_like(mg)
        for g in range(N_GROUPS):
            r = EXPERT_LOGIT_ROW0 + g * EXP_PER_GROUP + j
            t = jnp.where(g_sel == g, logits_t[r:r + 1, :], t)
        le.append(t)
    m1 = functools.reduce(jnp.maximum, le)
    i1 = jnp.full(mg.shape, EXP_PER_GROUP - 1, jnp.int32)
    for j in range(EXP_PER_GROUP - 2, -1, -1):
        i1 = jnp.where(le[j] == m1, j, i1)
    neg = jnp.float32(-jnp.inf)
    rest = [jnp.where(i1 == j, neg, le[j]) for j in range(EXP_PER_GROUP)]
    m2 = functools.reduce(jnp.maximum, rest)
    i2 = jnp.full(mg.shape, EXP_PER_GROUP - 1, jnp.int32)
    for j in range(EXP_PER_GROUP - 2, -1, -1):
        i2 = jnp.where(rest[j] == m2, j, i2)
    e2 = jnp.exp(m2 - m1)
    w1 = p_sel / (1.0 + e2)
    w2 = p_sel * e2 / (1.0 + e2)
    base = g_sel * EXP_PER_GROUP
    return (base + i1).astype(jnp.float32), (base + i2).astype(jnp.float32), w1, w2


def _out_proj_kernel(o_ref, yc_ref, wo_ref, x_ref, gate1_ref, shift2_ref, scale2_ref, gpost_ref, gpre_ref,
                     wr_ref, *rest):
    x1_ref, h2_ref, slots_ref, slots_t_ref, counts_ref = rest[-5:]
    tm = o_ref.shape[0]
    sub = min(OUT_SUB_ROWS, tm)
    gain1 = gate1_ref[...] * gpost_ref[...]
    gain2 = gpre_ref[...] * (1.0 + scale2_ref[...])
    routes = []
    for r in range(tm // sub):
        rows = slice(r * sub, (r + 1) * sub)
        out = (jnp.dot(o_ref[rows, :], wo_ref[0:ATTN_W, :], preferred_element_type=jnp.float32)
               + jnp.dot(yc_ref[rows, :], wo_ref[ATTN_W:D_MODEL, :], preferred_element_type=jnp.float32))
        x1 = x_ref[rows, :] + _rms(out) * gain1
        x1_ref[rows, :] = x1
        h2 = (_rms(x1) * gain2 + shift2_ref[...]).astype(jnp.bfloat16)
        h2_ref[rows, :] = h2
        logits = jnp.dot(h2, wr_ref[...], preferred_element_type=jnp.float32)
        routes.append(_route(logits.T))
    per_tile = DISPATCH_TILE // sub
    for d in range(tm // DISPATCH_TILE):
        e1, e2, w1, w2 = (jnp.concatenate(parts, axis=1)
                          for parts in zip(*routes[d * per_tile:(d + 1) * per_tile]))
        _plan(e1, e2, w1, w2, slots_ref.at[d], slots_t_ref.at[d], counts_ref.at[d])


def _out_proj_call(o, yc, w_o, x, mod, mod_per_batch, g_post1, g_pre2, w_router,
                   total_tokens, token_offset, carried):
    bsz, seq, _ = x.shape
    tm = OUT_TILES * DISPATCH_TILE
    assert seq % tm == 0 and token_offset % tm == 0 and total_tokens % tm == 0
    n_tiles = seq // tm
    tile0 = token_offset // tm
    n_all = total_tokens // DISPATCH_TILE

    def mod_spec(chunk):
        if mod_per_batch:
            return pl.BlockSpec((None, 1, D_MODEL), lambda b, i: (b, 0, chunk))
        return pl.BlockSpec((None, 1, D_MODEL), lambda b, i: (0, 0, chunk))

    def vec_spec():
        return pl.BlockSpec((1, D_MODEL), lambda b, i: (0, 0))

    tok = lambda w: pl.BlockSpec((None, tm, w), lambda b, i: (b, i, 0))
    flat = lambda w: pl.BlockSpec((tm, w), lambda b, i: (tile0 + b * n_tiles + i, 0))
    per_tile = lambda r, c: pl.BlockSpec((OUT_TILES, r, c), lambda b, i: (tile0 + b * n_tiles + i, 0, 0))
    in_specs = [
        tok(ATTN_W), tok(CONV_W),
        pl.BlockSpec((D_MODEL, D_MODEL), lambda b, i: (0, 0)),
        tok(D_MODEL),
        mod_spec(2), mod_spec(3), mod_spec(4),
        vec_spec(), vec_spec(),
        pl.BlockSpec((D_MODEL, LANES), lambda b, i: (0, 0)),
    ]
    args = [o, yc, w_o, x, mod, mod, mod, g_post1, g_pre2, w_router]
    aliases = {}
    if carried is not None:
        for j, arr in enumerate(carried):
            aliases[len(args)] = j
            in_specs.append(pl.BlockSpec(memory_space=pl.ANY))
            args.append(arr)
    return pl.pallas_call(
        _out_proj_kernel,
        grid=(bsz, n_tiles),
        in_specs=in_specs,
        out_specs=[flat(D_MODEL), flat(D_MODEL), per_tile(8, DISPATCH_TILE), per_tile(DISPATCH_TILE, LANES),
                   per_tile(N_EXPERTS, LANES)],
        out_shape=[
            jax.ShapeDtypeStruct((total_tokens, D_MODEL), jnp.float32),
            jax.ShapeDtypeStruct((total_tokens, D_MODEL), jnp.bfloat16),
            jax.ShapeDtypeStruct((n_all, 8, DISPATCH_TILE), jnp.float32),
            jax.ShapeDtypeStruct((n_all, DISPATCH_TILE, LANES), jnp.float32),
            jax.ShapeDtypeStruct((n_all, N_EXPERTS, LANES), jnp.float32),
        ],
        input_output_aliases=aliases,
        name="out_proj",
    )(*args)


def _plan(e1, e2, w1, w2, slots_ref, slots_t_ref, counts_ref):
    t = e1.shape[1]
    e1 = e1.astype(jnp.int32)
    e2 = e2.astype(jnp.int32)
    eid = lax.broadcasted_iota(jnp.int32, (N_EXPERTS, t), 0)
    hot1 = jnp.where(eid == e1, 1.0, 0.0)
    hot2 = jnp.where(eid == e2, 1.0, 0.0)
    hot = jnp.concatenate([hot1, hot2], axis=0).astype(jnp.bfloat16)
    before = (lax.broadcasted_iota(jnp.int32, (t, t), 0) < lax.broadcasted_iota(jnp.int32, (t, t), 1))
    before = jnp.where(before, 1.0, 0.0).astype(jnp.bfloat16)
    rank = jnp.dot(hot, before, preferred_element_type=jnp.float32)
    n1 = jnp.sum(hot1, axis=1, keepdims=True)
    n2 = jnp.sum(hot2, axis=1, keepdims=True)
    ones = jnp.ones((8, t), jnp.bfloat16)
    cnt_row = lax.dot_general(ones, (hot1 + hot2).astype(jnp.bfloat16), NT_DIMS,
                              preferred_element_type=jnp.float32)[0:1, :]
    pad_row = jnp.floor((cnt_row + (CHUNK - 1)) * (1.0 / CHUNK)) * CHUNK
    lower = (lax.broadcasted_iota(jnp.int32, (N_EXPERTS, N_EXPERTS), 1)
             < lax.broadcasted_iota(jnp.int32, (N_EXPERTS, N_EXPERTS), 0))
    base = jnp.sum(jnp.where(lower, pad_row, 0.0), axis=1, keepdims=True)
    slot1 = jnp.sum(hot1 * (base + rank[:N_EXPERTS]), axis=0, keepdims=True)
    slot2 = jnp.sum(hot2 * (base + n1 + rank[N_EXPERTS:]), axis=0, keepdims=True)
    slots_ref[...] = jnp.concatenate([slot1, slot2, jnp.zeros((6, t), jnp.float32)], axis=0)
    wide = jnp.concatenate([slot1, slot2, w1, w2, jnp.zeros((LANES - 4, t), jnp.float32)], axis=0)
    slots_t_ref[...] = wide.T
    counts_ref[...] = jnp.broadcast_to(n1 + n2, counts_ref.shape)


def _layout_tables(counts, n_blocks):
    cnt = counts[:, :, 0].astype(jnp.int32)
    n_tiles = cnt.shape[0]
    nch = (cnt + (CHUNK - 1)) // CHUNK
    csum = jnp.cumsum(nch, axis=0)
    rows = CHUNK * csum[-1]
    blocks = (rows + (EXPERT_BLOCK - 1)) // EXPERT_BLOCK
    blk_end = jnp.cumsum(blocks)
    blk_start = blk_end - blocks
    off = blk_start[None, :] * EXPERT_BLOCK + CHUNK * (csum - nch)
    first = jnp.cumsum(nch, axis=1) - nch
    used = jnp.sum(nch, axis=1)
    c = jnp.arange(N_CHUNKS, dtype=jnp.int32)
    e_idx = jnp.sum((c[None, :, None] >= first[:, None, :]).astype(jnp.int32), axis=2) - 1
    hot = e_idx[:, :, None] == jnp.arange(N_EXPERTS, dtype=jnp.int32)[None, None, :]
    pick = lambda a: jnp.sum(jnp.where(hot, a[:, None, :], 0), axis=2)
    row = pick(off) + CHUNK * (c[None, :] - pick(first))
    valid = c[None, :] < used[:, None]
    spare = (n_blocks * EXPERT_BLOCK + (jnp.arange(n_tiles, dtype=jnp.int32) % SPARE_SETS)[:, None] * TILE_SLOTS
             + CHUNK * c[None, :])
    scatter_rows = jnp.where(valid, row, spare)
    gather_rows = jnp.where(valid, row, 0)
    n_used = blk_end[-1]
    bc = jnp.minimum(jnp.arange(n_blocks, dtype=jnp.int32), n_used - 1)
    block_expert = jnp.sum((bc[:, None] >= blk_end[None, :]).astype(jnp.int32), axis=1)
    tail_start = blk_start * EXPERT_BLOCK + rows
    tail_chunks = (blocks * EXPERT_BLOCK - rows) // CHUNK
    flat = lambda a: a.reshape(-1).astype(jnp.int32)
    return (flat(scatter_rows // CHUNK), flat(gather_rows // CHUNK), flat(tail_start // CHUNK), flat(tail_chunks),
            flat(block_expert), flat(n_used))


def _chunk_copies(tile, chunk_ref, make_copy, action):
    for c in range(N_CHUNKS):
        action(make_copy(c, chunk_ref[tile * N_CHUNKS + c]))


def _wait_chunk_copies(make_copy):
    for _ in range(N_CHUNKS):
        make_copy(0, 0).wait()


def _dispatch_kernel(rows_ref, tail_start_ref, tail_chunks_ref, h_ref, slots_ref, xs_ref, buf, zbuf, sem, zsem):
    i = pl.program_id(0)
    n = pl.num_programs(0)
    slot = i % 2
    t = DISPATCH_TILE

    def copies(sl, d):
        def make(src_piece, dst_piece):
            return pltpu.make_async_copy(buf.at[sl, d, src_piece], xs_ref.at[dst_piece], sem.at[sl])
        return make

    for d in range(MOE_TILES):
        s = slots_ref[d]
        s1 = s[0:1, :].astype(jnp.int32)
        s2 = s[1:2, :].astype(jnp.int32)
        h = h_ref[d * t:(d + 1) * t, :]
        for g in range(TILE_SLOTS // SLOT_GROUP):
            rid = lax.broadcasted_iota(jnp.int32, (SLOT_GROUP, t), 0) + g * SLOT_GROUP
            perm = jnp.where(rid == s1, 1.0, jnp.where(rid == s2, 1.0, 0.0)).astype(jnp.bfloat16)
            rows = jnp.dot(perm, h, preferred_element_type=jnp.float32)
            pieces = slice(g * GROUP_CHUNKS, (g + 1) * GROUP_CHUNKS)
            buf[slot, d, pieces] = rows.astype(jnp.bfloat16).reshape(GROUP_CHUNKS, CHUNK, D_MODEL)
        _chunk_copies(i * MOE_TILES + d, rows_ref, copies(slot, d), lambda c: c.start())

    @pl.when(i > 0)
    def _():
        for d in range(MOE_TILES):
            _wait_chunk_copies(copies(1 - slot, d))

    @pl.when(i == n - 1)
    def _():
        zbuf[...] = jnp.zeros_like(zbuf)

        def tails(action):
            for e in range(N_EXPERTS):
                start = tail_start_ref[e]

                def body(m, carry, start=start):
                    action(pltpu.make_async_copy(zbuf, xs_ref.at[start + m], zsem))
                    return carry

                lax.fori_loop(0, tail_chunks_ref[e], body, 0)

        tails(lambda c: c.start())
        tails(lambda c: c.wait())
        for d in range(MOE_TILES):
            _wait_chunk_copies(copies(slot, d))


def _dispatch_call(tables, h2, slots, n_blocks):
    scatter_rows, tail_start, tail_chunks = tables
    tokens = h2.shape[0]
    t = MOE_TILES * DISPATCH_TILE
    return pl.pallas_call(
        _dispatch_kernel,
        grid_spec=pltpu.PrefetchScalarGridSpec(
            num_scalar_prefetch=3,
            grid=(tokens // t,),
            in_specs=[
                pl.BlockSpec((t, D_MODEL), lambda i, *_: (i, 0)),
                pl.BlockSpec((MOE_TILES, 8, DISPATCH_TILE), lambda i, *_: (i, 0, 0)),
            ],
            out_specs=pl.BlockSpec(memory_space=pl.ANY),
            scratch_shapes=[
                pltpu.VMEM((2, MOE_TILES, N_CHUNKS, CHUNK, D_MODEL), jnp.bfloat16),
                pltpu.VMEM((CHUNK, D_MODEL), jnp.bfloat16),
                pltpu.SemaphoreType.DMA((2,)),
                pltpu.SemaphoreType.DMA(()),
            ],
        ),
        out_shape=jax.ShapeDtypeStruct(((n_blocks * EXPERT_BLOCK + SPARE_SETS * TILE_SLOTS) // CHUNK, CHUNK, D_MODEL),
                                       jnp.bfloat16),
        name="moe_dispatch",
    )(scatter_rows, tail_start, tail_chunks, h2, slots)


def _expert_kernel(bexp_ref, nused_ref, xs_ref, wg_ref, wu_ref, wd_ref, ys_ref, wgu_b, wd_b):
    b = pl.program_id(0)
    e = bexp_ref[b]
    prev = bexp_ref[jnp.maximum(b - 1, 0)]

    @pl.when((b == 0) | (e != prev))
    def _():
        wgu_b[:, 0:D_EXPERT] = wg_ref[...].astype(jnp.bfloat16)
        wgu_b[:, D_EXPERT:2 * D_EXPERT] = wu_ref[...].astype(jnp.bfloat16)
        wd_b[...] = wd_ref[...].astype(jnp.bfloat16)

    @pl.when(b < nused_ref[0])
    def _():
        x = xs_ref[...]
        y = None
        half = D_EXPERT // 2
        for c in range(2):
            cols = slice(c * half, (c + 1) * half)
            g = jnp.dot(x, wgu_b[:, c * half:(c + 1) * half], preferred_element_type=jnp.float32)
            u = jnp.dot(x, wgu_b[:, D_EXPERT + c * half:D_EXPERT + (c + 1) * half],
                        preferred_element_type=jnp.float32)
            act = (_silu(g) * u).astype(jnp.bfloat16)
            t = jnp.dot(act, wd_b[cols, :], preferred_element_type=jnp.float32)
            y = t if y is None else y + t
        ys_ref[...] = y.astype(jnp.bfloat16)


def _expert_call(block_expert, n_used, xs, w_eg, w_eu, w_ed, n_blocks):
    row_blk = lambda w: pl.BlockSpec((EXPERT_BLOCK, w), lambda b, be, nu: (jnp.minimum(b, nu[0] - 1), 0))
    wt_blk = lambda r, c: pl.BlockSpec((None, r, c), lambda b, be, nu: (be[b], 0, 0))
    return pl.pallas_call(
        _expert_kernel,
        grid_spec=pltpu.PrefetchScalarGridSpec(
            num_scalar_prefetch=2,
            grid=(n_blocks,),
            in_specs=[
                row_blk(D_MODEL),
                wt_blk(D_MODEL, D_EXPERT), wt_blk(D_MODEL, D_EXPERT), wt_blk(D_EXPERT, D_MODEL),
            ],
            out_specs=row_blk(D_MODEL),
            scratch_shapes=[
                pltpu.VMEM((D_MODEL, 2 * D_EXPERT), jnp.bfloat16),
                pltpu.VMEM((D_EXPERT, D_MODEL), jnp.bfloat16),
            ],
        ),
        out_shape=jax.ShapeDtypeStruct((n_blocks * EXPERT_BLOCK, D_MODEL), jnp.bfloat16),
        name="moe_experts",
    )(block_expert, n_used, xs, w_eg, w_eu, w_ed)


def _combine_kernel(tile0, rows_ref, ys_ref, slots_t_ref, x1_ref, gate2_ref, gpost_ref, out_ref, buf, sem):
    j = pl.program_id(0)
    n = pl.num_programs(0)
    slot = j % 2
    tile = tile0 + j * MOE_TILES

    def copies(sl, d):
        def make(dst_piece, src_piece):
            return pltpu.make_async_copy(ys_ref.at[src_piece], buf.at[sl, d, dst_piece], sem.at[sl])
        return make

    def fetch(first_tile, sl):
        for d in range(MOE_TILES):
            _chunk_copies(first_tile + d, rows_ref, copies(sl, d), lambda c: c.start())

    @pl.when(j == 0)
    def _():
        fetch(tile, slot)

    @pl.when(j + 1 < n)
    def _():
        fetch(tile + MOE_TILES, 1 - slot)

    for d in range(MOE_TILES):
        _wait_chunk_copies(copies(slot, d))
    gain = gate2_ref[...] * gpost_ref[...]
    lane = lax.broadcasted_iota(jnp.int32, (COMBINE_SUB_ROWS, TILE_SLOTS), 1)
    per_tile = DISPATCH_TILE // COMBINE_SUB_ROWS
    for d in range(MOE_TILES):
        sorted_rows = buf[slot, d].reshape(TILE_SLOTS, D_MODEL)
        for r in range(per_tile):
            rows = slice((d * per_tile + r) * COMBINE_SUB_ROWS, (d * per_tile + r + 1) * COMBINE_SUB_ROWS)
            local = slice(r * COMBINE_SUB_ROWS, (r + 1) * COMBINE_SUB_ROWS)
            s1 = slots_t_ref[d, local, 0:1].astype(jnp.int32)
            s2 = slots_t_ref[d, local, 1:2].astype(jnp.int32)
            unperm = jnp.where(lane == s1, slots_t_ref[d, local, 2:3],
                               jnp.where(lane == s2, slots_t_ref[d, local, 3:4], 0.0)).astype(jnp.bfloat16)
            y = jnp.dot(unperm, sorted_rows, preferred_element_type=jnp.float32)
            out_ref[rows, :] = x1_ref[rows, :] + _rms(y) * gain


def _combine_call(gather_rows, ys, slots_t, x1, mod, mod_per_batch, g_post2, token_offset, bsz, seq):
    t = MOE_TILES * DISPATCH_TILE
    tile0 = token_offset // DISPATCH_TILE
    step0 = token_offset // t
    tokens = bsz * seq
    assert tokens % t == 0 and token_offset % t == 0 and (seq % t == 0 or not mod_per_batch)
    if mod_per_batch:
        per = seq // t
        gate2_spec = pl.BlockSpec((None, 1, D_MODEL), lambda j, *_: (j // per, 0, 5))
    else:
        gate2_spec = pl.BlockSpec((None, 1, D_MODEL), lambda j, *_: (0, 0, 5))
    y = pl.pallas_call(
        functools.partial(_combine_kernel, tile0),
        grid_spec=pltpu.PrefetchScalarGridSpec(
            num_scalar_prefetch=1,
            grid=(tokens // t,),
            in_specs=[
                pl.BlockSpec(memory_space=pl.ANY),
                pl.BlockSpec((MOE_TILES, DISPATCH_TILE, LANES), lambda j, *_: (step0 + j, 0, 0)),
                pl.BlockSpec((t, D_MODEL), lambda j, *_: (step0 + j, 0)),
                gate2_spec,
                pl.BlockSpec((1, D_MODEL), lambda j, *_: (0, 0)),
            ],
            out_specs=pl.BlockSpec((t, D_MODEL), lambda j, *_: (j, 0)),
            scratch_shapes=[
                pltpu.VMEM((2, MOE_TILES, N_CHUNKS, CHUNK, D_MODEL), jnp.bfloat16),
                pltpu.SemaphoreType.DMA((2,)),
            ],
        ),
        out_shape=jax.ShapeDtypeStruct((tokens, D_MODEL), jnp.float32),
        name="moe_combine",
    )(gather_rows, ys.reshape(-1, CHUNK, D_MODEL), slots_t, x1, mod, g_post2)
    return y.reshape(bsz, seq, D_MODEL)


def kernel(x_prompt, x_sample, cache_k, cache_v, c, c_ctx, w_mod, b_mod, g_pre1, g_post1, g_pre2, g_post2,
           w_in, conv_w, lambda_q1, lambda_k1, lambda_q2, lambda_k2, subln_g, w_o, w_router_group,
           w_router_expert, w_exp_gate, w_exp_up, w_exp_down):
    n_lat = c.shape[0]
    cond = jnp.concatenate(
        [c, c_ctx[None, :], jnp.zeros((MOD_ROWS - n_lat - 1, D_MODEL), jnp.float32)], axis=0)
    mod, lam = _mod_call(cond, w_mod[0], b_mod, lambda_q1, lambda_k1, lambda_q2, lambda_k2)
    mod = mod.reshape(MOD_ROWS, 1, 6 * D_MODEL)
    mod_lat, mod_ctx = mod[:n_lat], mod[n_lat:n_lat + 1]

    w_in_b = w_in[0].astype(jnp.bfloat16)
    w_o_b = w_o[0].astype(jnp.bfloat16)
    w_router = jnp.concatenate(
        [w_router_group[0], jnp.zeros((D_MODEL, EXPERT_LOGIT_ROW0 - N_GROUPS), jnp.float32), w_router_expert[0],
         jnp.zeros((D_MODEL, LANES - EXPERT_LOGIT_ROW0 - N_EXPERTS), jnp.float32)], axis=1).astype(jnp.bfloat16)

    n_prompt = x_prompt.shape[0] * x_prompt.shape[1]
    n_sample = x_sample.shape[0] * x_sample.shape[1]
    total = n_prompt + n_sample

    def mixer(x, mod_x, per_batch, rope, ctx_kv, token_offset, carried):
        kv_dtype = jnp.bfloat16 if rope else jnp.float32
        q, k, v, yc = _in_proj_call(x, mod_x, per_batch, g_pre1, w_in_b, conv_w[0], rope, kv_dtype)
        kvs = []
        if ctx_kv is not None:
            ck, cv = ctx_kv
            n_ctx = ck.shape[3]
            kvs.append((ck, cv, lambda sp, hp: pl.BlockSpec((sp, None, hp, n_ctx, D_V),
                                                            lambda b, h, i: (b, 0, h, 0, 0))))
        seq = x.shape[1]
        kvs.append((k, v, lambda sp, hp: pl.BlockSpec((sp, hp, seq, D_V), lambda b, h, i: (b, h, 0, 0))))
        short = seq <= MERGE_HEADS_MAX_SEQ
        o = _attn_call(lam, subln_g, q, kvs, SHORT_SEQS_PER_STEP if short else 1,
                       N_HEADS if short else LONG_SEQ_HEADS_PER_STEP)
        if not per_batch:
            o, yc, x = (a.reshape(1, -1, a.shape[-1]) for a in (o, yc, x))
        shared = _out_proj_call(o, yc, w_o_b, x, mod_x, per_batch, g_post1, g_pre2, w_router,
                                total, token_offset, carried)
        return shared, k, v

    shared, kp, vp = mixer(x_prompt, mod_ctx, False, False, None, 0, None)
    (x1, h2, slots, slots_t, counts), _, _ = mixer(x_sample, mod_lat, True, True, (cache_k, cache_v),
                                                   n_prompt, shared)

    n_tiles = total // DISPATCH_TILE
    max_rows = 2 * total + n_tiles * N_EXPERTS * (CHUNK - 1) + N_EXPERTS * (EXPERT_BLOCK - CHUNK)
    n_blocks = -(-max_rows // EXPERT_BLOCK)
    scatter_rows, gather_rows, tail_start, tail_chunks, block_expert, n_used = _layout_tables(counts, n_blocks)
    xs = _dispatch_call((scatter_rows, tail_start, tail_chunks), h2, slots, n_blocks)
    ys = _expert_call(block_expert, n_used, xs.reshape(-1, D_MODEL), w_exp_gate[0], w_exp_up[0], w_exp_down[0],
                      n_blocks)
    yp = _combine_call(gather_rows, ys, slots_t, x1, mod_ctx, False, g_post2, 0,
                       x_prompt.shape[0], x_prompt.shape[1])
    ysamp = _combine_call(gather_rows, ys, slots_t, x1, mod_lat, True, g_post2, n_prompt,
                          x_sample.shape[0], x_sample.shape[1])
    return yp, ysamp, kp[:, None], vp[:, None]
```

```python
import functools
import math

import numpy as np
import jax
import jax.numpy as jnp
from jax import lax
from jax.experimental import pallas as pl
from jax.experimental.pallas import tpu as pltpu

D_MODEL = 1024
GRID_W = 64
N_HEADS = 4
D_QK = 64
D_V = 128
ATTN_W = N_HEADS * D_V
CONV_W = D_MODEL - ATTN_W
IN_W = 3 * ATTN_W + 3 * CONV_W
N_GROUPS = 4
EXP_PER_GROUP = 4
N_EXPERTS = N_GROUPS * EXP_PER_GROUP
D_EXPERT = 512
ROPE_BASE = 10000.0
EPS = 1e-6
LAM_INIT = 0.8 - 0.6 * math.exp(-0.3 * 0)

LANES = 128
F32_SUBLANES = 8
BF16_SUBLANES = 16
MOD_ROWS = 16
TOKEN_TILE = 1024
OUT_SUB_ROWS = 512
OUT_TILES = 2
Q_TILE = 2048
KEY_CHUNK = 256
LONG_SEQ_HEADS_PER_STEP = 2
MERGE_HEADS_MAX_SEQ = 512
SHORT_SEQS_PER_STEP = 4
EXPERT_LOGIT_ROW0 = F32_SUBLANES
DISPATCH_TILE = 512
CHUNK = BF16_SUBLANES
TILE_SLOTS = 1280
N_CHUNKS = TILE_SLOTS // CHUNK
SLOT_GROUP = 256
GROUP_CHUNKS = SLOT_GROUP // CHUNK
MOE_TILES = 2
SPARE_SETS = 2 * MOE_TILES
COMBINE_SUB_ROWS = 256
EXPERT_BLOCK = 1024
QK_SCALE = (1.0 / math.sqrt(D_QK)) * math.log2(math.e)

NT_DIMS = (((1,), (1,)), ((), ()))


def _rms(x):
    return x * lax.rsqrt(jnp.mean(x * x, axis=-1, keepdims=True) + EPS)


def _silu(x):
    return x * (1.0 / (1.0 + jnp.exp(-x)))


def _mod_kernel(cond_ref, w_ref, b_ref, lq1_ref, lk1_ref, lq2_ref, lk2_ref, mod_ref, lam_ref):
    s = _silu(cond_ref[...])
    s_hi = s.astype(jnp.bfloat16)
    s_lo = (s - s_hi.astype(jnp.float32)).astype(jnp.bfloat16)
    w = w_ref[...]
    w_hi = w.astype(jnp.bfloat16)
    w_lo = (w - w_hi.astype(jnp.float32)).astype(jnp.bfloat16)
    both = jnp.dot(jnp.concatenate([s_hi, s_lo], axis=0), w_hi, preferred_element_type=jnp.float32)
    m = both[0:MOD_ROWS] + both[MOD_ROWS:2 * MOD_ROWS] + jnp.dot(s_hi, w_lo, preferred_element_type=jnp.float32)
    mod_ref[...] = m + b_ref[...]
    a = jnp.sum(lq1_ref[...] * lk1_ref[...], axis=-1, keepdims=True)
    b = jnp.sum(lq2_ref[...] * lk2_ref[...], axis=-1, keepdims=True)
    lam_ref[...] = jnp.broadcast_to(jnp.exp(a) - jnp.exp(b) + LAM_INIT, lam_ref.shape)


def _mod_call(cond, w_mod, b_mod, lq1, lk1, lq2, lk2):
    n_col = 6 * D_MODEL
    col_tile = 1536
    small = pl.BlockSpec((1, D_QK), lambda j: (0, 0))
    return pl.pallas_call(
        _mod_kernel,
        grid=(n_col // col_tile,),
        in_specs=[
            pl.BlockSpec((MOD_ROWS, D_MODEL), lambda j: (0, 0)),
            pl.BlockSpec((D_MODEL, col_tile), lambda j: (0, j)),
            pl.BlockSpec((1, col_tile), lambda j: (0, j)),
            small, small, small, small,
        ],
        out_specs=[
            pl.BlockSpec((MOD_ROWS, col_tile), lambda j: (0, j)),
            pl.BlockSpec((1, LANES), lambda j: (0, 0)),
        ],
        out_shape=[
            jax.ShapeDtypeStruct((MOD_ROWS, n_col), jnp.float32),
            jax.ShapeDtypeStruct((1, LANES), jnp.float32),
        ],
        name="mod",
    )(cond, w_mod, b_mod, lq1, lk1, lq2, lk2)


def _in_proj_kernel(rope, n_tiles, seq_len, x_ref, xp_ref, xn_ref, shift_ref, scale_ref, g_ref, w_ref, cw_ref,
                    *rest):
    if rope:
        cos_ref, sina_ref, sinb_ref, q_ref, k_ref, v_ref, yc_ref = rest
    else:
        q_ref, k_ref, v_ref, yc_ref = rest
    i = pl.program_id(1)
    tm = x_ref.shape[0]
    spt = q_ref.shape[0]
    per_seq = tm // spt
    gain = g_ref[...] * (1.0 + scale_ref[...])
    shift = shift_ref[...]

    def modulate(x):
        return (_rms(x) * gain + shift).astype(jnp.bfloat16)

    h = modulate(x_ref[...])
    h_halo = modulate(jnp.concatenate([xp_ref[...], xn_ref[...]], axis=0))

    def proj(lhs, lo, hi):
        return jnp.dot(lhs, w_ref[:, lo:hi], preferred_element_type=jnp.float32)

    def rot(t):
        return (t * cos_ref[...] + pltpu.roll(t, LANES - 16, axis=1) * sina_ref[...]
                + pltpu.roll(t, 16, axis=1) * sinb_ref[...])

    c0 = 3 * ATTN_W
    h_ext = jnp.concatenate([h, h_halo], axis=0)
    cu_all = proj(h_ext, c0 + CONV_W, c0 + 2 * CONV_W) * proj(h_ext, c0 + 2 * CONV_W, c0 + 3 * CONV_W)
    gb = proj(h, c0, c0 + CONV_W)
    cu = cu_all[0:tm]
    prev_row = jnp.where(i > 0, cu_all[tm + F32_SUBLANES - 1:tm + F32_SUBLANES], 0.0)
    next_row = jnp.where(i < n_tiles - 1, cu_all[tm + F32_SUBLANES:tm + F32_SUBLANES + 1], 0.0)
    row = lax.broadcasted_iota(jnp.int32, cu.shape, 0)
    pos = (i * tm + row) & (seq_len - 1)
    prev = jnp.where(row == 0, prev_row, pltpu.roll(cu, 1, axis=0))
    nxt = jnp.where(row == tm - 1, next_row, pltpu.roll(cu, tm - 1, axis=0))
    prev = jnp.where(pos == 0, 0.0, prev)
    nxt = jnp.where(pos == seq_len - 1, 0.0, nxt)
    conv = cw_ref[0:1, :] * prev + cw_ref[1:2, :] * cu + cw_ref[2:3, :] * nxt
    yc_ref[...] = (gb * conv).astype(yc_ref.dtype)

    def per_head(z, ref, finish):
        for hd in range(N_HEADS):
            t = finish(z[:, hd * D_V:(hd + 1) * D_V]).astype(ref.dtype)
            for sq in range(spt):
                ref[sq, hd] = t[sq * per_seq:(sq + 1) * per_seq]

    per_head(proj(h, 0, ATTN_W), q_ref, lambda t: (rot(t) if rope else t) * QK_SCALE)
    per_head(proj(h, ATTN_W, 2 * ATTN_W), k_ref, lambda t: rot(t) if rope else t)
    per_head(proj(h, 2 * ATTN_W, 3 * ATTN_W), v_ref, lambda t: t)


def _rope_tables(seq):
    n_rows = seq // GRID_W
    row = np.repeat(np.arange(n_rows), GRID_W).astype(np.float64)
    col = np.tile(np.arange(GRID_W), n_rows).astype(np.float64)
    nf = D_QK // 4
    inv = ROPE_BASE ** (-np.arange(nf, dtype=np.float64) / nf)
    ar = row[:, None] * inv
    ac = col[:, None] * inv
    ang = np.concatenate([ar, ar, ac, ac], axis=-1)
    ang = np.concatenate([ang, ang], axis=-1)
    first_half = (np.arange(LANES) % 32) < 16
    cos = np.cos(ang)
    sin = np.sin(ang)
    sina = np.where(first_half, -sin, 0.0)
    sinb = np.where(first_half, 0.0, sin)
    return tuple(jnp.asarray(t, dtype=jnp.float32) for t in (cos, sina, sinb))


def _in_proj_call(x, mod, mod_per_batch, g_pre1, w_in, conv_w, rope, kv_dtype):
    n_seq, seq_len, _ = x.shape
    assert seq_len & (seq_len - 1) == 0
    spt = 1 if (mod_per_batch or seq_len >= TOKEN_TILE) else TOKEN_TILE // seq_len
    x = x.reshape(n_seq // spt, spt * seq_len, D_MODEL)
    bsz, seq, _ = x.shape
    tm = min(TOKEN_TILE, seq)
    n_tiles = seq // tm
    halo = tm // F32_SUBLANES
    n_halo = seq // F32_SUBLANES

    def mod_spec(chunk):
        if mod_per_batch:
            return pl.BlockSpec((None, 1, D_MODEL), lambda b, i: (b, 0, chunk))
        return pl.BlockSpec((None, 1, D_MODEL), lambda b, i: (0, 0, chunk))

    in_specs = [
        pl.BlockSpec((None, tm, D_MODEL), lambda b, i: (b, i, 0)),
        pl.BlockSpec((None, F32_SUBLANES, D_MODEL), lambda b, i: (b, jnp.maximum(i * halo - 1, 0), 0)),
        pl.BlockSpec((None, F32_SUBLANES, D_MODEL), lambda b, i: (b, jnp.minimum((i + 1) * halo, n_halo - 1), 0)),
        mod_spec(0), mod_spec(1),
        pl.BlockSpec((1, D_MODEL), lambda b, i: (0, 0)),
        pl.BlockSpec((D_MODEL, IN_W), lambda b, i: (0, 0)),
        pl.BlockSpec((3, CONV_W), lambda b, i: (0, 0)),
    ]
    args = [x, x, x, mod, mod, g_pre1, w_in, conv_w]
    if rope:
        in_specs += [pl.BlockSpec((tm, LANES), lambda b, i: (i, 0))] * 3
        args += list(_rope_tables(seq))
    head_spec = pl.BlockSpec((spt, N_HEADS, tm // spt, D_V), lambda b, i: (b, 0, i, 0))
    head_shape = lambda dt: jax.ShapeDtypeStruct((n_seq, N_HEADS, seq_len, D_V), dt)
    q, k, v, yc = pl.pallas_call(
        functools.partial(_in_proj_kernel, rope, n_tiles, seq_len),
        grid=(bsz, n_tiles),
        in_specs=in_specs,
        out_specs=[head_spec, head_spec, head_spec, pl.BlockSpec((None, tm, CONV_W), lambda b, i: (b, i, 0))],
        out_shape=[head_shape(jnp.bfloat16), head_shape(kv_dtype), head_shape(kv_dtype),
                   jax.ShapeDtypeStruct((bsz, seq, CONV_W), jnp.bfloat16)],
        name="in_proj_rope" if rope else "in_proj",
    )(*args)
    return q, k, v, yc.reshape(n_seq, seq_len, CONV_W)


def _attn_kernel(n_kv, lam_ref, g_ref, q_ref, *rest):
    kv_refs = rest[:2 * n_kv]
    o_ref = rest[2 * n_kv]
    seqs, heads, tq, _ = q_ref.shape
    for sq in range(seqs):
        _attend(sq, heads, tq, lam_ref, g_ref, q_ref, kv_refs, o_ref)


def _attend(sq, heads, tq, lam_ref, g_ref, q_ref, kv_refs, o_ref):
    n_kv = len(kv_refs) // 2
    lane = lax.broadcasted_iota(jnp.int32, (tq, 2 * D_QK), 1)
    for hh in range(heads):
        q = q_ref[sq, hh]
        zero = jnp.zeros_like(q)
        halves = (jnp.where(lane < D_QK, q, zero), jnp.where(lane >= D_QK, q, zero))
        acc = [jnp.zeros((tq, 2 * D_V), jnp.float32) for _ in range(2)]
        m = [jnp.full((tq, 1), -1e30, jnp.float32) for _ in range(2)]
        for j in range(n_kv):
            k_ref, v_ref = kv_refs[2 * j], kv_refs[2 * j + 1]
            n_keys = k_ref.shape[2]
            ck = min(KEY_CHUNK, n_keys)
            ones = jnp.ones((ck, D_V), jnp.bfloat16)
            for c in range(n_keys // ck):
                k = k_ref[sq, hh, c * ck:(c + 1) * ck, :].astype(jnp.bfloat16)
                v1 = jnp.concatenate([v_ref[sq, hh, c * ck:(c + 1) * ck, :].astype(jnp.bfloat16), ones], axis=1)
                for x, qh in enumerate(halves):
                    s = lax.dot_general(qh, k, NT_DIMS, preferred_element_type=jnp.float32)
                    m_new = jnp.maximum(m[x], jnp.max(s, axis=-1, keepdims=True))
                    p = jnp.exp2(s - m_new).astype(jnp.bfloat16)
                    acc[x] = (jnp.exp2(m[x] - m_new) * acc[x]
                              + jnp.dot(p, v1, preferred_element_type=jnp.float32))
                    m[x] = m_new
        o = (acc[0][:, 0:D_V] / acc[0][:, D_V:2 * D_V]
             - lam_ref[0:1, 0:1] * (acc[1][:, 0:D_V] / acc[1][:, D_V:2 * D_V]))
        o_ref[sq, :, hh * D_V:(hh + 1) * D_V] = (_rms(o) * (g_ref[...] * (1.0 - LAM_INIT))).astype(o_ref.dtype)


def _attn_call(lam, subln_g, q, kvs, seqs_per_step, heads_per_step):
    bsz, _, seq, _ = q.shape
    tq = min(Q_TILE, seq)
    sp, hp = seqs_per_step, heads_per_step
    in_specs = [
        pl.BlockSpec((1, LANES), lambda b, h, i: (0, 0)),
        pl.BlockSpec((1, D_V), lambda b, h, i: (0, 0)),
        pl.BlockSpec((sp, hp, tq, D_V), lambda b, h, i: (b, h, i, 0)),
    ]
    args = [lam, subln_g, q]
    for k, v, spec_fn in kvs:
        in_specs += [spec_fn(sp, hp), spec_fn(sp, hp)]
        args += [k, v]
    return pl.pallas_call(
        functools.partial(_attn_kernel, len(kvs)),
        grid=(bsz // sp, N_HEADS // hp, seq // tq),
        in_specs=in_specs,
        out_specs=pl.BlockSpec((sp, tq, hp * D_V), lambda b, h, i: (b, i, h)),
        out_shape=jax.ShapeDtypeStruct((bsz, seq, ATTN_W), jnp.bfloat16),
        name="attn%d" % len(kvs),
    )(*args)


def _route(logits_t):
    lg = [logits_t[g:g + 1, :] for g in range(N_GROUPS)]
    mg = functools.reduce(jnp.maximum, lg)
    p_sel = 1.0 / functools.reduce(jnp.add, [jnp.exp(t - mg) for t in lg])
    g_sel = jnp.full(mg.shape, N_GROUPS - 1, jnp.int32)
    for g in range(N_GROUPS - 2, -1, -1):
        g_sel = jnp.where(lg[g] == mg, g, g_sel)
    le = []
    for j in range(EXP_PER_GROUP):
        t = jnp.zeros_like(mg)
        for g in range(N_GROUPS):
            r = EXPERT_LOGIT_ROW0 + g * EXP_PER_GROUP + j
            t = jnp.where(g_sel == g, logits_t[r:r + 1, :], t)
        le.append(t)
    m1 = functools.reduce(jnp.maximum, le)
    i1 = jnp.full(mg.shape, EXP_PER_GROUP - 1, jnp.int32)
    for j in range(EXP_PER_GROUP - 2, -1, -1):
        i1 = jnp.where(le[j] == m1, j, i1)
    neg = jnp.float32(-jnp.inf)
    rest = [jnp.where(i1 == j, neg, le[j]) for j in range(EXP_PER_GROUP)]
    m2 = functools.reduce(jnp.maximum, rest)
    i2 = jnp.full(mg.shape, EXP_PER_GROUP - 1, jnp.int32)
    for j in range(EXP_PER_GROUP - 2, -1, -1):
        i2 = jnp.where(rest[j] == m2, j, i2)
    e2 = jnp.exp(m2 - m1)
    w1 = p_sel / (1.0 + e2)
    w2 = p_sel * e2 / (1.0 + e2)
    base = g_sel * EXP_PER_GROUP
    return (base + i1).astype(jnp.float32), (base + i2).astype(jnp.float32), w1, w2


def _out_proj_kernel(o_ref, yc_ref, wo_ref, x_ref, gate1_ref, shift2_ref, scale2_ref, gpost_ref, gpre_ref,
                     wr_ref, *rest):
    x1_ref, h2_ref, slots_ref, slots_t_ref, counts_ref = rest[-5:]
    tm = o_ref.shape[0]
    sub = min(OUT_SUB_ROWS, tm)
    gain1 = gate1_ref[...] * gpost_ref[...]
    gain2 = gpre_ref[...] * (1.0 + scale2_ref[...])
    routes = []
    for r in range(tm // sub):
        rows = slice(r * sub, (r + 1) * sub)
        out = (jnp.dot(o_ref[rows, :], wo_ref[0:ATTN_W, :], preferred_element_type=jnp.float32)
               + jnp.dot(yc_ref[rows, :], wo_ref[ATTN_W:D_MODEL, :], preferred_element_type=jnp.float32))
        x1 = x_ref[rows, :] + _rms(out) * gain1
        x1_ref[rows, :] = x1
        h2 = (_rms(x1) * gain2 + shift2_ref[...]).astype(jnp.bfloat16)
        h2_ref[rows, :] = h2
        logits = jnp.dot(h2, wr_ref[...], preferred_element_type=jnp.float32)
        routes.append(_route(logits.T))
    per_tile = DISPATCH_TILE // sub
    for d in range(tm // DISPATCH_TILE):
        e1, e2, w1, w2 = (jnp.concatenate(parts, axis=1)
                          for parts in zip(*routes[d * per_tile:(d + 1) * per_tile]))
        _plan(e1, e2, w1, w2, slots_ref.at[d], slots_t_ref.at[d], counts_ref.at[d])


def _out_proj_call(o, yc, w_o, x, mod, mod_per_batch, g_post1, g_pre2, w_router,
                   total_tokens, token_offset, carried):
    bsz, seq, _ = x.shape
    tm = OUT_TILES * DISPATCH_TILE
    assert seq % tm == 0 and token_offset % tm == 0 and total_tokens % tm == 0
    n_tiles = seq // tm
    tile0 = token_offset // tm
    n_all = total_tokens // DISPATCH_TILE

    def mod_spec(chunk):
        if mod_per_batch:
            return pl.BlockSpec((None, 1, D_MODEL), lambda b, i: (b, 0, chunk))
        return pl.BlockSpec((None, 1, D_MODEL), lambda b, i: (0, 0, chunk))

    def vec_spec():
        return pl.BlockSpec((1, D_MODEL), lambda b, i: (0, 0))

    tok = lambda w: pl.BlockSpec((None, tm, w), lambda b, i: (b, i, 0))
    flat = lambda w: pl.BlockSpec((tm, w), lambda b, i: (tile0 + b * n_tiles + i, 0))
    per_tile = lambda r, c: pl.BlockSpec((OUT_TILES, r, c), lambda b, i: (tile0 + b * n_tiles + i, 0, 0))
    in_specs = [
        tok(ATTN_W), tok(CONV_W),
        pl.BlockSpec((D_MODEL, D_MODEL), lambda b, i: (0, 0)),
        tok(D_MODEL),
        mod_spec(2), mod_spec(3), mod_spec(4),
        vec_spec(), vec_spec(),
        pl.BlockSpec((D_MODEL, LANES), lambda b, i: (0, 0)),
    ]
    args = [o, yc, w_o, x, mod, mod, mod, g_post1, g_pre2, w_router]
    aliases = {}
    if carried is not None:
        for j, arr in enumerate(carried):
            aliases[len(args)] = j
            in_specs.append(pl.BlockSpec(memory_space=pl.ANY))
            args.append(arr)
    return pl.pallas_call(
        _out_proj_kernel,
        grid=(bsz, n_tiles),
        in_specs=in_specs,
        out_specs=[flat(D_MODEL), flat(D_MODEL), per_tile(8, DISPATCH_TILE), per_tile(DISPATCH_TILE, LANES),
                   per_tile(N_EXPERTS, LANES)],
        out_shape=[
            jax.ShapeDtypeStruct((total_tokens, D_MODEL), jnp.float32),
            jax.ShapeDtypeStruct((total_tokens, D_MODEL), jnp.bfloat16),
            jax.ShapeDtypeStruct((n_all, 8, DISPATCH_TILE), jnp.float32),
            jax.ShapeDtypeStruct((n_all, DISPATCH_TILE, LANES), jnp.float32),
            jax.ShapeDtypeStruct((n_all, N_EXPERTS, LANES), jnp.float32),
        ],
        input_output_aliases=aliases,
        name="out_proj",
    )(*args)


def _plan(e1, e2, w1, w2, slots_ref, slots_t_ref, counts_ref):
    t = e1.shape[1]
    e1 = e1.astype(jnp.int32)
    e2 = e2.astype(jnp.int32)
    eid = lax.broadcasted_iota(jnp.int32, (N_EXPERTS, t), 0)
    hot1 = jnp.where(eid == e1, 1.0, 0.0)
    hot2 = jnp.where(eid == e2, 1.0, 0.0)
    hot = jnp.concatenate([hot1, hot2], axis=0).astype(jnp.bfloat16)
    before = (lax.broadcasted_iota(jnp.int32, (t, t), 0) < lax.broadcasted_iota(jnp.int32, (t, t), 1))
    before = jnp.where(before, 1.0, 0.0).astype(jnp.bfloat16)
    rank = jnp.dot(hot, before, preferred_element_type=jnp.float32)
    n1 = jnp.sum(hot1, axis=1, keepdims=True)
    n2 = jnp.sum(hot2, axis=1, keepdims=True)
    ones = jnp.ones((8, t), jnp.bfloat16)
    cnt_row = lax.dot_general(ones, (hot1 + hot2).astype(jnp.bfloat16), NT_DIMS,
                              preferred_element_type=jnp.float32)[0:1, :]
    pad_row = jnp.floor((cnt_row + (CHUNK - 1)) * (1.0 / CHUNK)) * CHUNK
    lower = (lax.broadcasted_iota(jnp.int32, (N_EXPERTS, N_EXPERTS), 1)
             < lax.broadcasted_iota(jnp.int32, (N_EXPERTS, N_EXPERTS), 0))
    base = jnp.sum(jnp.where(lower, pad_row, 0.0), axis=1, keepdims=True)
    slot1 = jnp.sum(hot1 * (base + rank[:N_EXPERTS]), axis=0, keepdims=True)
    slot2 = jnp.sum(hot2 * (base + n1 + rank[N_EXPERTS:]), axis=0, keepdims=True)
    slots_ref[...] = jnp.concatenate([slot1, slot2, jnp.zeros((6, t), jnp.float32)], axis=0)
    wide = jnp.concatenate([slot1, slot2, w1, w2, jnp.zeros((LANES - 4, t), jnp.float32)], axis=0)
    slots_t_ref[...] = wide.T
    counts_ref[...] = jnp.broadcast_to(n1 + n2, counts_ref.shape)


def _layout_tables(counts, n_blocks):
    cnt = counts[:, :, 0].astype(jnp.int32)
    n_tiles = cnt.shape[0]
    nch = (cnt + (CHUNK - 1)) // CHUNK
    csum = jnp.cumsum(nch, axis=0)
    rows = CHUNK * csum[-1]
    blocks = (rows + (EXPERT_BLOCK - 1)) // EXPERT_BLOCK
    blk_end = jnp.cumsum(blocks)
    blk_start = blk_end - blocks
    off = blk_start[None, :] * EXPERT_BLOCK + CHUNK * (csum - nch)
    first = jnp.cumsum(nch, axis=1) - nch
    used = jnp.sum(nch, axis=1)
    c = jnp.arange(N_CHUNKS, dtype=jnp.int32)
    e_idx = jnp.sum((c[None, :, None] >= first[:, None, :]).astype(jnp.int32), axis=2) - 1
    hot = e_idx[:, :, None] == jnp.arange(N_EXPERTS, dtype=jnp.int32)[None, None, :]
    pick = lambda a: jnp.sum(jnp.where(hot, a[:, None, :], 0), axis=2)
    row = pick(off) + CHUNK * (c[None, :] - pick(first))
    valid = c[None, :] < used[:, None]
    spare = (n_blocks * EXPERT_BLOCK + (jnp.arange(n_tiles, dtype=jnp.int32) % SPARE_SETS)[:, None] * TILE_SLOTS
             + CHUNK * c[None, :])
    scatter_rows = jnp.where(valid, row, spare)
    gather_rows = jnp.where(valid, row, 0)
    n_used = blk_end[-1]
    bc = jnp.minimum(jnp.arange(n_blocks, dtype=jnp.int32), n_used - 1)
    block_expert = jnp.sum((bc[:, None] >= blk_end[None, :]).astype(jnp.int32), axis=1)
    tail_start = blk_start * EXPERT_BLOCK + rows
    tail_chunks = (blocks * EXPERT_BLOCK - rows) // CHUNK
    flat = lambda a: a.reshape(-1).astype(jnp.int32)
    return (flat(scatter_rows // CHUNK), flat(gather_rows // CHUNK), flat(tail_start // CHUNK), flat(tail_chunks),
            flat(block_expert), flat(n_used))


def _chunk_copies(tile, chunk_ref, make_copy, action):
    for c in range(N_CHUNKS):
        action(make_copy(c, chunk_ref[tile * N_CHUNKS + c]))


def _wait_chunk_copies(make_copy):
    for _ in range(N_CHUNKS):
        make_copy(0, 0).wait()


def _dispatch_kernel(rows_ref, tail_start_ref, tail_chunks_ref, h_ref, slots_ref, xs_ref, buf, zbuf, sem, zsem):
    i = pl.program_id(0)
    n = pl.num_programs(0)
    slot = i % 2
    t = DISPATCH_TILE

    def copies(sl, d):
        def make(src_piece, dst_piece):
            return pltpu.make_async_copy(buf.at[sl, d, src_piece], xs_ref.at[dst_piece], sem.at[sl])
        return make

    for d in range(MOE_TILES):
        s = slots_ref[d]
        s1 = s[0:1, :].astype(jnp.int32)
        s2 = s[1:2, :].astype(jnp.int32)
        h = h_ref[d * t:(d + 1) * t, :]
        for g in range(TILE_SLOTS // SLOT_GROUP):
            rid = lax.broadcasted_iota(jnp.int32, (SLOT_GROUP, t), 0) + g * SLOT_GROUP
            perm = jnp.where(rid == s1, 1.0, jnp.where(rid == s2, 1.0, 0.0)).astype(jnp.bfloat16)
            rows = jnp.dot(perm, h, preferred_element_type=jnp.float32)
            pieces = slice(g * GROUP_CHUNKS, (g + 1) * GROUP_CHUNKS)
            buf[slot, d, pieces] = rows.astype(jnp.bfloat16).reshape(GROUP_CHUNKS, CHUNK, D_MODEL)
        _chunk_copies(i * MOE_TILES + d, rows_ref, copies(slot, d), lambda c: c.start())

    @pl.when(i > 0)
    def _():
        for d in range(MOE_TILES):
            _wait_chunk_copies(copies(1 - slot, d))

    @pl.when(i == n - 1)
    def _():
        zbuf[...] = jnp.zeros_like(zbuf)

        def tails(action):
            for e in range(N_EXPERTS):
                start = tail_start_ref[e]

                def body(m, carry, start=start):
                    action(pltpu.make_async_copy(zbuf, xs_ref.at[start + m], zsem))
                    return carry

                lax.fori_loop(0, tail_chunks_ref[e], body, 0)

        tails(lambda c: c.start())
        tails(lambda c: c.wait())
        for d in range(MOE_TILES):
            _wait_chunk_copies(copies(slot, d))


def _dispatch_call(tables, h2, slots, n_blocks):
    scatter_rows, tail_start, tail_chunks = tables
    tokens = h2.shape[0]
    t = MOE_TILES * DISPATCH_TILE
    return pl.pallas_call(
        _dispatch_kernel,
        grid_spec=pltpu.PrefetchScalarGridSpec(
            num_scalar_prefetch=3,
            grid=(tokens // t,),
            in_specs=[
                pl.BlockSpec((t, D_MODEL), lambda i, *_: (i, 0)),
                pl.BlockSpec((MOE_TILES, 8, DISPATCH_TILE), lambda i, *_: (i, 0, 0)),
            ],
            out_specs=pl.BlockSpec(memory_space=pl.ANY),
            scratch_shapes=[
                pltpu.VMEM((2, MOE_TILES, N_CHUNKS, CHUNK, D_MODEL), jnp.bfloat16),
                pltpu.VMEM((CHUNK, D_MODEL), jnp.bfloat16),
                pltpu.SemaphoreType.DMA((2,)),
                pltpu.SemaphoreType.DMA(()),
            ],
        ),
        out_shape=jax.ShapeDtypeStruct(((n_blocks * EXPERT_BLOCK + SPARE_SETS * TILE_SLOTS) // CHUNK, CHUNK, D_MODEL),
                                       jnp.bfloat16),
        name="moe_dispatch",
    )(scatter_rows, tail_start, tail_chunks, h2, slots)


def _expert_kernel(bexp_ref, nused_ref, xs_ref, wg_ref, wu_ref, wd_ref, ys_ref, wgu_b, wd_b):
    b = pl.program_id(0)
    e = bexp_ref[b]
    prev = bexp_ref[jnp.maximum(b - 1, 0)]

    @pl.when((b == 0) | (e != prev))
    def _():
        wgu_b[:, 0:D_EXPERT] = wg_ref[...].astype(jnp.bfloat16)
        wgu_b[:, D_EXPERT:2 * D_EXPERT] = wu_ref[...].astype(jnp.bfloat16)
        wd_b[...] = wd_ref[...].astype(jnp.bfloat16)

    @pl.when(b < nused_ref[0])
    def _():
        x = xs_ref[...]
        y = None
        half = D_EXPERT // 2
        for c in range(2):
            cols = slice(c * half, (c + 1) * half)
            g = jnp.dot(x, wgu_b[:, c * half:(c + 1) * half], preferred_element_type=jnp.float32)
            u = jnp.dot(x, wgu_b[:, D_EXPERT + c * half:D_EXPERT + (c + 1) * half],
                        preferred_element_type=jnp.float32)
            act = (_silu(g) * u).astype(jnp.bfloat16)
            t = jnp.dot(act, wd_b[cols, :], preferred_element_type=jnp.float32)
            y = t if y is None else y + t
        ys_ref[...] = y.astype(jnp.bfloat16)


def _expert_call(block_expert, n_used, xs, w_eg, w_eu, w_ed, n_blocks):
    row_blk = lambda w: pl.BlockSpec((EXPERT_BLOCK, w), lambda b, be, nu: (jnp.minimum(b, nu[0] - 1), 0))
    wt_blk = lambda r, c: pl.BlockSpec((None, r, c), lambda b, be, nu: (be[b], 0, 0))
    return pl.pallas_call(
        _expert_kernel,
        grid_spec=pltpu.PrefetchScalarGridSpec(
            num_scalar_prefetch=2,
            grid=(n_blocks,),
            in_specs=[
                row_blk(D_MODEL),
                wt_blk(D_MODEL, D_EXPERT), wt_blk(D_MODEL, D_EXPERT), wt_blk(D_EXPERT, D_MODEL),
            ],
            out_specs=row_blk(D_MODEL),
            scratch_shapes=[
                pltpu.VMEM((D_MODEL, 2 * D_EXPERT), jnp.bfloat16),
                pltpu.VMEM((D_EXPERT, D_MODEL), jnp.bfloat16),
            ],
        ),
        out_shape=jax.ShapeDtypeStruct((n_blocks * EXPERT_BLOCK, D_MODEL), jnp.bfloat16),
        name="moe_experts",
    )(block_expert, n_used, xs, w_eg, w_eu, w_ed)


def _combine_kernel(tile0, rows_ref, ys_ref, slots_t_ref, x1_ref, gate2_ref, gpost_ref, out_ref, buf, sem):
    j = pl.program_id(0)
    n = pl.num_programs(0)
    slot = j % 2
    tile = tile0 + j * MOE_TILES

    def copies(sl, d):
        def make(dst_piece, src_piece):
            return pltpu.make_async_copy(ys_ref.at[src_piece], buf.at[sl, d, dst_piece], sem.at[sl])
        return make

    def fetch(first_tile, sl):
        for d in range(MOE_TILES):
            _chunk_copies(first_tile + d, rows_ref, copies(sl, d), lambda c: c.start())

    @pl.when(j == 0)
    def _():
        fetch(tile, slot)

    @pl.when(j + 1 < n)
    def _():
        fetch(tile + MOE_TILES, 1 - slot)

    for d in range(MOE_TILES):
        _wait_chunk_copies(copies(slot, d))
    gain = gate2_ref[...] * gpost_ref[...]
    lane = lax.broadcasted_iota(jnp.int32, (COMBINE_SUB_ROWS, TILE_SLOTS), 1)
    per_tile = DISPATCH_TILE // COMBINE_SUB_ROWS
    for d in range(MOE_TILES):
        sorted_rows = buf[slot, d].reshape(TILE_SLOTS, D_MODEL)
        for r in range(per_tile):
            rows = slice((d * per_tile + r) * COMBINE_SUB_ROWS, (d * per_tile + r + 1) * COMBINE_SUB_ROWS)
            local = slice(r * COMBINE_SUB_ROWS, (r + 1) * COMBINE_SUB_ROWS)
            s1 = slots_t_ref[d, local, 0:1].astype(jnp.int32)
            s2 = slots_t_ref[d, local, 1:2].astype(jnp.int32)
            unperm = jnp.where(lane == s1, slots_t_ref[d, local, 2:3],
                               jnp.where(lane == s2, slots_t_ref[d, local, 3:4], 0.0)).astype(jnp.bfloat16)
            y = jnp.dot(unperm, sorted_rows, preferred_element_type=jnp.float32)
            out_ref[rows, :] = x1_ref[rows, :] + _rms(y) * gain


def _combine_call(gather_rows, ys, slots_t, x1, mod, mod_per_batch, g_post2, token_offset, bsz, seq):
    t = MOE_TILES * DISPATCH_TILE
    tile0 = token_offset // DISPATCH_TILE
    step0 = token_offset // t
    tokens = bsz * seq
    assert tokens % t == 0 and token_offset % t == 0 and (seq % t == 0 or not mod_per_batch)
    if mod_per_batch:
        per = seq // t
        gate2_spec = pl.BlockSpec((None, 1, D_MODEL), lambda j, *_: (j // per, 0, 5))
    else:
        gate2_spec = pl.BlockSpec((None, 1, D_MODEL), lambda j, *_: (0, 0, 5))
    y = pl.pallas_call(
        functools.partial(_combine_kernel, tile0),
        grid_spec=pltpu.PrefetchScalarGridSpec(
            num_scalar_prefetch=1,
            grid=(tokens // t,),
            in_specs=[
                pl.BlockSpec(memory_space=pl.ANY),
                pl.BlockSpec((MOE_TILES, DISPATCH_TILE, LANES), lambda j, *_: (step0 + j, 0, 0)),
                pl.BlockSpec((t, D_MODEL), lambda j, *_: (step0 + j, 0)),
                gate2_spec,
                pl.BlockSpec((1, D_MODEL), lambda j, *_: (0, 0)),
            ],
            out_specs=pl.BlockSpec((t, D_MODEL), lambda j, *_: (j, 0)),
            scratch_shapes=[
                pltpu.VMEM((2, MOE_TILES, N_CHUNKS, CHUNK, D_MODEL), jnp.bfloat16),
                pltpu.SemaphoreType.DMA((2,)),
            ],
        ),
        out_shape=jax.ShapeDtypeStruct((tokens, D_MODEL), jnp.float32),
        name="moe_combine",
    )(gather_rows, ys.reshape(-1, CHUNK, D_MODEL), slots_t, x1, mod, g_post2)
    return y.reshape(bsz, seq, D_MODEL)


def kernel(x_prompt, x_sample, cache_k, cache_v, c, c_ctx, w_mod, b_mod, g_pre1, g_post1, g_pre2, g_post2,
           w_in, conv_w, lambda_q1, lambda_k1, lambda_q2, lambda_k2, subln_g, w_o, w_router_group,
           w_router_expert, w_exp_gate, w_exp_up, w_exp_down):
    n_lat = c.shape[0]
    cond = jnp.concatenate(
        [c, c_ctx[None, :], jnp.zeros((MOD_ROWS - n_lat - 1, D_MODEL), jnp.float32)], axis=0)
    mod, lam = _mod_call(cond, w_mod[0], b_mod, lambda_q1, lambda_k1, lambda_q2, lambda_k2)
    mod = mod.reshape(MOD_ROWS, 1, 6 * D_MODEL)
    mod_lat, mod_ctx = mod[:n_lat], mod[n_lat:n_lat + 1]

    w_in_b = w_in[0].astype(jnp.bfloat16)
    w_o_b = w_o[0].astype(jnp.bfloat16)
    w_router = jnp.concatenate(
        [w_router_group[0], jnp.zeros((D_MODEL, EXPERT_LOGIT_ROW0 - N_GROUPS), jnp.float32), w_router_expert[0],
         jnp.zeros((D_MODEL, LANES - EXPERT_LOGIT_ROW0 - N_EXPERTS), jnp.float32)], axis=1).astype(jnp.bfloat16)

    n_prompt = x_prompt.shape[0] * x_prompt.shape[1]
    n_sample = x_sample.shape[0] * x_sample.shape[1]
    total = n_prompt + n_sample

    def mixer(x, mod_x, per_batch, rope, ctx_kv, token_offset, carried):
        kv_dtype = jnp.bfloat16 if rope else jnp.float32
        q, k, v, yc = _in_proj_call(x, mod_x, per_batch, g_pre1, w_in_b, conv_w[0], rope, kv_dtype)
        kvs = []
        if ctx_kv is not None:
            ck, cv = ctx_kv
            n_ctx = ck.shape[3]
            kvs.append((ck, cv, lambda sp, hp: pl.BlockSpec((sp, None, hp, n_ctx, D_V),
                                                            lambda b, h, i: (b, 0, h, 0, 0))))
        seq = x.shape[1]
        kvs.append((k, v, lambda sp, hp: pl.BlockSpec((sp, hp, seq, D_V), lambda b, h, i: (b, h, 0, 0))))
        short = seq <= MERGE_HEADS_MAX_SEQ
        o = _attn_call(lam, subln_g, q, kvs, SHORT_SEQS_PER_STEP if short else 1,
                       N_HEADS if short else LONG_SEQ_HEADS_PER_STEP)
        if not per_batch:
            o, yc, x = (a.reshape(1, -1, a.shape[-1]) for a in (o, yc, x))
        shared = _out_proj_call(o, yc, w_o_b, x, mod_x, per_batch, g_post1, g_pre2, w_router,
                                total, token_offset, carried)
        return shared, k, v

    shared, kp, vp = mixer(x_prompt, mod_ctx, False, False, None, 0, None)
    (x1, h2, slots, slots_t, counts), _, _ = mixer(x_sample, mod_lat, True, True, (cache_k, cache_v),
                                                   n_prompt, shared)

    n_tiles = total // DISPATCH_TILE
    max_rows = 2 * total + n_tiles * N_EXPERTS * (CHUNK - 1) + N_EXPERTS * (EXPERT_BLOCK - CHUNK)
    n_blocks = -(-max_rows // EXPERT_BLOCK)
    scatter_rows, gather_rows, tail_start, tail_chunks, block_expert, n_used = _layout_tables(counts, n_blocks)
    xs = _dispatch_call((scatter_rows, tail_start, tail_chunks), h2, slots, n_blocks)
    ys = _expert_call(block_expert, n_used, xs.reshape(-1, D_MODEL), w_exp_gate[0], w_exp_up[0], w_exp_down[0],
                      n_blocks)
    yp = _combine_call(gather_rows, ys, slots_t, x1, mod_ctx, False, g_post2, 0,
                       x_prompt.shape[0], x_prompt.shape[1])
    ysamp = _combine_call(gather_rows, ys, slots_t, x1, mod_lat, True, g_post2, n_prompt,
                          x_sample.shape[0], x_sample.shape[1])
    return yp, ysamp, kp[:, None], vp[:, None]
```
